```python
import math
import jax
import jax.numpy as jnp
from jax import lax
import numpy as np

D_MODEL = 1024
BATCH = 2
SEQ = 8192
DEPTH = 4

GRID_W = 64
CTX_LEN = 256
N_MIXERS = 3
EPS = 1e-6
N_MOD = 6

HEAD_DIM = 128
N_HEADS = D_MODEL // HEAD_DIM
N_KV_HEADS = max(1, N_HEADS // 4)
Q_GROUP = N_HEADS // N_KV_HEADS
Q_BLOCK = 128
ROPE_AXIS_DIM = HEAD_DIM // 2
ROPE_THETA = 10000.0

SSD_D_INNER = 2 * D_MODEL
SSD_HEAD_DIM = 64
SSD_N_HEADS = SSD_D_INNER // SSD_HEAD_DIM
SSD_N_GROUPS = 4
SSD_HPG = SSD_N_HEADS // SSD_N_GROUPS
SSD_D_STATE = 128
SSD_CONV = 3
SSD_CHUNK = 128
SSD_CONV_CH = SSD_D_INNER + 2 * SSD_N_GROUPS * SSD_D_STATE
SSD_IN_DIM = SSD_D_INNER + SSD_CONV_CH + 2 * SSD_N_HEADS
DT_MIN = 1e-3
DT_MAX = 0.1

CMLP_CHUNK = 128
CMLP_D = 2 * D_MODEL
CMLP_GROUPS = 8
CMLP_GROUP_W = CMLP_D // CMLP_GROUPS

D_FF = 7 * D_MODEL // 2
N_EXPERTS = 8
TOP_K = 2
D_FF_EXPERT = 7 * D_MODEL // 2

kernel_name = 'hybrid_diffusion_gqa_ssd_gmlp_moe'


def _rms_norm(x, g):
    xf = x.astype(jnp.float32)
    y = xf * lax.rsqrt(jnp.mean(xf * xf, axis=-1, keepdims=True) + EPS)
    return (y * g.astype(jnp.float32)).astype(x.dtype)


def _modulate(h, shift, scale):
    return h * (1 + scale) + shift


def _axial_rope_tables(n_tok):
    rows = n_tok // GRID_W
    row_ids = jnp.broadcast_to(jnp.arange(rows)[:, None], (rows, GRID_W)).reshape(-1)
    col_ids = jnp.broadcast_to(jnp.arange(GRID_W)[None, :], (rows, GRID_W)).reshape(-1)
    inv_freq = 1.0 / (ROPE_THETA ** (jnp.arange(0, ROPE_AXIS_DIM, 2, dtype=jnp.float32) / ROPE_AXIS_DIM))
    ang_r = row_ids.astype(jnp.float32)[:, None] * inv_freq
    ang_c = col_ids.astype(jnp.float32)[:, None] * inv_freq
    return (jnp.cos(ang_r), jnp.sin(ang_r), jnp.cos(ang_c), jnp.sin(ang_c))


def _rope_1d(x, cos, sin):
    x1, x2 = jnp.split(x, 2, axis=-1)
    cos = cos[:, None, :]
    sin = sin[:, None, :]
    return jnp.concatenate([x1 * cos - x2 * sin, x2 * cos + x1 * sin], axis=-1)


def _apply_axial_rope(x, tabs):
    cos_r, sin_r, cos_c, sin_c = tabs
    xf = x.astype(jnp.float32)
    x_row, x_col = jnp.split(xf, 2, axis=-1)
    out = jnp.concatenate([_rope_1d(x_row, cos_r, sin_r), _rope_1d(x_col, cos_c, sin_c)], axis=-1)
    return out.astype(x.dtype)


def _attend(q, k, v):
    s = jnp.einsum('bqkgd,bskd->bkgqs', q, k, preferred_element_type=jnp.float32) * (HEAD_DIM ** -0.5)
    p = jax.nn.softmax(s, axis=-1).astype(v.dtype)
    return jnp.einsum('bkgqs,bskd->bqkgd', p, v)


def _attention_mixer(a_c, a_l, w_qkv, q_g, k_g, w_o, rope_tabs, need_ctx):
    def project(a):
        b, n, _ = a.shape
        qkv = a @ w_qkv
        q, k, v = jnp.split(qkv, [N_HEADS * HEAD_DIM, (N_HEADS + N_KV_HEADS) * HEAD_DIM], axis=-1)
        q = _rms_norm(q.reshape(b, n, N_HEADS, HEAD_DIM), q_g)
        k = _rms_norm(k.reshape(b, n, N_KV_HEADS, HEAD_DIM), k_g)
        v = v.reshape(b, n, N_KV_HEADS, HEAD_DIM)
        return q, k, v

    q_c, k_c, v_c = project(a_c)
    q_l, k_l, v_l = project(a_l)
    q_l = _apply_axial_rope(q_l, rope_tabs)
    k_l = _apply_axial_rope(k_l, rope_tabs)
    k_all = jnp.concatenate([k_c, k_l], axis=1)
    v_all = jnp.concatenate([v_c, v_l], axis=1)
    b, n_lat = a_l.shape[:2]
    n_blk = n_lat // Q_BLOCK
    q_blocks = q_l.reshape(b, n_blk, Q_BLOCK, N_KV_HEADS, Q_GROUP, HEAD_DIM).transpose(1, 0, 2, 3, 4, 5)
    o_l = lax.map(lambda qb: _attend(qb, k_all, v_all), q_blocks)
    o_l = o_l.transpose(1, 0, 2, 3, 4, 5).reshape(b, n_lat, N_HEADS * HEAD_DIM)
    y_l = o_l @ w_o
    y_c = None
    if need_ctx:
        n_ctx = a_c.shape[1]
        o_c = _attend(q_c.reshape(b, n_ctx, N_KV_HEADS, Q_GROUP, HEAD_DIM), k_c, v_c)
        y_c = o_c.reshape(b, n_ctx, N_HEADS * HEAD_DIM) @ w_o
    return y_c, y_l


def _dwconv_centred(x, w, b):
    pad = SSD_CONV // 2
    y = lax.conv_general_dilated(x, w[:, None, :].astype(x.dtype), window_strides=(1,),
                                 padding=[(pad, pad)], dimension_numbers=('NWC', 'WIO', 'NWC'),
                                 feature_group_count=x.shape[-1])
    return y + b


def _ssd_scan(xh, dt, A, Bm, Cm, h0):
    b, n = xh.shape[:2]
    nc = n // SSD_CHUNK
    ln = SSD_CHUNK

    def chunk(t):
        return t.reshape((b, nc, ln) + t.shape[2:])

    x_, dt_, B_, C_ = chunk(xh), chunk(dt), chunk(Bm), chunk(Cm)
    dA_cs = jnp.cumsum(dt_ * A, axis=2)
    seg = dA_cs[:, :, :, None] - dA_cs[:, :, None, :]
    lower = jnp.tril(jnp.ones((ln, ln), dtype=bool))[None, None, :, :, None, None]
    decay = jnp.exp(jnp.where(lower, seg, -jnp.inf))
    cb = jnp.einsum('bclgn,bcsgn->bclsg', C_, B_)
    y_diag = jnp.einsum('bclsg,bclsgr,bcsgrp->bclgrp', cb, decay, x_ * dt_[..., None])
    last = dA_cs[:, :, -1:]
    w_state = jnp.exp(last - dA_cs) * dt_
    states = jnp.einsum('bclgn,bclgr,bclgrp->bcgrpn', B_, w_state, x_)
    chunk_decay = jnp.exp(last[:, :, 0])

    def step(h, inp):
        dec, st = inp
        return h * dec[..., None, None] + st, h

    h_final, h_in = lax.scan(step, h0, (jnp.moveaxis(chunk_decay, 1, 0), jnp.moveaxis(states, 1, 0)))
    h_in = jnp.moveaxis(h_in, 0, 1)
    y_off = jnp.einsum('bclgn,bcgrpn,bclgr->bclgrp', C_, h_in, jnp.exp(dA_cs))
    return (y_diag + y_off).reshape(xh.shape), h_final


def _ssd_mixer(a_c, a_l, w_in, conv_w, conv_b, dt_bias_f, dt_bias_b, a_log_f, a_log_b,
               d_skip, norm_g, w_out, need_ctx):
    f32 = jnp.float32

    def project(a):
        b, n, _ = a.shape
        zxbcdt = a @ w_in
        z, xbc, dt = jnp.split(zxbcdt, [SSD_D_INNER, SSD_D_INNER + SSD_CONV_CH], axis=-1)
        xbc = jax.nn.silu(_dwconv_centred(xbc, conv_w, conv_b))
        xs, Bm, Cm = jnp.split(xbc, [SSD_D_INNER, SSD_D_INNER + SSD_N_GROUPS * SSD_D_STATE], axis=-1)
        xs = xs.reshape(b, n, SSD_N_GROUPS, SSD_HPG, SSD_HEAD_DIM).astype(f32)
        Bm = Bm.reshape(b, n, SSD_N_GROUPS, SSD_D_STATE).astype(f32)
        Cm = Cm.reshape(b, n, SSD_N_GROUPS, SSD_D_STATE).astype(f32)
        dt = dt.astype(f32)
        dt_f = jax.nn.softplus(dt[..., :SSD_N_HEADS] + dt_bias_f.astype(f32)).reshape(b, n, SSD_N_GROUPS, SSD_HPG)
        dt_b = jax.nn.softplus(dt[..., SSD_N_HEADS:] + dt_bias_b.astype(f32)).reshape(b, n, SSD_N_GROUPS, SSD_HPG)
        return z, xs, Bm, Cm, dt_f, dt_b

    A_f = -jnp.exp(a_log_f.astype(f32)).reshape(SSD_N_GROUPS, SSD_HPG)
    A_b = -jnp.exp(a_log_b.astype(f32)).reshape(SSD_N_GROUPS, SSD_HPG)
    z_c, x_c, B_c, C_c, df_c, db_c = project(a_c)
    z_l, x_l, B_l, C_l, df_l, db_l = project(a_l)
    b = a_l.shape[0]
    h0 = jnp.zeros((b, SSD_N_GROUPS, SSD_HPG, SSD_HEAD_DIM, SSD_D_STATE), f32)

    def flip(t):
        return jnp.flip(t, axis=1)

    yf_c, sf_c = _ssd_scan(x_c, df_c, A_f, B_c, C_c, h0)
    yf_l, _ = _ssd_scan(x_l, df_l, A_f, B_l, C_l, sf_c)
    yb_c, sb_c = _ssd_scan(flip(x_c), flip(db_c), A_b, flip(B_c), flip(C_c), h0)
    yb_l, _ = _ssd_scan(flip(x_l), flip(db_l), A_b, flip(B_l), flip(C_l), sb_c)
    d = d_skip.astype(f32).reshape(SSD_N_GROUPS, SSD_HPG, 1)

    def finish(y_f, y_b_rev, xs, z):
        n = xs.shape[1]
        y = (y_f + flip(y_b_rev) + xs * d).reshape(b, n, SSD_D_INNER)
        gated = (y * jax.nn.silu(z.astype(f32))).reshape(b, n, SSD_N_GROUPS, SSD_D_INNER // SSD_N_GROUPS)
        gated = gated * lax.rsqrt(jnp.mean(gated * gated, axis=-1, keepdims=True) + EPS)
        gated = gated.reshape(b, n, SSD_D_INNER) * norm_g.astype(f32)
        return gated.astype(z.dtype) @ w_out

    y_l = finish(yf_l, yb_l, x_l, z_l)
    y_c = finish(yf_c, yb_c, x_c, z_c) if need_ctx else None
    return y_c, y_l


def _chunk_mlp_mixer(a_c, a_l, w_in, b_in, v_g, w_s, b_s, w_out, need_ctx):
    def mix(a):
        b, n, _ = a.shape
        nc = n // CMLP_CHUNK
        uv = jax.nn.gelu(a @ w_in + b_in, approximate=False)
        u, v = jnp.split(uv, 2, axis=-1)
        v = _rms_norm(v, v_g).reshape(b, nc, CMLP_CHUNK, CMLP_GROUPS, CMLP_GROUP_W)
        v = jnp.einsum('gts,bcsgd->bctgd', w_s, v) + b_s.T[None, None, :, :, None]
        return (u * v.reshape(b, n, CMLP_D)) @ w_out

    y_l = mix(a_l)
    y_c = mix(a_c) if need_ctx else None
    return y_c, y_l


def _swiglu(x, w_gate, w_up, w_down):
    return (jax.nn.silu(x @ w_gate) * (x @ w_up)) @ w_down


def _moe(x, router, w_gate, w_up, w_down):
    logits = (x @ router).astype(jnp.float32)
    top_v, top_i = lax.top_k(logits, TOP_K)
    top_w = jax.nn.softmax(top_v, axis=-1)
    gates = jnp.sum(jax.nn.one_hot(top_i, N_EXPERTS, dtype=jnp.float32) * top_w[..., None], axis=-2)
    gates = gates.astype(x.dtype)
    out = jnp.zeros_like(x)
    for e in range(N_EXPERTS):
        out = out + gates[..., e:e + 1] * _swiglu(x, w_gate[e], w_up[e], w_down[e])
    return out


def setup_inputs(seed: int = 0) -> dict:
    key = jax.random.key(seed)
    keys = jax.random.split(key, 48)
    kit = iter(range(48))
    f32 = jnp.float32

    def nk():
        return keys[next(kit)]

    def normal(shape, scale):
        return scale * jax.random.normal(nk(), shape, f32)

    def dense(shape, fan_in, mult=1.0):
        return normal(shape, mult * fan_in ** -0.5)

    def gain(shape):
        return 1.0 + normal(shape, 0.02)

    def bias(shape):
        return normal(shape, 0.02)

    n_attn = len(range(0, DEPTH, N_MIXERS))
    n_ssd = len(range(1, DEPTH, N_MIXERS))
    n_cmlp = len(range(2, DEPTH, N_MIXERS))
    n_dense = len(range(0, DEPTH, 2))
    n_moe = len(range(1, DEPTH, 2))

    x = jax.random.normal(nk(), (BATCH, SEQ, D_MODEL), f32)
    c = jax.random.normal(nk(), (BATCH, D_MODEL), f32)
    ctx = jax.random.normal(nk(), (BATCH, CTX_LEN, D_MODEL), f32)
    c_ctx = jax.random.normal(nk(), (D_MODEL,), f32)

    u = jax.random.uniform(nk(), (2, n_ssd, SSD_N_HEADS), f32)
    dt0 = jnp.exp(u * (math.log(DT_MAX) - math.log(DT_MIN)) + math.log(DT_MIN))
    dt_bias = dt0 + jnp.log(-jnp.expm1(-dt0))
    a_log = jnp.log(jax.random.uniform(nk(), (2, n_ssd, SSD_N_HEADS), f32, 1.0, 16.0))

    return {
        'x': x,
        'c': c,
        'ctx': ctx,
        'c_ctx': c_ctx,
        'w_mod': dense((DEPTH, D_MODEL, N_MOD * D_MODEL), D_MODEL, 0.5),
        'b_mod': bias((DEPTH, N_MOD * D_MODEL)),
        'norm1_g': gain((DEPTH, D_MODEL)),
        'norm2_g': gain((DEPTH, D_MODEL)),
        'attn_w_qkv': dense((n_attn, D_MODEL, (N_HEADS + 2 * N_KV_HEADS) * HEAD_DIM), D_MODEL),
        'attn_q_g': gain((n_attn, HEAD_DIM)),
        'attn_k_g': gain((n_attn, HEAD_DIM)),
        'attn_w_o': dense((n_attn, N_HEADS * HEAD_DIM, D_MODEL), N_HEADS * HEAD_DIM),
        'ssd_w_in': dense((n_ssd, D_MODEL, SSD_IN_DIM), D_MODEL),
        'ssd_conv_w': dense((n_ssd, SSD_CONV, SSD_CONV_CH), SSD_CONV),
        'ssd_conv_b': bias((n_ssd, SSD_CONV_CH)),
        'ssd_dt_bias_f': dt_bias[0],
        'ssd_dt_bias_b': dt_bias[1],
        'ssd_a_log_f': a_log[0],
        'ssd_a_log_b': a_log[1],
        'ssd_d_skip': gain((n_ssd, SSD_N_HEADS)),
        'ssd_norm_g': gain((n_ssd, SSD_D_INNER)),
        'ssd_w_out': dense((n_ssd, SSD_D_INNER, D_MODEL), SSD_D_INNER),
        'cmlp_w_in': dense((n_cmlp, D_MODEL, 2 * CMLP_D), D_MODEL),
        'cmlp_b_in': bias((n_cmlp, 2 * CMLP_D)),
        'cmlp_v_g': gain((n_cmlp, CMLP_D)),
        'cmlp_w_s': dense((n_cmlp, CMLP_GROUPS, CMLP_CHUNK, CMLP_CHUNK), CMLP_CHUNK),
        'cmlp_b_s': gain((n_cmlp, CMLP_GROUPS, CMLP_CHUNK)),
        'cmlp_w_out': dense((n_cmlp, CMLP_D, D_MODEL), CMLP_D),
        'ffn_w_gate': dense((n_dense, D_MODEL, D_FF), D_MODEL),
        'ffn_w_up': dense((n_dense, D_MODEL, D_FF), D_MODEL),
        'ffn_w_down': dense((n_dense, D_FF, D_MODEL), D_FF),
        'moe_router': dense((n_moe, D_MODEL, N_EXPERTS), D_MODEL),
        'moe_w_gate': dense((n_moe, N_EXPERTS, D_MODEL, D_FF_EXPERT), D_MODEL),
        'moe_w_up': dense((n_moe, N_EXPERTS, D_MODEL, D_FF_EXPERT), D_MODEL),
        'moe_w_down': dense((n_moe, N_EXPERTS, D_FF_EXPERT, D_MODEL), D_FF_EXPERT),
        'final_g': gain((D_MODEL,)),
    }


def reference(x, c, ctx, c_ctx, w_mod, b_mod, norm1_g, norm2_g,
              attn_w_qkv, attn_q_g, attn_k_g, attn_w_o,
              ssd_w_in, ssd_conv_w, ssd_conv_b, ssd_dt_bias_f, ssd_dt_bias_b,
              ssd_a_log_f, ssd_a_log_b, ssd_d_skip, ssd_norm_g, ssd_w_out,
              cmlp_w_in, cmlp_b_in, cmlp_v_g, cmlp_w_s, cmlp_b_s, cmlp_w_out,
              ffn_w_gate, ffn_w_up, ffn_w_down,
              moe_router, moe_w_gate, moe_w_up, moe_w_down, final_g):
    n_lat = x.shape[1]
    rope_tabs = _axial_rope_tables(n_lat)
    silu_c = jax.nn.silu(c)
    silu_cc = jax.nn.silu(c_ctx)
    h_lat, h_ctx = x, ctx
    for i in range(DEPTH):
        need_ctx = i < DEPTH - 1
        mod_l = silu_c @ w_mod[i] + b_mod[i]
        mod_c = silu_cc @ w_mod[i] + b_mod[i]
        sh1_l, sc1_l, g1_l, sh2_l, sc2_l, g2_l = jnp.split(mod_l[:, None, :], N_MOD, axis=-1)
        sh1_c, sc1_c, g1_c, sh2_c, sc2_c, g2_c = jnp.split(mod_c, N_MOD, axis=-1)

        a_l = _modulate(_rms_norm(h_lat, norm1_g[i]), sh1_l, sc1_l)
        a_c = _modulate(_rms_norm(h_ctx, norm1_g[i]), sh1_c, sc1_c)
        kind, j = i % N_MIXERS, i // N_MIXERS
        if kind == 0:
            y_c, y_l = _attention_mixer(a_c, a_l, attn_w_qkv[j], attn_q_g[j], attn_k_g[j], attn_w_o[j],
                                        rope_tabs, need_ctx)
        elif kind == 1:
            y_c, y_l = _ssd_mixer(a_c, a_l, ssd_w_in[j], ssd_conv_w[j], ssd_conv_b[j],
                                  ssd_dt_bias_f[j], ssd_dt_bias_b[j], ssd_a_log_f[j], ssd_a_log_b[j],
                                  ssd_d_skip[j], ssd_norm_g[j], ssd_w_out[j], need_ctx)
        else:
            y_c, y_l = _chunk_mlp_mixer(a_c, a_l, cmlp_w_in[j], cmlp_b_in[j], cmlp_v_g[j], cmlp_w_s[j],
                                        cmlp_b_s[j], cmlp_w_out[j], need_ctx)
        h_lat = h_lat + g1_l * y_l
        a_l = _modulate(_rms_norm(h_lat, norm2_g[i]), sh2_l, sc2_l)
        if need_ctx:
            h_ctx = h_ctx + g1_c * y_c
            a_c = _modulate(_rms_norm(h_ctx, norm2_g[i]), sh2_c, sc2_c)
            tokens = jnp.concatenate([a_c, a_l], axis=1)
        else:
            tokens = a_l

        k = i // 2
        if i % 2 == 0:
            f = _swiglu(tokens, ffn_w_gate[k], ffn_w_up[k], ffn_w_down[k])
        else:
            f = _moe(tokens, moe_router[k], moe_w_gate[k], moe_w_up[k], moe_w_down[k])
        h_lat = h_lat + g2_l * f[:, f.shape[1] - n_lat:]
        if need_ctx:
            h_ctx = h_ctx + g2_c * f[:, :f.shape[1] - n_lat]
    return _rms_norm(h_lat, final_g)
```

```python
import functools
import math

import jax
import jax.numpy as jnp
from jax import lax
from jax.experimental import pallas as pl
from jax.experimental.pallas import tpu as pltpu

f32 = jnp.float32
bf16 = jnp.bfloat16

D = 1024
BATCH = 2
SEQ = 8192
CTX = 256
DEPTH = 4
GRID_W = 64
EPS = 1e-6
N_MOD = 6

HEAD_DIM = 128
N_HEADS = 8
N_KV = 2
Q_GROUP = 4
ROPE_AXIS_DIM = 64
ROPE_THETA = 10000.0

SSD_INNER = 2048
SSD_P = 64
SSD_HEADS = 32
SSD_GROUPS = 4
SSD_HPG = 8
SSD_N = 128
SSD_CHUNK = 128
SSD_BC = SSD_GROUPS * SSD_N
SSD_CONV_CH = SSD_INNER + 2 * SSD_BC

CMLP_D = 2048
CMLP_GROUPS = 8
CMLP_GW = 256
CMLP_CHUNK = 128

D_FF = 3584
N_EXPERTS = 8

N_LAT = BATCH * SEQ
N_CTX = BATCH * CTX
T = N_LAT + N_CTX
TM = 512
NT = T // TM
NT_LAT = N_LAT // TM
TILES_PER_SAMPLE = SEQ // TM
CTX_MOD_ROW = BATCH

TF = 896
NF = D_FF // TF

TQ = 256
TK = 512
NQ_LAT = SEQ // TQ

VMEM_LIMIT = 56 * 1024 * 1024


def _mod_row(t):
    return jnp.minimum(t // TILES_PER_SAMPLE, CTX_MOD_ROW)


def _tile_spec(width):
    return pl.BlockSpec((TM, width), lambda t: (t, 0))


def _mod_spec():
    return pl.BlockSpec((None, 8, D), lambda t: (_mod_row(t), 0, 0))


def _const_spec(shape):
    n = len(shape)
    return pl.BlockSpec(shape, lambda *_: (0,) * n)


def _params(semantics):
    return pltpu.CompilerParams(dimension_semantics=semantics, vmem_limit_bytes=VMEM_LIMIT)


def _silu(x):
    return x * jax.nn.sigmoid(x)


def _norm_mod(x, g, shift, scale):
    y = x * lax.rsqrt(jnp.mean(x * x, axis=-1, keepdims=True) + EPS) * g
    return y * (1.0 + scale) + shift


def _mod_kernel(c_ref, w_ref, b_ref, o_ref):
    s = _silu(c_ref[...])
    o_ref[...] = jnp.dot(s.astype(bf16), w_ref[...].astype(bf16), preferred_element_type=f32) + b_ref[...]


def _modulation(cond, w_mod, b_mod):
    nblk = N_MOD * D // D
    out = pl.pallas_call(
        _mod_kernel,
        grid=(DEPTH, nblk),
        in_specs=[
            pl.BlockSpec((8, D), lambda i, j: (0, 0)),
            pl.BlockSpec((None, D, D), lambda i, j: (i, 0, j)),
            pl.BlockSpec((None, 1, D), lambda i, j: (i, 0, j)),
        ],
        out_specs=pl.BlockSpec((None, 8, D), lambda i, j: (i, 0, j)),
        out_shape=jax.ShapeDtypeStruct((DEPTH, 8, N_MOD * D), f32),
        compiler_params=_params(("arbitrary", "arbitrary")),
        name="modulation",
    )(cond, w_mod, b_mod.reshape(DEPTH, 1, N_MOD * D))
    mod = out[:, :3].reshape(DEPTH, 3, N_MOD, D)
    return jnp.pad(mod, ((0, 0), (0, 0), (0, 8 - N_MOD), (0, 0)))


def _qkv_kernel(h_ref, mod_ref, g_ref, w_ref, qg_ref, kg_ref, cos_ref, sin_ref, q_ref, k_ref, v_ref):
    a = _norm_mod(h_ref[...], g_ref[...], mod_ref[0:1, :], mod_ref[1:2, :])
    qkv = jnp.dot(a.astype(bf16), w_ref[...], preferred_element_type=f32)
    cos = cos_ref[...]
    sin = sin_ref[...]
    lane = lax.broadcasted_iota(jnp.int32, (TM, HEAD_DIM), 1)
    first_half = (lane % ROPE_AXIS_DIM) < (ROPE_AXIS_DIM // 2)

    def head(x, g, scale):
        y = x * lax.rsqrt(jnp.mean(x * x, axis=-1, keepdims=True) + EPS) * g
        partner = jnp.where(first_half, pltpu.roll(y, HEAD_DIM - 32, 1), pltpu.roll(y, 32, 1))
        return (y * cos + partner * sin) * scale

    qg = qg_ref[...]
    kg = kg_ref[...]
    for hh in range(N_HEADS):
        sl = slice(hh * HEAD_DIM, (hh + 1) * HEAD_DIM)
        q_ref[:, sl] = head(qkv[:, sl], qg, HEAD_DIM ** -0.5).astype(bf16)
    for hh in range(N_KV):
        sl = slice(hh * HEAD_DIM, (hh + 1) * HEAD_DIM)
        src = slice((N_HEADS + hh) * HEAD_DIM, (N_HEADS + hh + 1) * HEAD_DIM)
        k_ref[:, sl] = head(qkv[:, src], kg, 1.0).astype(bf16)
    v_ref[...] = qkv[:, (N_HEADS + N_KV) * HEAD_DIM:].astype(bf16)


def _qkv_proj(h, mod, g, w_qkv, q_g, k_g, cos_t, sin_t):
    rope_spec = pl.BlockSpec((TM, HEAD_DIM), lambda t: (jnp.where(t < NT_LAT, t % TILES_PER_SAMPLE, TILES_PER_SAMPLE), 0))
    return pl.pallas_call(
        _qkv_kernel,
        grid=(NT,),
        in_specs=[
            _tile_spec(D), _mod_spec(), _const_spec((1, D)), _const_spec((D, (N_HEADS + 2 * N_KV) * HEAD_DIM)),
            _const_spec((1, HEAD_DIM)), _const_spec((1, HEAD_DIM)), rope_spec, rope_spec,
        ],
        out_specs=[_tile_spec(N_HEADS * HEAD_DIM), _tile_spec(N_KV * HEAD_DIM), _tile_spec(N_KV * HEAD_DIM)],
        out_shape=[
            jax.ShapeDtypeStruct((T, N_HEADS * HEAD_DIM), bf16),
            jax.ShapeDtypeStruct((T, N_KV * HEAD_DIM), bf16),
            jax.ShapeDtypeStruct((T, N_KV * HEAD_DIM), bf16),
        ],
        compiler_params=_params(("parallel",)),
        name="attn_qkv",
    )(h, mod, g, w_qkv, q_g, k_g, cos_t, sin_t)


def _attn_kernel(q_ref, kl_ref, vl_ref, kc_ref, vc_ref, o_ref, m_sc, l_sc, acc_sc):
    qi = pl.program_id(2)
    q = q_ref[...]
    qs = jnp.concatenate([q[:, i * HEAD_DIM:(i + 1) * HEAD_DIM] for i in range(Q_GROUP)], axis=0)

    m_sc[...] = jnp.full_like(m_sc, -jnp.inf)
    l_sc[...] = jnp.zeros_like(l_sc)
    acc_sc[...] = jnp.zeros_like(acc_sc)

    def process(k, v):
        n = k.shape[0]
        s = lax.dot_general(qs, k, (((1,), (1,)), ((), ())), preferred_element_type=f32)
        m_prev = m_sc[...]
        m_next = jnp.maximum(m_prev, jnp.max(s, axis=1, keepdims=True))
        alpha = jnp.exp(m_prev - m_next)
        p = jnp.exp(s - pltpu.repeat(m_next, n // HEAD_DIM, axis=1))
        psum = p[:, 0:HEAD_DIM]
        for j in range(1, n // HEAD_DIM):
            psum = psum + p[:, j * HEAD_DIM:(j + 1) * HEAD_DIM]
        l_sc[...] = alpha * l_sc[...] + psum
        acc_sc[...] = alpha * acc_sc[...] + jnp.dot(p.astype(bf16), v, preferred_element_type=f32)
        m_sc[...] = m_next

    process(kc_ref[...], vc_ref[...])

    @pl.when(qi < NQ_LAT)
    def _():
        def body(c, carry):
            off = pl.multiple_of(c * TK, TK)
            process(kl_ref[pl.ds(off, TK), :], vl_ref[pl.ds(off, TK), :])
            return carry
        lax.fori_loop(0, SEQ // TK, body, 0)

    l = jnp.sum(l_sc[...], axis=1, keepdims=True)
    o = acc_sc[...] / l
    for i in range(Q_GROUP):
        o_ref[:, i * HEAD_DIM:(i + 1) * HEAD_DIM] = o[i * TQ:(i + 1) * TQ].astype(bf16)


def _attention(q, k, v, need_ctx):
    nq = NQ_LAT + (1 if need_ctx else 0)
    ctx_blk = N_LAT // CTX

    def q_map(b, kh, qi):
        return (jnp.where(qi < NQ_LAT, b * NQ_LAT + qi, N_LAT // TQ + b), kh)

    lat_spec = pl.BlockSpec((SEQ, HEAD_DIM), lambda b, kh, qi: (b, kh))
    ctx_spec = pl.BlockSpec((CTX, HEAD_DIM), lambda b, kh, qi: (ctx_blk + b, kh))
    return pl.pallas_call(
        _attn_kernel,
        grid=(BATCH, N_KV, nq),
        in_specs=[pl.BlockSpec((TQ, Q_GROUP * HEAD_DIM), q_map), lat_spec, lat_spec, ctx_spec, ctx_spec],
        out_specs=pl.BlockSpec((TQ, Q_GROUP * HEAD_DIM), q_map),
        out_shape=jax.ShapeDtypeStruct((T, N_HEADS * HEAD_DIM), bf16),
        scratch_shapes=[pltpu.VMEM((Q_GROUP * TQ, HEAD_DIM), f32)] * 3,
        compiler_params=_params(("parallel", "parallel", "arbitrary")),
        name="attn_core",
    )(q, k, v, k, v)


def _proj_res_kernel(y_ref, w_ref, h_ref, mod_ref, o_ref):
    y = jnp.dot(y_ref[...], w_ref[...], preferred_element_type=f32)
    o_ref[...] = h_ref[...] + mod_ref[2:3, :] * y


def _proj_res(y, w, h, mod, nt):
    kdim = y.shape[1]
    return pl.pallas_call(
        _proj_res_kernel,
        grid=(nt,),
        in_specs=[_tile_spec(kdim), _const_spec((kdim, D)), _tile_spec(D), _mod_spec()],
        out_specs=_tile_spec(D),
        out_shape=jax.ShapeDtypeStruct((T, D), f32),
        compiler_params=_params(("parallel",)),
        name="proj_res",
    )(y, w, h, mod)


def _ffn_kernel(h_ref, mod_ref, g_ref, wg_ref, wu_ref, wd_ref, o_ref, xn_sc, acc_sc):
    f = pl.program_id(1)

    @pl.when(f == 0)
    def _():
        xn_sc[...] = _norm_mod(h_ref[...], g_ref[...], mod_ref[3:4, :], mod_ref[4:5, :]).astype(bf16)
        acc_sc[...] = jnp.zeros_like(acc_sc)

    x = xn_sc[...]
    gate = jnp.dot(x, wg_ref[...], preferred_element_type=f32)
    up = jnp.dot(x, wu_ref[...], preferred_element_type=f32)
    hid = (_silu(gate) * up).astype(bf16)
    acc_sc[...] += jnp.dot(hid, wd_ref[...], preferred_element_type=f32)

    @pl.when(f == NF - 1)
    def _():
        o_ref[...] = h_ref[...] + mod_ref[5:6, :] * acc_sc[...]


def _ffn(h, mod, g, wg, wu, wd, nt):
    return pl.pallas_call(
        _ffn_kernel,
        grid=(nt, NF),
        in_specs=[
            pl.BlockSpec((TM, D), lambda t, f: (t, 0)),
            pl.BlockSpec((None, 8, D), lambda t, f: (_mod_row(t), 0, 0)),
            pl.BlockSpec((1, D), lambda t, f: (0, 0)),
            pl.BlockSpec((D, TF), lambda t, f: (0, f)),
            pl.BlockSpec((D, TF), lambda t, f: (0, f)),
            pl.BlockSpec((TF, D), lambda t, f: (f, 0)),
        ],
        out_specs=pl.BlockSpec((TM, D), lambda t, f: (t, 0)),
        out_shape=jax.ShapeDtypeStruct((T, D), f32),
        scratch_shapes=[pltpu.VMEM((TM, D), bf16), pltpu.VMEM((TM, D), f32)],
        compiler_params=_params(("parallel", "arbitrary")),
        name="ffn_dense",
    )(h, mod, g, wg, wu, wd)


def _split_bf16(x):
    hi = x.astype(bf16)
    lo = (x - hi.astype(f32)).astype(bf16)
    return hi, lo


def _moe_kernel(h_ref, mod_ref, g_ref, r_ref, wg_ref, wu_ref, wd_ref, o_ref, xn_sc, gate_sc, acc_sc):
    e = pl.program_id(1)
    f = pl.program_id(2)
    lane = lax.broadcasted_iota(jnp.int32, (TM, 128), 1)

    @pl.when((e == 0) & (f == 0))
    def _():
        x = _norm_mod(h_ref[...], g_ref[...], mod_ref[3:4, :], mod_ref[4:5, :])
        xh, xl = _split_bf16(x)
        rh, rl = _split_bf16(r_ref[...])
        logits = (jnp.dot(xh, rh, preferred_element_type=f32) + jnp.dot(xl, rh, preferred_element_type=f32)
                  + jnp.dot(xh, rl, preferred_element_type=f32))
        lg = jnp.where(lane < N_EXPERTS, logits, -jnp.inf)
        m1 = jnp.max(lg, axis=1, keepdims=True)
        i1 = jnp.min(jnp.where(lg == m1, lane, 128), axis=1, keepdims=True)
        lg2 = jnp.where(lane == i1, -jnp.inf, lg)
        m2 = jnp.max(lg2, axis=1, keepdims=True)
        i2 = jnp.min(jnp.where(lg2 == m2, lane, 128), axis=1, keepdims=True)
        e2 = jnp.exp(m2 - m1)
        den = 1.0 + e2
        gate_sc[...] = jnp.where(lane == i1, 1.0 / den, 0.0) + jnp.where(lane == i2, e2 / den, 0.0)
        xn_sc[...] = xh
        acc_sc[...] = jnp.zeros_like(acc_sc)

    x = xn_sc[...]
    gate = jnp.dot(x, wg_ref[...], preferred_element_type=f32)
    up = jnp.dot(x, wu_ref[...], preferred_element_type=f32)
    hid = (_silu(gate) * up).astype(bf16)
    ge = jnp.sum(jnp.where(lane == e, gate_sc[...], 0.0), axis=1, keepdims=True)
    acc_sc[...] += ge * jnp.dot(hid, wd_ref[...], preferred_element_type=f32)

    @pl.when((e == N_EXPERTS - 1) & (f == NF - 1))
    def _():
        o_ref[...] = h_ref[...] + mod_ref[5:6, :] * acc_sc[...]


def _moe(h, mod, g, router, wg, wu, wd, nt):
    return pl.pallas_call(
        _moe_kernel,
        grid=(nt, N_EXPERTS, NF),
        in_specs=[
            pl.BlockSpec((TM, D), lambda t, e, f: (t, 0)),
            pl.BlockSpec((None, 8, D), lambda t, e, f: (_mod_row(t), 0, 0)),
            pl.BlockSpec((1, D), lambda t, e, f: (0, 0)),
            pl.BlockSpec((D, 128), lambda t, e, f: (0, 0)),
            pl.BlockSpec((None, D, TF), lambda t, e, f: (e, 0, f)),
            pl.BlockSpec((None, D, TF), lambda t, e, f: (e, 0, f)),
            pl.BlockSpec((None, TF, D), lambda t, e, f: (e, f, 0)),
        ],
        out_specs=pl.BlockSpec((TM, D), lambda t, e, f: (t, 0)),
        out_shape=jax.ShapeDtypeStruct((T, D), f32),
        scratch_shapes=[pltpu.VMEM((TM, D), bf16), pltpu.VMEM((TM, 128), f32), pltpu.VMEM((TM, D), f32)],
        compiler_params=_params(("parallel", "arbitrary", "arbitrary")),
        name="moe",
    )(h, mod, g, router, wg, wu, wd)


def _ssd_in_kernel(h_ref, mod_ref, g_ref, wz_ref, wx_ref, wdt_ref, z_ref, x_ref, dt_ref):
    a = _norm_mod(h_ref[...], g_ref[...], mod_ref[0:1, :], mod_ref[1:2, :]).astype(bf16)
    for j in range(SSD_INNER // D):
        z_ref[:, j * D:(j + 1) * D] = jnp.dot(a, wz_ref[:, j * D:(j + 1) * D], preferred_element_type=f32).astype(bf16)
    for j in range(SSD_CONV_CH // D):
        x_ref[:, j * D:(j + 1) * D] = jnp.dot(a, wx_ref[:, j * D:(j + 1) * D], preferred_element_type=f32).astype(bf16)
    dt_ref[...] = jnp.dot(a, wdt_ref[...], preferred_element_type=f32)


def _ssd_in_proj(h, mod, g, wz, wx, wdt):
    return pl.pallas_call(
        _ssd_in_kernel,
        grid=(NT,),
        in_specs=[_tile_spec(D), _mod_spec(), _const_spec((1, D)), _const_spec((D, SSD_INNER)),
                  _const_spec((D, SSD_CONV_CH)), _const_spec((D, 256))],
        out_specs=[_tile_spec(SSD_INNER), _tile_spec(SSD_CONV_CH), _tile_spec(256)],
        out_shape=[jax.ShapeDtypeStruct((T, SSD_INNER), bf16), jax.ShapeDtypeStruct((T, SSD_CONV_CH), bf16),
                   jax.ShapeDtypeStruct((T, 256), f32)],
        compiler_params=_params(("parallel",)),
        name="ssd_in_proj",
    )(h, mod, g, wz, wx, wdt)


CONV_TM = 256
CONV_HALO = 16
CONV_TILES_PER_SAMPLE = SEQ // CONV_TM


def _conv_kernel(x_ref, prev_ref, next_ref, w_ref, b_ref, o_ref):
    i = pl.program_id(0)
    is_ctx = i >= N_LAT // CONV_TM
    seg_start = is_ctx | (i % CONV_TILES_PER_SAMPLE == 0)
    seg_end = is_ctx | (i % CONV_TILES_PER_SAMPLE == CONV_TILES_PER_SAMPLE - 1)
    x = x_ref[...].astype(f32)
    prev_row = jnp.where(seg_start, 0.0, prev_ref[CONV_HALO - 1:CONV_HALO, :].astype(f32))
    next_row = jnp.where(seg_end, 0.0, next_ref[0:1, :].astype(f32))
    row = lax.broadcasted_iota(jnp.int32, x.shape, 0)
    xm1 = jnp.where(row == 0, prev_row, pltpu.roll(x, 1, 0))
    xp1 = jnp.where(row == CONV_TM - 1, next_row, pltpu.roll(x, CONV_TM - 1, 0))
    y = w_ref[0:1, :] * xm1 + w_ref[1:2, :] * x + w_ref[2:3, :] * xp1 + b_ref[...]
    o_ref[...] = _silu(y).astype(bf16)


def _ssd_conv(xbc, conv_w, conv_b):
    n = T // CONV_TM
    per = CONV_TM // CONV_HALO
    last = T // CONV_HALO - 1
    return pl.pallas_call(
        _conv_kernel,
        grid=(n,),
        in_specs=[
            pl.BlockSpec((CONV_TM, SSD_CONV_CH), lambda i: (i, 0)),
            pl.BlockSpec((CONV_HALO, SSD_CONV_CH), lambda i: (jnp.maximum(i * per - 1, 0), 0)),
            pl.BlockSpec((CONV_HALO, SSD_CONV_CH), lambda i: (jnp.minimum((i + 1) * per, last), 0)),
            _const_spec((8, SSD_CONV_CH)), _const_spec((1, SSD_CONV_CH)),
        ],
        out_specs=pl.BlockSpec((CONV_TM, SSD_CONV_CH), lambda i: (i, 0)),
        out_shape=jax.ShapeDtypeStruct((T, SSD_CONV_CH), bf16),
        compiler_params=_params(("parallel",)),
        name="ssd_conv",
    )(xbc, xbc, xbc, conv_w, conv_b)


def _dot3(a_parts, b):
    out = jnp.dot(a_parts[0], b, preferred_element_type=f32)
    for a in a_parts[1:]:
        out = out + jnp.dot(a, b, preferred_element_type=f32)
    return out


def _split3(x):
    p0 = x.astype(bf16)
    r = x - p0.astype(f32)
    p1 = r.astype(bf16)
    p2 = (r - p1.astype(f32)).astype(bf16)
    return p0, p1, p2


def _ssd_scan_kernel(x_ref, b_ref, c_ref, dt_ref, bias_ref, alog_ref, y_ref, state_sc):
    d = pl.program_id(0)
    c = pl.program_id(2)

    @pl.when(c == 0)
    def _():
        state_sc[...] = jnp.zeros_like(state_sc)

    L = SSD_CHUNK
    li = lax.broadcasted_iota(jnp.int32, (L, L), 0)
    si = lax.broadcasted_iota(jnp.int32, (L, L), 1)
    causal = (si - li) * (1 - 2 * d) <= 0
    tri = jnp.where(causal, 1.0, 0.0).astype(bf16)
    hi = lax.broadcasted_iota(jnp.int32, (128, SSD_INNER), 0)
    ci = lax.broadcasted_iota(jnp.int32, (128, SSD_INNER), 1)
    expand = jnp.where(ci // SSD_P == hi, 1.0, 0.0).astype(bf16)

    dt = jax.nn.softplus(dt_ref[...] + bias_ref[...])
    a_neg = -jnp.exp(alog_ref[...])
    da = dt * a_neg
    da_parts = _split3(da)
    cs = (jnp.dot(tri, da_parts[0], preferred_element_type=f32) + jnp.dot(tri, da_parts[1], preferred_element_type=f32)
          + jnp.dot(tri, da_parts[2], preferred_element_type=f32))
    dn_t = (((0,), (1,)), ((), ()))
    cs_t = (lax.dot_general(da_parts[0], tri, dn_t, preferred_element_type=f32)
            + lax.dot_general(da_parts[1], tri, dn_t, preferred_element_type=f32)
            + lax.dot_general(da_parts[2], tri, dn_t, preferred_element_type=f32))
    total = jnp.sum(da, axis=0, keepdims=True)

    cs_x = _dot3(_split3(cs), expand)
    dt_x = _dot3(_split3(dt), expand)
    total_x = _dot3(_split3(jnp.broadcast_to(total, (8, 128))), expand)[0:1]

    x = x_ref[...].astype(f32)
    xdt = (x * dt_x).astype(bf16)
    xw = (x * (jnp.exp(total_x - cs_x) * dt_x)).astype(bf16)
    out_scale = jnp.exp(cs_x)
    chunk_decay = jnp.exp(total_x)

    for g in range(SSD_GROUPS):
        bg = b_ref[:, g * SSD_N:(g + 1) * SSD_N]
        cg = c_ref[:, g * SSD_N:(g + 1) * SSD_N]
        cb = lax.dot_general(cg, bg, (((1,), (1,)), ((), ())), preferred_element_type=f32)
        gsl = slice(g * SSD_HPG * SSD_P, (g + 1) * SSD_HPG * SSD_P)
        st = state_sc[g]
        y_off = jnp.dot(cg, st.astype(bf16), preferred_element_type=f32) * out_scale[:, gsl]
        ys = []
        for r in range(SSD_HPG):
            hd = g * SSD_HPG + r
            seg = cs[:, hd:hd + 1] - cs_t[hd:hd + 1, :]
            decay = jnp.exp(jnp.where(causal, seg, -jnp.inf))
            m = (cb * decay).astype(bf16)
            ys.append(jnp.dot(m, xdt[:, hd * SSD_P:(hd + 1) * SSD_P], preferred_element_type=f32))
        y_ref[:, gsl] = (jnp.concatenate(ys, axis=1) + y_off).astype(bf16)
        new = lax.dot_general(bg, xw[:, gsl], (((0,), (0,)), ((), ())), preferred_element_type=f32)
        state_sc[g] = st * chunk_decay[:, gsl] + new


SSD_NCHUNK = (SEQ + CTX) // SSD_CHUNK
SSD_CTX_CHUNKS = CTX // SSD_CHUNK
SSD_LAT_CHUNKS = SEQ // SSD_CHUNK


def _ssd_chunk_block(d, b, c):
    ctx_j = jnp.where(d == 0, c, SSD_CTX_CHUNKS - 1 - c)
    lat_j = jnp.where(d == 0, c - SSD_CTX_CHUNKS, SSD_LAT_CHUNKS - 1 - (c - SSD_CTX_CHUNKS))
    return jnp.where(c < SSD_CTX_CHUNKS, N_LAT // SSD_CHUNK + b * SSD_CTX_CHUNKS + ctx_j, b * SSD_LAT_CHUNKS + lat_j)


def _ssd_scan(xbc, dt, dt_bias, a_log):
    x_blk = SSD_INNER // SSD_BC
    return pl.pallas_call(
        _ssd_scan_kernel,
        grid=(2, BATCH, SSD_NCHUNK),
        in_specs=[
            pl.BlockSpec((SSD_CHUNK, SSD_INNER), lambda d, b, c: (_ssd_chunk_block(d, b, c), 0)),
            pl.BlockSpec((SSD_CHUNK, SSD_BC), lambda d, b, c: (_ssd_chunk_block(d, b, c), x_blk)),
            pl.BlockSpec((SSD_CHUNK, SSD_BC), lambda d, b, c: (_ssd_chunk_block(d, b, c), x_blk + 1)),
            pl.BlockSpec((SSD_CHUNK, 128), lambda d, b, c: (_ssd_chunk_block(d, b, c), d)),
            pl.BlockSpec((None, 1, 128), lambda d, b, c: (d, 0, 0)),
            pl.BlockSpec((None, 1, 128), lambda d, b, c: (d, 0, 0)),
        ],
        out_specs=pl.BlockSpec((None, SSD_CHUNK, SSD_INNER), lambda d, b, c: (d, _ssd_chunk_block(d, b, c), 0)),
        out_shape=jax.ShapeDtypeStruct((2, T, SSD_INNER), bf16),
        scratch_shapes=[pltpu.VMEM((SSD_GROUPS, SSD_N, SSD_HPG * SSD_P), f32)],
        compiler_params=_params(("parallel", "parallel", "arbitrary")),
        name="ssd_scan",
    )(xbc, xbc, xbc, dt, dt_bias, a_log)


def _ssd_out_kernel(yf_ref, yb_ref, x_ref, z_ref, dskip_ref, ng_ref, w_ref, h_ref, mod_ref, o_ref):
    y = yf_ref[...].astype(f32) + yb_ref[...].astype(f32) + x_ref[...].astype(f32) * dskip_ref[...]
    gated = y * _silu(z_ref[...].astype(f32))
    gw = SSD_INNER // SSD_GROUPS
    parts = []
    for g in range(SSD_GROUPS):
        s = gated[:, g * gw:(g + 1) * gw]
        parts.append(s * lax.rsqrt(jnp.mean(s * s, axis=-1, keepdims=True) + EPS))
    normed = (jnp.concatenate(parts, axis=1) * ng_ref[...]).astype(bf16)
    out = jnp.dot(normed, w_ref[...], preferred_element_type=f32)
    o_ref[...] = h_ref[...] + mod_ref[2:3, :] * out


def _ssd_out(y2, xbc, z, dskip, ng, w_out, h, mod, nt):
    return pl.pallas_call(
        _ssd_out_kernel,
        grid=(nt,),
        in_specs=[
            pl.BlockSpec((None, TM, SSD_INNER), lambda t: (0, t, 0)),
            pl.BlockSpec((None, TM, SSD_INNER), lambda t: (1, t, 0)),
            pl.BlockSpec((TM, SSD_INNER), lambda t: (t, 0)),
            _tile_spec(SSD_INNER), _const_spec((1, SSD_INNER)), _const_spec((1, SSD_INNER)),
            _const_spec((SSD_INNER, D)), _tile_spec(D), _mod_spec(),
        ],
        out_specs=_tile_spec(D),
        out_shape=jax.ShapeDtypeStruct((T, D), f32),
        compiler_params=_params(("parallel",)),
        name="ssd_out",
    )(y2, y2, xbc, z, dskip, ng, w_out, h, mod)


def _gelu(x):
    return 0.5 * x * (1.0 + lax.erf(x * math.sqrt(0.5)))


def _cmlp_kernel(h_ref, mod_ref, g_ref, wu_ref, wv_ref, bu_ref, bv_ref, vg_ref, ws_ref, bs_ref, wo_ref, o_ref, uv_sc):
    a = _norm_mod(h_ref[...], g_ref[...], mod_ref[0:1, :], mod_ref[1:2, :]).astype(bf16)
    v = _gelu(jnp.dot(a, wv_ref[...], preferred_element_type=f32) + bv_ref[...])
    v = (v * lax.rsqrt(jnp.mean(v * v, axis=-1, keepdims=True) + EPS) * vg_ref[...]).astype(bf16)
    u = _gelu(jnp.dot(a, wu_ref[...], preferred_element_type=f32) + bu_ref[...])
    for ck in range(TM // CMLP_CHUNK):
        rows = slice(ck * CMLP_CHUNK, (ck + 1) * CMLP_CHUNK)
        for g in range(CMLP_GROUPS):
            cols = slice(g * CMLP_GW, (g + 1) * CMLP_GW)
            mixed = jnp.dot(ws_ref[g], v[rows, cols], preferred_element_type=f32) + bs_ref[:, cols]
            uv_sc[rows, cols] = (u[rows, cols] * mixed).astype(bf16)
    out = jnp.dot(uv_sc[...], wo_ref[...], preferred_element_type=f32)
    o_ref[...] = h_ref[...] + mod_ref[2:3, :] * out


def _cmlp(h, mod, g, wu, wv, bu, bv, vg, ws, bs, wo, nt):
    return pl.pallas_call(
        _cmlp_kernel,
        grid=(nt,),
        in_specs=[
            _tile_spec(D), _mod_spec(), _const_spec((1, D)), _const_spec((D, CMLP_D)), _const_spec((D, CMLP_D)),
            _const_spec((1, CMLP_D)), _const_spec((1, CMLP_D)), _const_spec((1, CMLP_D)),
            _const_spec((CMLP_GROUPS, CMLP_CHUNK, CMLP_CHUNK)), _const_spec((CMLP_CHUNK, CMLP_D)),
            _const_spec((CMLP_D, D)),
        ],
        out_specs=_tile_spec(D),
        out_shape=jax.ShapeDtypeStruct((T, D), f32),
        scratch_shapes=[pltpu.VMEM((TM, CMLP_D), bf16)],
        compiler_params=_params(("parallel",)),
        name="cmlp",
    )(h, mod, g, wu, wv, bu, bv, vg, ws, bs, wo)


def _final_kernel(h_ref, g_ref, o_ref):
    x = h_ref[...]
    o_ref[...] = x * lax.rsqrt(jnp.mean(x * x, axis=-1, keepdims=True) + EPS) * g_ref[...]


def _final_norm(h, g):
    return pl.pallas_call(
        _final_kernel,
        grid=(NT_LAT,),
        in_specs=[_tile_spec(D), _const_spec((1, D))],
        out_specs=_tile_spec(D),
        out_shape=jax.ShapeDtypeStruct((N_LAT, D), f32),
        compiler_params=_params(("parallel",)),
        name="final_norm",
    )(h, g)


def _rope_tables():
    pos = jnp.arange(SEQ)
    inv_freq = 1.0 / (ROPE_THETA ** (jnp.arange(0, ROPE_AXIS_DIM, 2, dtype=f32) / ROPE_AXIS_DIM))
    ang_r = (pos // GRID_W).astype(f32)[:, None] * inv_freq
    ang_c = (pos % GRID_W).astype(f32)[:, None] * inv_freq
    cos = jnp.concatenate([jnp.cos(ang_r)] * 2 + [jnp.cos(ang_c)] * 2, axis=1)
    sin = jnp.concatenate([-jnp.sin(ang_r), jnp.sin(ang_r), -jnp.sin(ang_c), jnp.sin(ang_c)], axis=1)
    cos = jnp.concatenate([cos, jnp.ones((TM, HEAD_DIM), f32)], axis=0)
    sin = jnp.concatenate([sin, jnp.zeros((TM, HEAD_DIM), f32)], axis=0)
    return cos, sin


def _row(v):
    return v.reshape(1, -1)


def _pad_lanes(v, width=128):
    return jnp.pad(v, ((0, 0), (0, width - v.shape[1])))


def kernel(x, c, ctx, c_ctx, w_mod, b_mod, norm1_g, norm2_g, attn_w_qkv, attn_q_g, attn_k_g, attn_w_o, ssd_w_in, ssd_conv_w, ssd_conv_b, ssd_dt_bias_f, ssd_dt_bias_b, ssd_a_log_f, ssd_a_log_b, ssd_d_skip, ssd_norm_g, ssd_w_out, cmlp_w_in, cmlp_b_in, cmlp_v_g, cmlp_w_s, cmlp_b_s, cmlp_w_out, ffn_w_gate, ffn_w_up, ffn_w_down, moe_router, moe_w_gate, moe_w_up, moe_w_down, final_g):
    h = jnp.concatenate([x.reshape(N_LAT, D), ctx.reshape(N_CTX, D)], axis=0)
    cond = jnp.concatenate([c, c_ctx[None, :], jnp.zeros((8 - BATCH - 1, D), f32)], axis=0)
    mods = _modulation(cond, w_mod, b_mod)
    cos_t, sin_t = _rope_tables()

    for i in range(DEPTH):
        need_ctx = i < DEPTH - 1
        nt = NT if need_ctx else NT_LAT
        mod = mods[i]
        kind, j = i % 3, i // 3
        g1 = _row(norm1_g[i])
        if kind == 0:
            q, k, v = _qkv_proj(h, mod, g1, attn_w_qkv[j].astype(bf16), _row(attn_q_g[j]), _row(attn_k_g[j]), cos_t, sin_t)
            o = _attention(q, k, v, need_ctx)
            h = _proj_res(o, attn_w_o[j].astype(bf16), h, mod, nt)
        elif kind == 1:
            w_in = ssd_w_in[j]
            wz = w_in[:, :SSD_INNER].astype(bf16)
            wx = w_in[:, SSD_INNER:SSD_INNER + SSD_CONV_CH].astype(bf16)
            w_dt = w_in[:, SSD_INNER + SSD_CONV_CH:]
            wdt = jnp.concatenate([_pad_lanes(w_dt[:, :SSD_HEADS]), _pad_lanes(w_dt[:, SSD_HEADS:])], axis=1).astype(bf16)
            z, xbc, dt = _ssd_in_proj(h, mod, g1, wz, wx, wdt)
            conv_w = jnp.pad(ssd_conv_w[j], ((0, 8 - ssd_conv_w.shape[1]), (0, 0)))
            xbc = _ssd_conv(xbc, conv_w, _row(ssd_conv_b[j]))
            dt_bias = jnp.stack([_pad_lanes(_row(ssd_dt_bias_f[j])), _pad_lanes(_row(ssd_dt_bias_b[j]))])
            a_log = jnp.stack([_pad_lanes(_row(ssd_a_log_f[j])), _pad_lanes(_row(ssd_a_log_b[j]))])
            y2 = _ssd_scan(xbc, dt, dt_bias, a_log)
            dskip = _row(jnp.repeat(ssd_d_skip[j], SSD_P))
            h = _ssd_out(y2, xbc, z, dskip, _row(ssd_norm_g[j]), ssd_w_out[j].astype(bf16), h, mod, nt)
        else:
            w_in = cmlp_w_in[j]
            b_in = cmlp_b_in[j]
            bs = jnp.repeat(cmlp_b_s[j].T, CMLP_GW, axis=1)
            h = _cmlp(h, mod, g1, w_in[:, :CMLP_D].astype(bf16), w_in[:, CMLP_D:].astype(bf16),
                      _row(b_in[:CMLP_D]), _row(b_in[CMLP_D:]), _row(cmlp_v_g[j]), cmlp_w_s[j].astype(bf16), bs,
                      cmlp_w_out[j].astype(bf16), nt)
        kk = i // 2
        g2 = _row(norm2_g[i])
        if i % 2 == 0:
            h = _ffn(h, mod, g2, ffn_w_gate[kk].astype(bf16), ffn_w_up[kk].astype(bf16), ffn_w_down[kk].astype(bf16), nt)
        else:
            h = _moe(h, mod, g2, _pad_lanes(moe_router[kk]), moe_w_gate[kk].astype(bf16), moe_w_up[kk].astype(bf16),
                     moe_w_down[kk].astype(bf16), nt)
    return _final_norm(h, _row(final_g)).reshape(BATCH, SEQ, D)
```

```python
import functools
import math

import jax
import jax.numpy as jnp
from jax import lax
from jax.experimental import pallas as pl
from jax.experimental.pallas import tpu as pltpu

f32 = jnp.float32
bf16 = jnp.bfloat16

D = 1024
BATCH = 2
SEQ = 8192
CTX = 256
DEPTH = 4
GRID_W = 64
EPS = 1e-6
N_MOD = 6

HEAD_DIM = 128
N_HEADS = 8
N_KV = 2
Q_GROUP = 4
ROPE_AXIS_DIM = 64
ROPE_THETA = 10000.0

SSD_INNER = 2048
SSD_P = 64
SSD_HEADS = 32
SSD_GROUPS = 4
SSD_HPG = 8
SSD_N = 128
SSD_CHUNK = 128
SSD_BC = SSD_GROUPS * SSD_N
SSD_CONV_CH = SSD_INNER + 2 * SSD_BC

CMLP_D = 2048
CMLP_GROUPS = 8
CMLP_GW = 256
CMLP_CHUNK = 128

D_FF = 3584
N_EXPERTS = 8

N_LAT = BATCH * SEQ
N_CTX = BATCH * CTX
T = N_LAT + N_CTX
TM = 512
NT = T // TM
NT_LAT = N_LAT // TM
TILES_PER_SAMPLE = SEQ // TM
CTX_MOD_ROW = BATCH

TF = 896
NF = D_FF // TF

TQ = 256
TK = 512
NQ_LAT = SEQ // TQ

VMEM_LIMIT = 56 * 1024 * 1024


def _mod_row(t):
    return jnp.minimum(t // TILES_PER_SAMPLE, CTX_MOD_ROW)


def _tile_spec(width):
    return pl.BlockSpec((TM, width), lambda t: (t, 0))


def _mod_spec():
    return pl.BlockSpec((None, 8, D), lambda t: (_mod_row(t), 0, 0))


def _const_spec(shape):
    n = len(shape)
    return pl.BlockSpec(shape, lambda *_: (0,) * n)


def _params(semantics):
    return pltpu.CompilerParams(dimension_semantics=semantics, vmem_limit_bytes=VMEM_LIMIT)


def _silu(x):
    return x * jax.nn.sigmoid(x)


def _norm_mod(x, g, shift, scale):
    y = x * lax.rsqrt(jnp.mean(x * x, axis=-1, keepdims=True) + EPS) * g
    return y * (1.0 + scale) + shift


def _mod_kernel(c_ref, w_ref, b_ref, o_ref):
    s = _silu(c_ref[...])
    o_ref[...] = jnp.dot(s.astype(bf16), w_ref[...].astype(bf16), preferred_element_type=f32) + b_ref[...]


def _modulation(cond, w_mod, b_mod):
    nblk = N_MOD * D // D
    out = pl.pallas_call(
        _mod_kernel,
        grid=(DEPTH, nblk),
        in_specs=[
            pl.BlockSpec((8, D), lambda i, j: (0, 0)),
            pl.BlockSpec((None, D, D), lambda i, j: (i, 0, j)),
            pl.BlockSpec((None, 1, D), lambda i, j: (i, 0, j)),
        ],
        out_specs=pl.BlockSpec((None, 8, D), lambda i, j: (i, 0, j)),
        out_shape=jax.ShapeDtypeStruct((DEPTH, 8, N_MOD * D), f32),
        compiler_params=_params(("arbitrary", "arbitrary")),
        name="modulation",
    )(cond, w_mod, b_mod.reshape(DEPTH, 1, N_MOD * D))
    mod = out[:, :3].reshape(DEPTH, 3, N_MOD, D)
    return jnp.pad(mod, ((0, 0), (0, 0), (0, 8 - N_MOD), (0, 0)))


def _qkv_kernel(h_ref, mod_ref, g_ref, w_ref, qg_ref, kg_ref, cos_ref, sin_ref, q_ref, k_ref, v_ref):
    a = _norm_mod(h_ref[...], g_ref[...], mod_ref[0:1, :], mod_ref[1:2, :])
    qkv = jnp.dot(a.astype(bf16), w_ref[...], preferred_element_type=f32)
    cos = cos_ref[...]
    sin = sin_ref[...]
    lane = lax.broadcasted_iota(jnp.int32, (TM, HEAD_DIM), 1)
    first_half = (lane % ROPE_AXIS_DIM) < (ROPE_AXIS_DIM // 2)

    def head(x, g, scale):
        y = x * lax.rsqrt(jnp.mean(x * x, axis=-1, keepdims=True) + EPS) * g
        partner = jnp.where(first_half, pltpu.roll(y, HEAD_DIM - 32, 1), pltpu.roll(y, 32, 1))
        return (y * cos + partner * sin) * scale

    qg = qg_ref[...]
    kg = kg_ref[...]
    for hh in range(N_HEADS):
        sl = slice(hh * HEAD_DIM, (hh + 1) * HEAD_DIM)
        q_ref[:, sl] = head(qkv[:, sl], qg, HEAD_DIM ** -0.5).astype(bf16)
    for hh in range(N_KV):
        sl = slice(hh * HEAD_DIM, (hh + 1) * HEAD_DIM)
        src = slice((N_HEADS + hh) * HEAD_DIM, (N_HEADS + hh + 1) * HEAD_DIM)
        k_ref[:, sl] = head(qkv[:, src], kg, 1.0).astype(bf16)
    v_ref[...] = qkv[:, (N_HEADS + N_KV) * HEAD_DIM:].astype(bf16)


def _qkv_proj(h, mod, g, w_qkv, q_g, k_g, cos_t, sin_t):
    rope_spec = pl.BlockSpec((TM, HEAD_DIM), lambda t: (jnp.where(t < NT_LAT, t % TILES_PER_SAMPLE, TILES_PER_SAMPLE), 0))
    return pl.pallas_call(
        _qkv_kernel,
        grid=(NT,),
        in_specs=[
            _tile_spec(D), _mod_spec(), _const_spec((1, D)), _const_spec((D, (N_HEADS + 2 * N_KV) * HEAD_DIM)),
            _const_spec((1, HEAD_DIM)), _const_spec((1, HEAD_DIM)), rope_spec, rope_spec,
        ],
        out_specs=[_tile_spec(N_HEADS * HEAD_DIM), _tile_spec(N_KV * HEAD_DIM), _tile_spec(N_KV * HEAD_DIM)],
        out_shape=[
            jax.ShapeDtypeStruct((T, N_HEADS * HEAD_DIM), bf16),
            jax.ShapeDtypeStruct((T, N_KV * HEAD_DIM), bf16),
            jax.ShapeDtypeStruct((T, N_KV * HEAD_DIM), bf16),
        ],
        compiler_params=_params(("parallel",)),
        name="attn_qkv",
    )(h, mod, g, w_qkv, q_g, k_g, cos_t, sin_t)


def _attn_kernel(q_ref, kl_ref, vl_ref, kc_ref, vc_ref, o_ref, m_sc, l_sc, acc_sc):
    qi = pl.program_id(2)
    q = q_ref[...]
    qs = jnp.concatenate([q[:, i * HEAD_DIM:(i + 1) * HEAD_DIM] for i in range(Q_GROUP)], axis=0)

    m_sc[...] = jnp.full_like(m_sc, -jnp.inf)
    l_sc[...] = jnp.zeros_like(l_sc)
    acc_sc[...] = jnp.zeros_like(acc_sc)

    def process(k, v):
        n = k.shape[0]
        s = lax.dot_general(qs, k, (((1,), (1,)), ((), ())), preferred_element_type=f32)
        m_prev = m_sc[...]
        m_next = jnp.maximum(m_prev, jnp.max(s, axis=1, keepdims=True))
        alpha = jnp.exp(m_prev - m_next)
        p = jnp.exp(s - jnp.concatenate([m_next] * (n // HEAD_DIM), axis=1))
        psum = p[:, 0:HEAD_DIM]
        for j in range(1, n // HEAD_DIM):
            psum = psum + p[:, j * HEAD_DIM:(j + 1) * HEAD_DIM]
        l_sc[...] = alpha * l_sc[...] + psum
        acc_sc[...] = alpha * acc_sc[...] + jnp.dot(p.astype(bf16), v, preferred_element_type=f32)
        m_sc[...] = m_next

    process(kc_ref[...], vc_ref[...])

    @pl.when(qi < NQ_LAT)
    def _():
        def body(c, carry):
            off = pl.multiple_of(c * TK, TK)
            process(kl_ref[pl.ds(off, TK), :], vl_ref[pl.ds(off, TK), :])
            return carry
        lax.fori_loop(0, SEQ // TK, body, 0)

    l = jnp.sum(l_sc[...], axis=1, keepdims=True)
    o = acc_sc[...] / l
    for i in range(Q_GROUP):
        o_ref[:, i * HEAD_DIM:(i + 1) * HEAD_DIM] = o[i * TQ:(i + 1) * TQ].astype(bf16)


def _attention(q, k, v, need_ctx):
    nq = NQ_LAT + (1 if need_ctx else 0)
    ctx_blk = N_LAT // CTX

    def q_map(b, kh, qi):
        return (jnp.where(qi < NQ_LAT, b * NQ_LAT + qi, N_LAT // TQ + b), kh)

    lat_spec = pl.BlockSpec((SEQ, HEAD_DIM), lambda b, kh, qi: (b, kh))
    ctx_spec = pl.BlockSpec((CTX, HEAD_DIM), lambda b, kh, qi: (ctx_blk + b, kh))
    return pl.pallas_call(
        _attn_kernel,
        grid=(BATCH, N_KV, nq),
        in_specs=[pl.BlockSpec((TQ, Q_GROUP * HEAD_DIM), q_map), lat_spec, lat_spec, ctx_spec, ctx_spec],
        out_specs=pl.BlockSpec((TQ, Q_GROUP * HEAD_DIM), q_map),
        out_shape=jax.ShapeDtypeStruct((T if need_ctx else N_LAT, N_HEADS * HEAD_DIM), bf16),
        scratch_shapes=[pltpu.VMEM((Q_GROUP * TQ, HEAD_DIM), f32)] * 3,
        compiler_params=_params(("parallel", "parallel", "arbitrary")),
        name="attn_core",
    )(q, k, v, k, v)


def _proj_res_kernel(y_ref, w_ref, h_ref, mod_ref, o_ref):
    y = jnp.dot(y_ref[...], w_ref[...], preferred_element_type=f32)
    o_ref[...] = h_ref[...] + mod_ref[2:3, :] * y


def _proj_res(y, w, h, mod, nt):
    kdim = y.shape[1]
    return pl.pallas_call(
        _proj_res_kernel,
        grid=(nt,),
        in_specs=[_tile_spec(kdim), _const_spec((kdim, D)), _tile_spec(D), _mod_spec()],
        out_specs=_tile_spec(D),
        out_shape=jax.ShapeDtypeStruct((nt * TM, D), f32),
        compiler_params=_params(("parallel",)),
        name="proj_res",
    )(y, w, h, mod)


def _ffn_kernel(h_ref, mod_ref, g_ref, wg_ref, wu_ref, wd_ref, o_ref, xn_sc, acc_sc):
    f = pl.program_id(1)

    @pl.when(f == 0)
    def _():
        xn_sc[...] = _norm_mod(h_ref[...], g_ref[...], mod_ref[3:4, :], mod_ref[4:5, :]).astype(bf16)
        acc_sc[...] = jnp.zeros_like(acc_sc)

    x = xn_sc[...]
    gate = jnp.dot(x, wg_ref[...], preferred_element_type=f32)
    up = jnp.dot(x, wu_ref[...], preferred_element_type=f32)
    hid = (_silu(gate) * up).astype(bf16)
    acc_sc[...] += jnp.dot(hid, wd_ref[...], preferred_element_type=f32)

    @pl.when(f == NF - 1)
    def _():
        o_ref[...] = h_ref[...] + mod_ref[5:6, :] * acc_sc[...]


def _ffn(h, mod, g, wg, wu, wd, nt):
    return pl.pallas_call(
        _ffn_kernel,
        grid=(nt, NF),
        in_specs=[
            pl.BlockSpec((TM, D), lambda t, f: (t, 0)),
            pl.BlockSpec((None, 8, D), lambda t, f: (_mod_row(t), 0, 0)),
            pl.BlockSpec((1, D), lambda t, f: (0, 0)),
            pl.BlockSpec((D, TF), lambda t, f: (0, f)),
            pl.BlockSpec((D, TF), lambda t, f: (0, f)),
            pl.BlockSpec((TF, D), lambda t, f: (f, 0)),
        ],
        out_specs=pl.BlockSpec((TM, D), lambda t, f: (t, 0)),
        out_shape=jax.ShapeDtypeStruct((nt * TM, D), f32),
        scratch_shapes=[pltpu.VMEM((TM, D), bf16), pltpu.VMEM((TM, D), f32)],
        compiler_params=_params(("parallel", "arbitrary")),
        name="ffn_dense",
    )(h, mod, g, wg, wu, wd)


def _split_bf16(x):
    hi = x.astype(bf16)
    lo = (x - hi.astype(f32)).astype(bf16)
    return hi, lo


def _split3(x):
    p0 = x.astype(bf16)
    r = x - p0.astype(f32)
    p1 = r.astype(bf16)
    p2 = (r - p1.astype(f32)).astype(bf16)
    return p0, p1, p2


def _dot3(a_parts, b):
    out = jnp.dot(a_parts[0], b, preferred_element_type=f32)
    for a in a_parts[1:]:
        out = out + jnp.dot(a, b, preferred_element_type=f32)
    return out


RT = 256
RT_PER_SAMPLE = SEQ // RT
ALIGN = 16
WIN = RT + ALIGN
FT = 512
STAGE = FT + RT
CAP = 17408
CAP_BLOCKS = CAP // FT
DN_T = (((0,), (0,)), ((), ()))


def _route_kernel(h_ref, mod_ref, g_ref, r_ref, xs_hbm, gs_hbm, rank_ref, seg_ref, cnt_ref,
                  x_stage, g_stage, cnt_sc, sems, *, n_rt):
    i = pl.program_id(0)

    @pl.when(i == 0)
    def _():
        x_stage[...] = jnp.zeros_like(x_stage)
        g_stage[...] = jnp.zeros_like(g_stage)
        for e in range(N_EXPERTS):
            cnt_sc[e] = 0

    x = _norm_mod(h_ref[...], g_ref[...], mod_ref[3:4, :], mod_ref[4:5, :])
    xh, xl = _split_bf16(x)
    rh, rl = _split_bf16(r_ref[...])
    logits = (jnp.dot(xh, rh, preferred_element_type=f32) + jnp.dot(xl, rh, preferred_element_type=f32)
              + jnp.dot(xh, rl, preferred_element_type=f32))
    lane = lax.broadcasted_iota(jnp.int32, (RT, 128), 1)
    lg = jnp.where(lane < N_EXPERTS, logits, -jnp.inf)
    m1 = jnp.max(lg, axis=1, keepdims=True)
    i1 = jnp.min(jnp.where(lg == m1, lane, 128), axis=1, keepdims=True)
    lg2 = jnp.where(lane == i1, -jnp.inf, lg)
    m2 = jnp.max(lg2, axis=1, keepdims=True)
    i2 = jnp.min(jnp.where(lg2 == m2, lane, 128), axis=1, keepdims=True)
    e2 = jnp.exp(m2 - m1)
    den = 1.0 + e2
    gates = jnp.where(lane == i1, 1.0 / den, 0.0) + jnp.where(lane == i2, e2 / den, 0.0)
    used = jnp.where((lane == i1) | (lane == i2), 1.0, 0.0).astype(bf16)

    tp = lax.broadcasted_iota(jnp.int32, (RT, RT), 0)
    tt = lax.broadcasted_iota(jnp.int32, (RT, RT), 1)
    earlier = jnp.where(tp < tt, 1.0, 0.0).astype(bf16)
    eye = jnp.where(tp == tt, 1.0, 0.0).astype(bf16)
    rank_t = lax.dot_general(used, earlier, DN_T, preferred_element_type=f32)
    used_t = lax.dot_general(used, eye, DN_T, preferred_element_type=f32)
    rank_ref[...] = jnp.where(used_t[0:N_EXPERTS] > 0, rank_t[0:N_EXPERTS], -1.0)

    g_parts = _split3(gates)
    win_row = lax.broadcasted_iota(jnp.int32, (WIN, RT), 0).astype(f32)

    def flush(e, done_rows):
        dst = pl.multiple_of(e * CAP + done_rows, FT)
        cx = pltpu.make_async_copy(x_stage.at[e, pl.ds(0, FT), :], xs_hbm.at[pl.ds(dst, FT), :], sems.at[0])
        cg = pltpu.make_async_copy(g_stage.at[e, pl.ds(0, FT), :], gs_hbm.at[pl.ds(dst, FT), :], sems.at[1])
        cx.start()
        cg.start()
        cx.wait()
        cg.wait()

    for e in range(N_EXPERTS):
        cnt = cnt_sc[e]
        seg_ref[i * N_EXPERTS + e] = cnt
        fill = cnt % FT
        start = pl.multiple_of((fill // ALIGN) * ALIGN, ALIGN)
        phase = (fill - start).astype(f32)
        n_e = jnp.sum(used_t[e:e + 1, :]).astype(jnp.int32)
        sel = (win_row == rank_t[e:e + 1, :] + phase) & (used_t[e:e + 1, :] > 0)
        p = jnp.where(sel, 1.0, 0.0).astype(bf16)
        rows = jnp.dot(p, xh, preferred_element_type=f32)
        x_stage[e, pl.ds(start, WIN), :] = (x_stage[e, pl.ds(start, WIN), :].astype(f32) + rows).astype(bf16)
        g_stage[e, pl.ds(start, WIN), :] = (
            g_stage[e, pl.ds(start, WIN), :] + jnp.dot(p, g_parts[0], preferred_element_type=f32)
            + jnp.dot(p, g_parts[1], preferred_element_type=f32) + jnp.dot(p, g_parts[2], preferred_element_type=f32))
        cnt_sc[e] = cnt + n_e

        @pl.when(fill + n_e >= FT)
        def _():
            flush(e, cnt - fill)
            x_stage[e, 0:RT, :] = x_stage[e, FT:STAGE, :]
            x_stage[e, RT:STAGE, :] = jnp.zeros((STAGE - RT, D), bf16)
            g_stage[e, 0:RT, :] = g_stage[e, FT:STAGE, :]
            g_stage[e, RT:STAGE, :] = jnp.zeros((STAGE - RT, 128), f32)

    @pl.when(i == n_rt - 1)
    def _():
        for e in range(N_EXPERTS):
            c = cnt_sc[e]
            cnt_ref[e] = c
            flush(e, c - c % FT)


def _moe_route(h, mod, g, router, n_rt):
    return pl.pallas_call(
        functools.partial(_route_kernel, n_rt=n_rt),
        grid=(n_rt,),
        in_specs=[
            pl.BlockSpec((RT, D), lambda i: (i, 0)),
            pl.BlockSpec((None, 8, D), lambda i: (jnp.minimum(i // RT_PER_SAMPLE, CTX_MOD_ROW), 0, 0)),
            _const_spec((1, D)), _const_spec((D, 128)),
        ],
        out_specs=[
            pl.BlockSpec(memory_space=pl.ANY), pl.BlockSpec(memory_space=pl.ANY),
            pl.BlockSpec((None, N_EXPERTS, RT), lambda i: (i, 0, 0)),
            pl.BlockSpec(memory_space=pltpu.SMEM), pl.BlockSpec(memory_space=pltpu.SMEM),
        ],
        out_shape=[
            jax.ShapeDtypeStruct((N_EXPERTS * CAP, D), bf16),
            jax.ShapeDtypeStruct((N_EXPERTS * CAP, 128), f32),
            jax.ShapeDtypeStruct((n_rt, N_EXPERTS, RT), f32),
            jax.ShapeDtypeStruct((n_rt * N_EXPERTS,), jnp.int32),
            jax.ShapeDtypeStruct((N_EXPERTS,), jnp.int32),
        ],
        scratch_shapes=[
            pltpu.VMEM((N_EXPERTS, STAGE, D), bf16), pltpu.VMEM((N_EXPERTS, STAGE, 128), f32),
            pltpu.SMEM((N_EXPERTS,), jnp.int32), pltpu.SemaphoreType.DMA((2,)),
        ],
        compiler_params=_params(("arbitrary",)),
        name="moe_route",
    )(h, mod, g, router)


def _moe_ffn_kernel(te_ref, tb_ref, nt_ref, x_ref, gs_ref, wg_ref, wu_ref, wd_ref, y_ref, acc_sc):
    j = pl.program_id(0)
    f = pl.program_id(1)

    @pl.when(j < nt_ref[0])
    def _():
        @pl.when(f == 0)
        def _():
            acc_sc[...] = jnp.zeros_like(acc_sc)

        x = x_ref[...]
        gate = jnp.dot(x, wg_ref[...], preferred_element_type=f32)
        up = jnp.dot(x, wu_ref[...], preferred_element_type=f32)
        hid = (_silu(gate) * up).astype(bf16)
        acc_sc[...] += jnp.dot(hid, wd_ref[...], preferred_element_type=f32)

        @pl.when(f == NF - 1)
        def _():
            lane = lax.broadcasted_iota(jnp.int32, (FT, 128), 1)
            ge = jnp.sum(jnp.where(lane == te_ref[j], gs_ref[...], 0.0), axis=1, keepdims=True)
            y_ref[...] = (ge * acc_sc[...]).astype(bf16)


def _moe_experts(xs, gs, wg, wu, wd, tile_expert, tile_block, n_tiles):
    max_tiles = tile_expert.shape[0]

    def f_eff(j, f, nt):
        return jnp.where(j < nt[0], f, NF - 1)

    grid_spec = pltpu.PrefetchScalarGridSpec(
        num_scalar_prefetch=3,
        grid=(max_tiles, NF),
        in_specs=[
            pl.BlockSpec((FT, D), lambda j, f, te, tb, nt: (tb[j], 0)),
            pl.BlockSpec((FT, 128), lambda j, f, te, tb, nt: (tb[j], 0)),
            pl.BlockSpec((None, D, TF), lambda j, f, te, tb, nt: (te[j], 0, f_eff(j, f, nt))),
            pl.BlockSpec((None, D, TF), lambda j, f, te, tb, nt: (te[j], 0, f_eff(j, f, nt))),
            pl.BlockSpec((None, TF, D), lambda j, f, te, tb, nt: (te[j], f_eff(j, f, nt), 0)),
        ],
        out_specs=pl.BlockSpec((FT, D), lambda j, f, te, tb, nt: (tb[j], 0)),
        scratch_shapes=[pltpu.VMEM((FT, D), f32)],
    )
    return pl.pallas_call(
        _moe_ffn_kernel,
        grid_spec=grid_spec,
        out_shape=jax.ShapeDtypeStruct((N_EXPERTS * CAP, D), bf16),
        compiler_params=_params(("arbitrary", "arbitrary")),
        name="moe_experts",
    )(tile_expert, tile_block, n_tiles, xs, gs, wg, wu, wd)


def _moe_combine_kernel(seg_ref, nseg_ref, wmax_ref, rank_ref, h_ref, mod_ref, ys_hbm, o_ref, ybuf, acc_sc, sems):
    i = pl.program_id(0)
    acc_sc[...] = jnp.zeros_like(acc_sc)
    win_row = lax.broadcasted_iota(jnp.int32, (WIN, RT), 0).astype(f32)

    def window(e):
        seg = seg_ref[i * N_EXPERTS + e]
        start = pl.multiple_of(jnp.minimum((seg // ALIGN) * ALIGN, wmax_ref[e]), ALIGN)
        copy = pltpu.make_async_copy(ys_hbm.at[pl.ds(e * CAP + start, WIN), :], ybuf.at[e], sems.at[e])
        return seg, start, copy

    for e in range(N_EXPERTS):
        @pl.when(nseg_ref[i * N_EXPERTS + e] > 0)
        def _():
            window(e)[2].start()

    for e in range(N_EXPERTS):
        @pl.when(nseg_ref[i * N_EXPERTS + e] > 0)
        def _():
            seg, start, copy = window(e)
            copy.wait()
            r = rank_ref[e:e + 1, :]
            sel = (win_row == r + (seg - start).astype(f32)) & (r >= 0)
            p = jnp.where(sel, 1.0, 0.0).astype(bf16)
            acc_sc[...] += lax.dot_general(p, ybuf[e], DN_T, preferred_element_type=f32)

    o_ref[...] = h_ref[...] + mod_ref[5:6, :] * acc_sc[...]


def _moe_combine(ys, rank, seg, nseg, wmax, h, mod, n_rt):
    grid_spec = pltpu.PrefetchScalarGridSpec(
        num_scalar_prefetch=3,
        grid=(n_rt,),
        in_specs=[
            pl.BlockSpec((None, N_EXPERTS, RT), lambda i, *_: (i, 0, 0)),
            pl.BlockSpec((RT, D), lambda i, *_: (i, 0)),
            pl.BlockSpec((None, 8, D), lambda i, *_: (jnp.minimum(i // RT_PER_SAMPLE, CTX_MOD_ROW), 0, 0)),
            pl.BlockSpec(memory_space=pl.ANY),
        ],
        out_specs=pl.BlockSpec((RT, D), lambda i, *_: (i, 0)),
        scratch_shapes=[pltpu.VMEM((N_EXPERTS, WIN, D), bf16), pltpu.VMEM((RT, D), f32),
                        pltpu.SemaphoreType.DMA((N_EXPERTS,))],
    )
    return pl.pallas_call(
        _moe_combine_kernel,
        grid_spec=grid_spec,
        out_shape=jax.ShapeDtypeStruct((n_rt * RT, D), f32),
        compiler_params=_params(("arbitrary",)),
        name="moe_combine",
    )(seg, nseg, wmax, rank, h, mod, ys)


def _moe(h, mod, g, router, wg, wu, wd, nt):
    n_rt = nt * (TM // RT)
    xs, gs, rank, seg, counts = _moe_route(h, mod, g, router, n_rt)
    tiles = (counts + FT - 1) // FT
    ends = jnp.cumsum(tiles)
    n_tiles = ends[-1]
    max_tiles = (2 * n_rt * RT) // FT + N_EXPERTS
    j = jnp.minimum(jnp.arange(max_tiles, dtype=jnp.int32), n_tiles - 1)
    tile_expert = jnp.sum((j[:, None] >= ends[None, :]).astype(jnp.int32), axis=1)
    tile_block = tile_expert * CAP_BLOCKS + j - (ends - tiles)[tile_expert]
    ys = _moe_experts(xs, gs, wg, wu, wd, tile_expert, tile_block, n_tiles.reshape(1))
    seg2 = seg.reshape(n_rt, N_EXPERTS)
    nseg = (jnp.concatenate([seg2[1:], counts[None, :]], axis=0) - seg2).reshape(-1)
    wmax = jnp.maximum(tiles * FT - WIN, 0)
    return _moe_combine(ys, rank, seg, nseg, wmax, h, mod, n_rt)


def _ssd_in_kernel(h_ref, mod_ref, g_ref, wz_ref, wx_ref, wdt_ref, z_ref, x_ref, dt_ref):
    a = _norm_mod(h_ref[...], g_ref[...], mod_ref[0:1, :], mod_ref[1:2, :]).astype(bf16)
    for j in range(SSD_INNER // D):
        z_ref[:, j * D:(j + 1) * D] = jnp.dot(a, wz_ref[:, j * D:(j + 1) * D], preferred_element_type=f32).astype(bf16)
    for j in range(SSD_CONV_CH // D):
        x_ref[:, j * D:(j + 1) * D] = jnp.dot(a, wx_ref[:, j * D:(j + 1) * D], preferred_element_type=f32).astype(bf16)
    dt_ref[...] = jnp.dot(a, wdt_ref[...], preferred_element_type=f32)


def _ssd_in_proj(h, mod, g, wz, wx, wdt):
    return pl.pallas_call(
        _ssd_in_kernel,
        grid=(NT,),
        in_specs=[_tile_spec(D), _mod_spec(), _const_spec((1, D)), _const_spec((D, SSD_INNER)),
                  _const_spec((D, SSD_CONV_CH)), _const_spec((D, 256))],
        out_specs=[_tile_spec(SSD_INNER), _tile_spec(SSD_CONV_CH), _tile_spec(256)],
        out_shape=[jax.ShapeDtypeStruct((T, SSD_INNER), bf16), jax.ShapeDtypeStruct((T, SSD_CONV_CH), bf16),
                   jax.ShapeDtypeStruct((T, 256), f32)],
        compiler_params=_params(("parallel",)),
        name="ssd_in_proj",
    )(h, mod, g, wz, wx, wdt)


CONV_TM = 256
CONV_HALO = 16
CONV_TILES_PER_SAMPLE = SEQ // CONV_TM


def _conv_kernel(x_ref, prev_ref, next_ref, w_ref, b_ref, o_ref):
    i = pl.program_id(0)
    is_ctx = i >= N_LAT // CONV_TM
    seg_start = is_ctx | (i % CONV_TILES_PER_SAMPLE == 0)
    seg_end = is_ctx | (i % CONV_TILES_PER_SAMPLE == CONV_TILES_PER_SAMPLE - 1)
    x = x_ref[...].astype(f32)
    prev_row = jnp.where(seg_start, 0.0, prev_ref[CONV_HALO - 1:CONV_HALO, :].astype(f32))
    next_row = jnp.where(seg_end, 0.0, next_ref[0:1, :].astype(f32))
    row = lax.broadcasted_iota(jnp.int32, x.shape, 0)
    xm1 = jnp.where(row == 0, prev_row, pltpu.roll(x, 1, 0))
    xp1 = jnp.where(row == CONV_TM - 1, next_row, pltpu.roll(x, CONV_TM - 1, 0))
    y = w_ref[0:1, :] * xm1 + w_ref[1:2, :] * x + w_ref[2:3, :] * xp1 + b_ref[...]
    o_ref[...] = _silu(y).astype(bf16)


def _ssd_conv(xbc, conv_w, conv_b):
    n = T // CONV_TM
    per = CONV_TM // CONV_HALO
    last = T // CONV_HALO - 1
    return pl.pallas_call(
        _conv_kernel,
        grid=(n,),
        in_specs=[
            pl.BlockSpec((CONV_TM, SSD_CONV_CH), lambda i: (i, 0)),
            pl.BlockSpec((CONV_HALO, SSD_CONV_CH), lambda i: (jnp.maximum(i * per - 1, 0), 0)),
            pl.BlockSpec((CONV_HALO, SSD_CONV_CH), lambda i: (jnp.minimum((i + 1) * per, last), 0)),
            _const_spec((8, SSD_CONV_CH)), _const_spec((1, SSD_CONV_CH)),
        ],
        out_specs=pl.BlockSpec((CONV_TM, SSD_CONV_CH), lambda i: (i, 0)),
        out_shape=jax.ShapeDtypeStruct((T, SSD_CONV_CH), bf16),
        compiler_params=_params(("parallel",)),
        name="ssd_conv",
    )(xbc, xbc, xbc, conv_w, conv_b)


def _ssd_scan_kernel(x_ref, b_ref, c_ref, dt_ref, bias_ref, alog_ref, y_ref, state_sc):
    d = pl.program_id(0)
    c = pl.program_id(2)

    @pl.when(c == 0)
    def _():
        state_sc[...] = jnp.zeros_like(state_sc)

    L = SSD_CHUNK
    li = lax.broadcasted_iota(jnp.int32, (L, L), 0)
    si = lax.broadcasted_iota(jnp.int32, (L, L), 1)
    causal = (si - li) * (1 - 2 * d) <= 0
    tri = jnp.where(causal, 1.0, 0.0).astype(bf16)
    hi = lax.broadcasted_iota(jnp.int32, (128, SSD_INNER), 0)
    ci = lax.broadcasted_iota(jnp.int32, (128, SSD_INNER), 1)
    expand = jnp.where(ci // SSD_P == hi, 1.0, 0.0).astype(bf16)

    dt = jax.nn.softplus(dt_ref[...] + bias_ref[...])
    a_neg = -jnp.exp(alog_ref[...])
    da = dt * a_neg
    da_parts = _split3(da)
    cs = (jnp.dot(tri, da_parts[0], preferred_element_type=f32) + jnp.dot(tri, da_parts[1], preferred_element_type=f32)
          + jnp.dot(tri, da_parts[2], preferred_element_type=f32))
    dn_t = (((0,), (1,)), ((), ()))
    cs_t = (lax.dot_general(da_parts[0], tri, dn_t, preferred_element_type=f32)
            + lax.dot_general(da_parts[1], tri, dn_t, preferred_element_type=f32)
            + lax.dot_general(da_parts[2], tri, dn_t, preferred_element_type=f32))
    total = jnp.sum(da, axis=0, keepdims=True)

    cs_x = _dot3(_split3(cs), expand)
    dt_x = _dot3(_split3(dt), expand)
    total_x = _dot3(_split3(jnp.broadcast_to(total, (8, 128))), expand)[0:1]

    x = x_ref[...].astype(f32)
    xdt = (x * dt_x).astype(bf16)
    xw = (x * (jnp.exp(total_x - cs_x) * dt_x)).astype(bf16)
    out_scale = jnp.exp(cs_x)
    chunk_decay = jnp.exp(total_x)

    for g in range(SSD_GROUPS):
        bg = b_ref[:, g * SSD_N:(g + 1) * SSD_N]
        cg = c_ref[:, g * SSD_N:(g + 1) * SSD_N]
        cb = lax.dot_general(cg, bg, (((1,), (1,)), ((), ())), preferred_element_type=f32)
        gsl = slice(g * SSD_HPG * SSD_P, (g + 1) * SSD_HPG * SSD_P)
        st = state_sc[g]
        y_off = jnp.dot(cg, st.astype(bf16), preferred_element_type=f32) * out_scale[:, gsl]
        ys = []
        for r in range(SSD_HPG):
            hd = g * SSD_HPG + r
            seg = cs[:, hd:hd + 1] - cs_t[hd:hd + 1, :]
            decay = jnp.exp(jnp.where(causal, seg, -jnp.inf))
            m = (cb * decay).astype(bf16)
            ys.append(jnp.dot(m, xdt[:, hd * SSD_P:(hd + 1) * SSD_P], preferred_element_type=f32))
        y_ref[:, gsl] = (jnp.concatenate(ys, axis=1) + y_off).astype(bf16)
        new = lax.dot_general(bg, xw[:, gsl], (((0,), (0,)), ((), ())), preferred_element_type=f32)
        state_sc[g] = st * chunk_decay[:, gsl] + new


SSD_NCHUNK = (SEQ + CTX) // SSD_CHUNK
SSD_CTX_CHUNKS = CTX // SSD_CHUNK
SSD_LAT_CHUNKS = SEQ // SSD_CHUNK


def _ssd_chunk_block(d, b, c):
    ctx_j = jnp.where(d == 0, c, SSD_CTX_CHUNKS - 1 - c)
    lat_j = jnp.where(d == 0, c - SSD_CTX_CHUNKS, SSD_LAT_CHUNKS - 1 - (c - SSD_CTX_CHUNKS))
    return jnp.where(c < SSD_CTX_CHUNKS, N_LAT // SSD_CHUNK + b * SSD_CTX_CHUNKS + ctx_j, b * SSD_LAT_CHUNKS + lat_j)


def _ssd_scan(xbc, dt, dt_bias, a_log):
    x_blk = SSD_INNER // SSD_BC
    return pl.pallas_call(
        _ssd_scan_kernel,
        grid=(2, BATCH, SSD_NCHUNK),
        in_specs=[
            pl.BlockSpec((SSD_CHUNK, SSD_INNER), lambda d, b, c: (_ssd_chunk_block(d, b, c), 0)),
            pl.BlockSpec((SSD_CHUNK, SSD_BC), lambda d, b, c: (_ssd_chunk_block(d, b, c), x_blk)),
            pl.BlockSpec((SSD_CHUNK, SSD_BC), lambda d, b, c: (_ssd_chunk_block(d, b, c), x_blk + 1)),
            pl.BlockSpec((SSD_CHUNK, 128), lambda d, b, c: (_ssd_chunk_block(d, b, c), d)),
            pl.BlockSpec((None, 1, 128), lambda d, b, c: (d, 0, 0)),
            pl.BlockSpec((None, 1, 128), lambda d, b, c: (d, 0, 0)),
        ],
        out_specs=pl.BlockSpec((None, SSD_CHUNK, SSD_INNER), lambda d, b, c: (d, _ssd_chunk_block(d, b, c), 0)),
        out_shape=jax.ShapeDtypeStruct((2, T, SSD_INNER), bf16),
        scratch_shapes=[pltpu.VMEM((SSD_GROUPS, SSD_N, SSD_HPG * SSD_P), f32)],
        compiler_params=_params(("parallel", "parallel", "arbitrary")),
        name="ssd_scan",
    )(xbc, xbc, xbc, dt, dt_bias, a_log)


def _ssd_out_kernel(yf_ref, yb_ref, x_ref, z_ref, dskip_ref, ng_ref, w_ref, h_ref, mod_ref, o_ref):
    y = yf_ref[...].astype(f32) + yb_ref[...].astype(f32) + x_ref[...].astype(f32) * dskip_ref[...]
    gated = y * _silu(z_ref[...].astype(f32))
    gw = SSD_INNER // SSD_GROUPS
    parts = []
    for g in range(SSD_GROUPS):
        s = gated[:, g * gw:(g + 1) * gw]
        parts.append(s * lax.rsqrt(jnp.mean(s * s, axis=-1, keepdims=True) + EPS))
    normed = (jnp.concatenate(parts, axis=1) * ng_ref[...]).astype(bf16)
    out = jnp.dot(normed, w_ref[...], preferred_element_type=f32)
    o_ref[...] = h_ref[...] + mod_ref[2:3, :] * out


def _ssd_out(y2, xbc, z, dskip, ng, w_out, h, mod, nt):
    return pl.pallas_call(
        _ssd_out_kernel,
        grid=(nt,),
        in_specs=[
            pl.BlockSpec((None, TM, SSD_INNER), lambda t: (0, t, 0)),
            pl.BlockSpec((None, TM, SSD_INNER), lambda t: (1, t, 0)),
            pl.BlockSpec((TM, SSD_INNER), lambda t: (t, 0)),
            _tile_spec(SSD_INNER), _const_spec((1, SSD_INNER)), _const_spec((1, SSD_INNER)),
            _const_spec((SSD_INNER, D)), _tile_spec(D), _mod_spec(),
        ],
        out_specs=_tile_spec(D),
        out_shape=jax.ShapeDtypeStruct((nt * TM, D), f32),
        compiler_params=_params(("parallel",)),
        name="ssd_out",
    )(y2, y2, xbc, z, dskip, ng, w_out, h, mod)


def _gelu(x):
    return 0.5 * x * (1.0 + lax.erf(x * math.sqrt(0.5)))


def _cmlp_kernel(h_ref, mod_ref, g_ref, wu_ref, wv_ref, bu_ref, bv_ref, vg_ref, ws_ref, bs_ref, wo_ref, o_ref, uv_sc):
    a = _norm_mod(h_ref[...], g_ref[...], mod_ref[0:1, :], mod_ref[1:2, :]).astype(bf16)
    v = _gelu(jnp.dot(a, wv_ref[...], preferred_element_type=f32) + bv_ref[...])
    v = (v * lax.rsqrt(jnp.mean(v * v, axis=-1, keepdims=True) + EPS) * vg_ref[...]).astype(bf16)
    u = _gelu(jnp.dot(a, wu_ref[...], preferred_element_type=f32) + bu_ref[...])
    for ck in range(TM // CMLP_CHUNK):
        rows = slice(ck * CMLP_CHUNK, (ck + 1) * CMLP_CHUNK)
        for g in range(CMLP_GROUPS):
            cols = slice(g * CMLP_GW, (g + 1) * CMLP_GW)
            mixed = jnp.dot(ws_ref[g], v[rows, cols], preferred_element_type=f32) + bs_ref[:, cols]
            uv_sc[rows, cols] = (u[rows, cols] * mixed).astype(bf16)
    out = jnp.dot(uv_sc[...], wo_ref[...], preferred_element_type=f32)
    o_ref[...] = h_ref[...] + mod_ref[2:3, :] * out


def _cmlp(h, mod, g, wu, wv, bu, bv, vg, ws, bs, wo, nt):
    return pl.pallas_call(
        _cmlp_kernel,
        grid=(nt,),
        in_specs=[
            _tile_spec(D), _mod_spec(), _const_spec((1, D)), _const_spec((D, CMLP_D)), _const_spec((D, CMLP_D)),
            _const_spec((1, CMLP_D)), _const_spec((1, CMLP_D)), _const_spec((1, CMLP_D)),
            _const_spec((CMLP_GROUPS, CMLP_CHUNK, CMLP_CHUNK)), _const_spec((CMLP_CHUNK, CMLP_D)),
            _const_spec((CMLP_D, D)),
        ],
        out_specs=_tile_spec(D),
        out_shape=jax.ShapeDtypeStruct((nt * TM, D), f32),
        scratch_shapes=[pltpu.VMEM((TM, CMLP_D), bf16)],
        compiler_params=_params(("parallel",)),
        name="cmlp",
    )(h, mod, g, wu, wv, bu, bv, vg, ws, bs, wo)


def _final_kernel(h_ref, g_ref, o_ref):
    x = h_ref[...]
    o_ref[...] = x * lax.rsqrt(jnp.mean(x * x, axis=-1, keepdims=True) + EPS) * g_ref[...]


def _final_norm(h, g):
    return pl.pallas_call(
        _final_kernel,
        grid=(NT_LAT,),
        in_specs=[_tile_spec(D), _const_spec((1, D))],
        out_specs=_tile_spec(D),
        out_shape=jax.ShapeDtypeStruct((N_LAT, D), f32),
        compiler_params=_params(("parallel",)),
        name="final_norm",
    )(h, g)


def _rope_tables():
    pos = jnp.arange(SEQ)
    inv_freq = 1.0 / (ROPE_THETA ** (jnp.arange(0, ROPE_AXIS_DIM, 2, dtype=f32) / ROPE_AXIS_DIM))
    ang_r = (pos // GRID_W).astype(f32)[:, None] * inv_freq
    ang_c = (pos % GRID_W).astype(f32)[:, None] * inv_freq
    cos = jnp.concatenate([jnp.cos(ang_r)] * 2 + [jnp.cos(ang_c)] * 2, axis=1)
    sin = jnp.concatenate([-jnp.sin(ang_r), jnp.sin(ang_r), -jnp.sin(ang_c), jnp.sin(ang_c)], axis=1)
    cos = jnp.concatenate([cos, jnp.ones((TM, HEAD_DIM), f32)], axis=0)
    sin = jnp.concatenate([sin, jnp.zeros((TM, HEAD_DIM), f32)], axis=0)
    return cos, sin


def _row(v):
    return v.reshape(1, -1)


def _pad_lanes(v, width=128):
    return jnp.pad(v, ((0, 0), (0, width - v.shape[1])))


def kernel(x, c, ctx, c_ctx, w_mod, b_mod, norm1_g, norm2_g, attn_w_qkv, attn_q_g, attn_k_g, attn_w_o, ssd_w_in, ssd_conv_w, ssd_conv_b, ssd_dt_bias_f, ssd_dt_bias_b, ssd_a_log_f, ssd_a_log_b, ssd_d_skip, ssd_norm_g, ssd_w_out, cmlp_w_in, cmlp_b_in, cmlp_v_g, cmlp_w_s, cmlp_b_s, cmlp_w_out, ffn_w_gate, ffn_w_up, ffn_w_down, moe_router, moe_w_gate, moe_w_up, moe_w_down, final_g):
    h = jnp.concatenate([x.reshape(N_LAT, D), ctx.reshape(N_CTX, D)], axis=0)
    cond = jnp.concatenate([c, c_ctx[None, :], jnp.zeros((8 - BATCH - 1, D), f32)], axis=0)
    mods = _modulation(cond, w_mod, b_mod)
    cos_t, sin_t = _rope_tables()

    for i in range(DEPTH):
        need_ctx = i < DEPTH - 1
        nt = NT if need_ctx else NT_LAT
        mod = mods[i]
        kind, j = i % 3, i // 3
        g1 = _row(norm1_g[i])
        if kind == 0:
            q, k, v = _qkv_proj(h, mod, g1, attn_w_qkv[j].astype(bf16), _row(attn_q_g[j]), _row(attn_k_g[j]), cos_t, sin_t)
            o = _attention(q, k, v, need_ctx)
            h = _proj_res(o, attn_w_o[j].astype(bf16), h, mod, nt)
        elif kind == 1:
            w_in = ssd_w_in[j]
            wz = w_in[:, :SSD_INNER].astype(bf16)
            wx = w_in[:, SSD_INNER:SSD_INNER + SSD_CONV_CH].astype(bf16)
            w_dt = w_in[:, SSD_INNER + SSD_CONV_CH:]
            wdt = jnp.concatenate([_pad_lanes(w_dt[:, :SSD_HEADS]), _pad_lanes(w_dt[:, SSD_HEADS:])], axis=1).astype(bf16)
            z, xbc, dt = _ssd_in_proj(h, mod, g1, wz, wx, wdt)
            conv_w = jnp.pad(ssd_conv_w[j], ((0, 8 - ssd_conv_w.shape[1]), (0, 0)))
            xbc = _ssd_conv(xbc, conv_w, _row(ssd_conv_b[j]))
            dt_bias = jnp.stack([_pad_lanes(_row(ssd_dt_bias_f[j])), _pad_lanes(_row(ssd_dt_bias_b[j]))])
            a_log = jnp.stack([_pad_lanes(_row(ssd_a_log_f[j])), _pad_lanes(_row(ssd_a_log_b[j]))])
            y2 = _ssd_scan(xbc, dt, dt_bias, a_log)
            dskip = _row(jnp.repeat(ssd_d_skip[j], SSD_P))
            h = _ssd_out(y2, xbc, z, dskip, _row(ssd_norm_g[j]), ssd_w_out[j].astype(bf16), h, mod, nt)
        else:
            w_in = cmlp_w_in[j]
            b_in = cmlp_b_in[j]
            bs = jnp.repeat(cmlp_b_s[j].T, CMLP_GW, axis=1)
            h = _cmlp(h, mod, g1, w_in[:, :CMLP_D].astype(bf16), w_in[:, CMLP_D:].astype(bf16),
                      _row(b_in[:CMLP_D]), _row(b_in[CMLP_D:]), _row(cmlp_v_g[j]), cmlp_w_s[j].astype(bf16), bs,
                      cmlp_w_out[j].astype(bf16), nt)
        kk = i // 2
        g2 = _row(norm2_g[i])
        if i % 2 == 0:
            h = _ffn(h, mod, g2, ffn_w_gate[kk].astype(bf16), ffn_w_up[kk].astype(bf16), ffn_w_down[kk].astype(bf16), nt)
        else:
            h = _moe(h, mod, g2, _pad_lanes(moe_router[kk]), moe_w_gate[kk].astype(bf16), moe_w_up[kk].astype(bf16),
                     moe_w_down[kk].astype(bf16), nt)
    return _final_norm(h, _row(final_g)).reshape(BATCH, SEQ, D)
```

```python
import functools
import math

import jax
import jax.numpy as jnp
from jax import lax
from jax.experimental import pallas as pl
from jax.experimental.pallas import tpu as pltpu

f32 = jnp.float32
bf16 = jnp.bfloat16

D = 1024
BATCH = 2
SEQ = 8192
CTX = 256
DEPTH = 4
GRID_W = 64
EPS = 1e-6
N_MOD = 6

HEAD_DIM = 128
N_HEADS = 8
N_KV = 2
Q_GROUP = 4
ROPE_AXIS_DIM = 64
ROPE_THETA = 10000.0

SSD_INNER = 2048
SSD_P = 64
SSD_HEADS = 32
SSD_GROUPS = 4
SSD_HPG = 8
SSD_N = 128
SSD_CHUNK = 128
SSD_BC = SSD_GROUPS * SSD_N
SSD_CONV_CH = SSD_INNER + 2 * SSD_BC

CMLP_D = 2048
CMLP_GROUPS = 8
CMLP_GW = 256
CMLP_CHUNK = 128

D_FF = 3584
N_EXPERTS = 8

N_LAT = BATCH * SEQ
N_CTX = BATCH * CTX
T = N_LAT + N_CTX
TM = 512
NT = T // TM
NT_LAT = N_LAT // TM
TILES_PER_SAMPLE = SEQ // TM
CTX_MOD_ROW = BATCH

TF = 1792
NF = D_FF // TF

TQ = 256
TK = 512
ATT_CHUNKS = SEQ // TK
VW = 2 * HEAD_DIM
LOG2E = math.log2(math.e)
NQ_LAT = SEQ // TQ

VMEM_LIMIT = 56 * 1024 * 1024


def _mod_row(t):
    return jnp.minimum(t // TILES_PER_SAMPLE, CTX_MOD_ROW)


def _tile_spec(width):
    return pl.BlockSpec((TM, width), lambda t: (t, 0))


def _mod_spec():
    return pl.BlockSpec((None, 8, D), lambda t: (_mod_row(t), 0, 0))


def _const_spec(shape):
    n = len(shape)
    return pl.BlockSpec(shape, lambda *_: (0,) * n)


def _params(semantics):
    return pltpu.CompilerParams(dimension_semantics=semantics, vmem_limit_bytes=VMEM_LIMIT)


def _silu(x):
    return x * jax.nn.sigmoid(x)


def _norm_mod(x, g, shift, scale):
    y = x * lax.rsqrt(jnp.mean(x * x, axis=-1, keepdims=True) + EPS) * g
    return y * (1.0 + scale) + shift


def _mod_kernel(c_ref, w_ref, b_ref, o_ref):
    s = _silu(c_ref[...])
    o_ref[...] = jnp.dot(s.astype(bf16), w_ref[...].astype(bf16), preferred_element_type=f32) + b_ref[...]


def _modulation(cond, w_mod, b_mod):
    nblk = N_MOD * D // D
    out = pl.pallas_call(
        _mod_kernel,
        grid=(DEPTH, nblk),
        in_specs=[
            pl.BlockSpec((8, D), lambda i, j: (0, 0)),
            pl.BlockSpec((None, D, D), lambda i, j: (i, 0, j)),
            pl.BlockSpec((None, 1, D), lambda i, j: (i, 0, j)),
        ],
        out_specs=pl.BlockSpec((None, 8, D), lambda i, j: (i, 0, j)),
        out_shape=jax.ShapeDtypeStruct((DEPTH, 8, N_MOD * D), f32),
        compiler_params=_params(("arbitrary", "arbitrary")),
        name="modulation",
    )(cond, w_mod, b_mod.reshape(DEPTH, 1, N_MOD * D))
    mod = out[:, :3].reshape(DEPTH, 3, N_MOD, D)
    return jnp.pad(mod, ((0, 0), (0, 0), (0, 8 - N_MOD), (0, 0)))


def _qkv_kernel(h_ref, mod_ref, g_ref, w_ref, qg_ref, kg_ref, cos_ref, sin_ref, q_ref, k_ref, v_ref):
    a = _norm_mod(h_ref[...], g_ref[...], mod_ref[0:1, :], mod_ref[1:2, :])
    qkv = jnp.dot(a.astype(bf16), w_ref[...], preferred_element_type=f32)
    cos = cos_ref[...]
    sin = sin_ref[...]
    lane = lax.broadcasted_iota(jnp.int32, (TM, HEAD_DIM), 1)
    first_half = (lane % ROPE_AXIS_DIM) < (ROPE_AXIS_DIM // 2)

    def head(x, g, scale):
        y = x * lax.rsqrt(jnp.mean(x * x, axis=-1, keepdims=True) + EPS) * g
        partner = jnp.where(first_half, pltpu.roll(y, HEAD_DIM - 32, 1), pltpu.roll(y, 32, 1))
        return (y * cos + partner * sin) * scale

    qg = qg_ref[...]
    kg = kg_ref[...]
    for hh in range(N_HEADS):
        sl = slice(hh * HEAD_DIM, (hh + 1) * HEAD_DIM)
        q_ref[:, sl] = head(qkv[:, sl], qg, HEAD_DIM ** -0.5 * LOG2E).astype(bf16)
    for hh in range(N_KV):
        sl = slice(hh * HEAD_DIM, (hh + 1) * HEAD_DIM)
        src = slice((N_HEADS + hh) * HEAD_DIM, (N_HEADS + hh + 1) * HEAD_DIM)
        k_ref[:, sl] = head(qkv[:, src], kg, 1.0).astype(bf16)
        v_ref[:, hh * VW:hh * VW + HEAD_DIM] = qkv[:, (N_HEADS + N_KV + hh) * HEAD_DIM:(N_HEADS + N_KV + hh + 1) * HEAD_DIM].astype(bf16)
        v_ref[:, hh * VW + HEAD_DIM:(hh + 1) * VW] = jnp.ones((TM, HEAD_DIM), bf16)


def _qkv_proj(h, mod, g, w_qkv, q_g, k_g, cos_t, sin_t):
    rope_spec = pl.BlockSpec((TM, HEAD_DIM), lambda t: (jnp.where(t < NT_LAT, t % TILES_PER_SAMPLE, TILES_PER_SAMPLE), 0))
    return pl.pallas_call(
        _qkv_kernel,
        grid=(NT,),
        in_specs=[
            _tile_spec(D), _mod_spec(), _const_spec((1, D)), _const_spec((D, (N_HEADS + 2 * N_KV) * HEAD_DIM)),
            _const_spec((1, HEAD_DIM)), _const_spec((1, HEAD_DIM)), rope_spec, rope_spec,
        ],
        out_specs=[_tile_spec(N_HEADS * HEAD_DIM), _tile_spec(N_KV * HEAD_DIM), _tile_spec(N_KV * VW)],
        out_shape=[
            jax.ShapeDtypeStruct((T, N_HEADS * HEAD_DIM), bf16),
            jax.ShapeDtypeStruct((T, N_KV * HEAD_DIM), bf16),
            jax.ShapeDtypeStruct((T, N_KV * VW), bf16),
        ],
        compiler_params=_params(("parallel",)),
        name="attn_qkv",
    )(h, mod, g, w_qkv, q_g, k_g, cos_t, sin_t)


def _attn_kernel(q_ref, kl_ref, vl_ref, kc_ref, vc_ref, o_ref, m_sc, acc_sc, s_a, s_b):
    qi = pl.program_id(2)
    q = q_ref[...]
    qs = jnp.concatenate([q[:, i * HEAD_DIM:(i + 1) * HEAD_DIM] for i in range(Q_GROUP)], axis=0)

    def scores(k):
        return lax.dot_general(qs, k, (((1,), (1,)), ((), ())), preferred_element_type=f32)

    def absorb(s, v):
        m_prev = m_sc[...]
        m_next = jnp.maximum(m_prev, jnp.max(s, axis=1, keepdims=True))
        alpha = jnp.exp2(m_prev - m_next)
        p = jnp.exp2(s - jnp.concatenate([m_next] * (s.shape[1] // HEAD_DIM), axis=1))
        acc_sc[...] = (jnp.concatenate([alpha, alpha], axis=1) * acc_sc[...]
                       + jnp.dot(p.astype(bf16), v, preferred_element_type=f32))
        m_sc[...] = m_next

    m_sc[...] = jnp.full_like(m_sc, -jnp.inf)
    acc_sc[...] = jnp.zeros_like(acc_sc)
    absorb(scores(kc_ref[...]), vc_ref[...])

    @pl.when(qi < NQ_LAT)
    def _():
        def chunk(ref, c):
            return ref[pl.ds(pl.multiple_of(c * TK, TK), TK), :]

        s_a[...] = scores(chunk(kl_ref, 0))

        def body(c2, carry):
            c = 2 * c2
            s_b[...] = scores(chunk(kl_ref, c + 1))
            absorb(s_a[...], chunk(vl_ref, c))
            s_a[...] = scores(chunk(kl_ref, jnp.minimum(c + 2, ATT_CHUNKS - 1)))
            absorb(s_b[...], chunk(vl_ref, c + 1))
            return carry
        lax.fori_loop(0, ATT_CHUNKS // 2, body, 0)

    acc = acc_sc[...]
    o = acc[:, :HEAD_DIM] / acc[:, HEAD_DIM:]
    for i in range(Q_GROUP):
        o_ref[:, i * HEAD_DIM:(i + 1) * HEAD_DIM] = o[i * TQ:(i + 1) * TQ].astype(bf16)


def _attention(q, k, v, need_ctx):
    nq = NQ_LAT + (1 if need_ctx else 0)
    ctx_blk = N_LAT // CTX

    def q_map(b, kh, qi):
        return (jnp.where(qi < NQ_LAT, b * NQ_LAT + qi, N_LAT // TQ + b), kh)

    def lat_spec(width):
        return pl.BlockSpec((SEQ, width), lambda b, kh, qi: (b, kh))

    def ctx_spec(width):
        return pl.BlockSpec((CTX, width), lambda b, kh, qi: (ctx_blk + b, kh))

    return pl.pallas_call(
        _attn_kernel,
        grid=(BATCH, N_KV, nq),
        in_specs=[pl.BlockSpec((TQ, Q_GROUP * HEAD_DIM), q_map), lat_spec(HEAD_DIM), lat_spec(VW),
                  ctx_spec(HEAD_DIM), ctx_spec(VW)],
        out_specs=pl.BlockSpec((TQ, Q_GROUP * HEAD_DIM), q_map),
        out_shape=jax.ShapeDtypeStruct((T if need_ctx else N_LAT, N_HEADS * HEAD_DIM), bf16),
        scratch_shapes=[pltpu.VMEM((Q_GROUP * TQ, HEAD_DIM), f32), pltpu.VMEM((Q_GROUP * TQ, VW), f32),
                        pltpu.VMEM((Q_GROUP * TQ, TK), f32), pltpu.VMEM((Q_GROUP * TQ, TK), f32)],
        compiler_params=_params(("parallel", "parallel", "arbitrary")),
        name="attn_core",
    )(q, k, v, k, v)


def _proj_res_kernel(y_ref, w_ref, h_ref, mod_ref, o_ref):
    y = jnp.dot(y_ref[...], w_ref[...], preferred_element_type=f32)
    o_ref[...] = h_ref[...] + mod_ref[2:3, :] * y


def _proj_res(y, w, h, mod, nt):
    kdim = y.shape[1]
    return pl.pallas_call(
        _proj_res_kernel,
        grid=(nt,),
        in_specs=[_tile_spec(kdim), _const_spec((kdim, D)), _tile_spec(D), _mod_spec()],
        out_specs=_tile_spec(D),
        out_shape=jax.ShapeDtypeStruct((nt * TM, D), f32),
        compiler_params=_params(("parallel",)),
        name="proj_res",
    )(y, w, h, mod)


def _ffn_kernel(h_ref, mod_ref, g_ref, wg_ref, wu_ref, wd_ref, o_ref, xn_sc, acc_sc):
    f = pl.program_id(1)

    @pl.when(f == 0)
    def _():
        xn_sc[...] = _norm_mod(h_ref[...], g_ref[...], mod_ref[3:4, :], mod_ref[4:5, :]).astype(bf16)
        acc_sc[...] = jnp.zeros_like(acc_sc)

    x = xn_sc[...]
    gate = jnp.dot(x, wg_ref[...], preferred_element_type=f32)
    up = jnp.dot(x, wu_ref[...], preferred_element_type=f32)
    hid = (_silu(gate) * up).astype(bf16)
    acc_sc[...] += jnp.dot(hid, wd_ref[...], preferred_element_type=f32)

    @pl.when(f == NF - 1)
    def _():
        o_ref[...] = h_ref[...] + mod_ref[5:6, :] * acc_sc[...]


def _ffn(h, mod, g, wg, wu, wd, nt):
    return pl.pallas_call(
        _ffn_kernel,
        grid=(nt, NF),
        in_specs=[
            pl.BlockSpec((TM, D), lambda t, f: (t, 0)),
            pl.BlockSpec((None, 8, D), lambda t, f: (_mod_row(t), 0, 0)),
            pl.BlockSpec((1, D), lambda t, f: (0, 0)),
            pl.BlockSpec((D, TF), lambda t, f: (0, f)),
            pl.BlockSpec((D, TF), lambda t, f: (0, f)),
            pl.BlockSpec((TF, D), lambda t, f: (f, 0)),
        ],
        out_specs=pl.BlockSpec((TM, D), lambda t, f: (t, 0)),
        out_shape=jax.ShapeDtypeStruct((nt * TM, D), f32),
        scratch_shapes=[pltpu.VMEM((TM, D), bf16), pltpu.VMEM((TM, D), f32)],
        compiler_params=_params(("parallel", "arbitrary")),
        name="ffn_dense",
    )(h, mod, g, wg, wu, wd)


def _split_bf16(x):
    hi = x.astype(bf16)
    lo = (x - hi.astype(f32)).astype(bf16)
    return hi, lo


def _split3(x):
    p0 = x.astype(bf16)
    r = x - p0.astype(f32)
    p1 = r.astype(bf16)
    p2 = (r - p1.astype(f32)).astype(bf16)
    return p0, p1, p2


def _dot3(a_parts, b):
    out = jnp.dot(a_parts[0], b, preferred_element_type=f32)
    for a in a_parts[1:]:
        out = out + jnp.dot(a, b, preferred_element_type=f32)
    return out


RT = 256
RT_PER_SAMPLE = SEQ // RT
ALIGN = 16
WIN = RT + ALIGN
FT = 512
STAGE = FT + RT
CAP = 17408
CAP_BLOCKS = CAP // FT
DN_T = (((0,), (0,)), ((), ()))
NO_ROW = -1e9


def _route_kernel(h_ref, mod_ref, g_ref, r_ref, xs_hbm, gs_hbm, rank_ref, seg_ref, cnt_ref,
                  x_stage, g_stage, cnt_sc, sems, *, n_rt):
    i = pl.program_id(0)

    @pl.when(i == 0)
    def _():
        x_stage[...] = jnp.zeros_like(x_stage)
        g_stage[...] = jnp.zeros_like(g_stage)
        for e in range(N_EXPERTS):
            cnt_sc[e] = 0

    x = _norm_mod(h_ref[...], g_ref[...], mod_ref[3:4, :], mod_ref[4:5, :])
    xh, xl = _split_bf16(x)
    rh, rl = _split_bf16(r_ref[...])
    logits = (jnp.dot(xh, rh, preferred_element_type=f32) + jnp.dot(xl, rh, preferred_element_type=f32)
              + jnp.dot(xh, rl, preferred_element_type=f32))
    lane = lax.broadcasted_iota(jnp.int32, (RT, 128), 1)
    lg = jnp.where(lane < N_EXPERTS, logits, -jnp.inf)
    m1 = jnp.max(lg, axis=1, keepdims=True)
    i1 = jnp.min(jnp.where(lg == m1, lane, 128), axis=1, keepdims=True)
    lg2 = jnp.where(lane == i1, -jnp.inf, lg)
    m2 = jnp.max(lg2, axis=1, keepdims=True)
    i2 = jnp.min(jnp.where(lg2 == m2, lane, 128), axis=1, keepdims=True)
    e2 = jnp.exp(m2 - m1)
    den = 1.0 + e2
    gates = jnp.where(lane == i1, 1.0 / den, 0.0) + jnp.where(lane == i2, e2 / den, 0.0)
    used = jnp.where((lane == i1) | (lane == i2), 1.0, 0.0).astype(bf16)

    tp = lax.broadcasted_iota(jnp.int32, (RT, RT), 0)
    tt = lax.broadcasted_iota(jnp.int32, (RT, RT), 1)
    earlier = jnp.where(tp < tt, 1.0, 0.0).astype(bf16)
    eye = jnp.where(tp == tt, 1.0, 0.0).astype(bf16)
    rank_t = lax.dot_general(used, earlier, DN_T, preferred_element_type=f32)
    used_t = lax.dot_general(used, eye, DN_T, preferred_element_type=f32)
    rank_t = jnp.where(used_t > 0, rank_t, NO_ROW)
    rank_ref[...] = rank_t[0:N_EXPERTS]

    g_hi, g_lo = _split_bf16(gates)
    xg = jnp.concatenate([xh, g_hi, g_lo], axis=1)
    win_row = lax.broadcasted_iota(jnp.int32, (WIN, RT), 0).astype(f32)

    def flush_copies(e, done_rows):
        dst = pl.multiple_of(e * CAP + done_rows, FT)
        return (pltpu.make_async_copy(x_stage.at[e, pl.ds(0, FT), :], xs_hbm.at[pl.ds(dst, FT), :], sems.at[e, 0]),
                pltpu.make_async_copy(g_stage.at[e, pl.ds(0, FT), :], gs_hbm.at[pl.ds(dst, FT), :], sems.at[e, 1]))

    full = []
    for e in range(N_EXPERTS):
        cnt = cnt_sc[e]
        seg_ref[i * N_EXPERTS + e] = cnt
        fill = cnt % FT
        start = pl.multiple_of((fill // ALIGN) * ALIGN, ALIGN)
        phase = (fill - start).astype(f32)
        n_e = jnp.sum(used_t[e:e + 1, :]).astype(jnp.int32)
        p = jnp.where(win_row == rank_t[e:e + 1, :] + phase, 1.0, 0.0).astype(bf16)
        rows = jnp.dot(p, xg, preferred_element_type=f32)
        xrow = rows[:, :D]
        grow = rows[:, D:D + 128] + rows[:, D + 128:]
        head = pl.ds(start, ALIGN)
        tail = pl.ds(pl.multiple_of(start + ALIGN, ALIGN), RT)
        x_stage[e, head, :] = (x_stage[e, head, :].astype(f32) + xrow[:ALIGN]).astype(bf16)
        x_stage[e, tail, :] = xrow[ALIGN:].astype(bf16)
        g_stage[e, head, :] = g_stage[e, head, :] + grow[:ALIGN]
        g_stage[e, tail, :] = grow[ALIGN:]
        cnt_sc[e] = cnt + n_e
        full.append((fill + n_e >= FT, cnt - fill))

        @pl.when(full[e][0])
        def _():
            for copy in flush_copies(e, full[e][1]):
                copy.start()

    for e in range(N_EXPERTS):
        @pl.when(full[e][0])
        def _():
            for copy in flush_copies(e, full[e][1]):
                copy.wait()
            x_stage[e, 0:RT, :] = x_stage[e, FT:STAGE, :]
            x_stage[e, RT:STAGE, :] = jnp.zeros((STAGE - RT, D), bf16)
            g_stage[e, 0:RT, :] = g_stage[e, FT:STAGE, :]
            g_stage[e, RT:STAGE, :] = jnp.zeros((STAGE - RT, 128), f32)

    @pl.when(i == n_rt - 1)
    def _():
        for e in range(N_EXPERTS):
            c = cnt_sc[e]
            cnt_ref[e] = c
            for copy in flush_copies(e, c - c % FT):
                copy.start()
                copy.wait()


def _moe_route(h, mod, g, router, n_rt):
    return pl.pallas_call(
        functools.partial(_route_kernel, n_rt=n_rt),
        grid=(n_rt,),
        in_specs=[
            pl.BlockSpec((RT, D), lambda i: (i, 0)),
            pl.BlockSpec((None, 8, D), lambda i: (jnp.minimum(i // RT_PER_SAMPLE, CTX_MOD_ROW), 0, 0)),
            _const_spec((1, D)), _const_spec((D, 128)),
        ],
        out_specs=[
            pl.BlockSpec(memory_space=pl.ANY), pl.BlockSpec(memory_space=pl.ANY),
            pl.BlockSpec((None, N_EXPERTS, RT), lambda i: (i, 0, 0)),
            pl.BlockSpec(memory_space=pltpu.SMEM), pl.BlockSpec(memory_space=pltpu.SMEM),
        ],
        out_shape=[
            jax.ShapeDtypeStruct((N_EXPERTS * CAP, D), bf16),
            jax.ShapeDtypeStruct((N_EXPERTS * CAP, 128), f32),
            jax.ShapeDtypeStruct((n_rt, N_EXPERTS, RT), f32),
            jax.ShapeDtypeStruct((n_rt * N_EXPERTS,), jnp.int32),
            jax.ShapeDtypeStruct((N_EXPERTS,), jnp.int32),
        ],
        scratch_shapes=[
            pltpu.VMEM((N_EXPERTS, STAGE, D), bf16), pltpu.VMEM((N_EXPERTS, STAGE, 128), f32),
            pltpu.SMEM((N_EXPERTS,), jnp.int32), pltpu.SemaphoreType.DMA((N_EXPERTS, 2)),
        ],
        compiler_params=_params(("arbitrary",)),
        name="moe_route",
    )(h, mod, g, router)


def _moe_ffn_kernel(te_ref, tb_ref, nt_ref, x_ref, gs_ref, wg_ref, wu_ref, wd_ref, y_ref, acc_sc):
    j = pl.program_id(0)
    f = pl.program_id(1)

    @pl.when(j < nt_ref[0])
    def _():
        @pl.when(f == 0)
        def _():
            acc_sc[...] = jnp.zeros_like(acc_sc)

        x = x_ref[...]
        gate = jnp.dot(x, wg_ref[...], preferred_element_type=f32)
        up = jnp.dot(x, wu_ref[...], preferred_element_type=f32)
        hid = (_silu(gate) * up).astype(bf16)
        acc_sc[...] += jnp.dot(hid, wd_ref[...], preferred_element_type=f32)

        @pl.when(f == NF - 1)
        def _():
            lane = lax.broadcasted_iota(jnp.int32, (FT, 128), 1)
            ge = jnp.sum(jnp.where(lane == te_ref[j], gs_ref[...], 0.0), axis=1, keepdims=True)
            y_ref[...] = (ge * acc_sc[...]).astype(bf16)


def _moe_experts(xs, gs, wg, wu, wd, tile_expert, tile_block, n_tiles):
    max_tiles = tile_expert.shape[0]

    def f_eff(j, f, nt):
        return jnp.where(j < nt[0], f, NF - 1)

    grid_spec = pltpu.PrefetchScalarGridSpec(
        num_scalar_prefetch=3,
        grid=(max_tiles, NF),
        in_specs=[
            pl.BlockSpec((FT, D), lambda j, f, te, tb, nt: (tb[j], 0)),
            pl.BlockSpec((FT, 128), lambda j, f, te, tb, nt: (tb[j], 0)),
            pl.BlockSpec((None, D, TF), lambda j, f, te, tb, nt: (te[j], 0, f_eff(j, f, nt))),
            pl.BlockSpec((None, D, TF), lambda j, f, te, tb, nt: (te[j], 0, f_eff(j, f, nt))),
            pl.BlockSpec((None, TF, D), lambda j, f, te, tb, nt: (te[j], f_eff(j, f, nt), 0)),
        ],
        out_specs=pl.BlockSpec((FT, D), lambda j, f, te, tb, nt: (tb[j], 0)),
        scratch_shapes=[pltpu.VMEM((FT, D), f32)],
    )
    return pl.pallas_call(
        _moe_ffn_kernel,
        grid_spec=grid_spec,
        out_shape=jax.ShapeDtypeStruct((N_EXPERTS * CAP, D), bf16),
        compiler_params=_params(("arbitrary", "arbitrary")),
        name="moe_experts",
    )(tile_expert, tile_block, n_tiles, xs, gs, wg, wu, wd)


def _moe_combine_kernel(seg_ref, nseg_ref, wmax_ref, rank_ref, h_ref, mod_ref, ys_hbm, o_ref, ybuf, acc_sc, sems,
                        *, n_rt):
    i = pl.program_id(0)
    win_row = lax.broadcasted_iota(jnp.int32, (WIN, RT), 0).astype(f32)

    def window(step, e):
        seg = seg_ref[step * N_EXPERTS + e]
        start = pl.multiple_of(jnp.minimum((seg // ALIGN) * ALIGN, wmax_ref[e]), ALIGN)
        copy = pltpu.make_async_copy(ys_hbm.at[pl.ds(e * CAP + start, WIN), :], ybuf.at[step % 2, e],
                                     sems.at[step % 2, e])
        return seg, start, copy

    def fetch(step):
        for e in range(N_EXPERTS):
            @pl.when(nseg_ref[step * N_EXPERTS + e] > 0)
            def _():
                window(step, e)[2].start()

    @pl.when(i == 0)
    def _():
        fetch(0)

    @pl.when(i + 1 < n_rt)
    def _():
        fetch(i + 1)

    acc_sc[...] = jnp.zeros_like(acc_sc)
    for e in range(N_EXPERTS):
        @pl.when(nseg_ref[i * N_EXPERTS + e] > 0)
        def _():
            seg, start, copy = window(i, e)
            copy.wait()
            r = rank_ref[e:e + 1, :]
            p = jnp.where(win_row == r + (seg - start).astype(f32), 1.0, 0.0).astype(bf16)
            acc_sc[...] += lax.dot_general(p, ybuf[i % 2, e], DN_T, preferred_element_type=f32)

    o_ref[...] = h_ref[...] + mod_ref[5:6, :] * acc_sc[...]


def _moe_combine(ys, rank, seg, nseg, wmax, h, mod, n_rt):
    grid_spec = pltpu.PrefetchScalarGridSpec(
        num_scalar_prefetch=3,
        grid=(n_rt,),
        in_specs=[
            pl.BlockSpec((None, N_EXPERTS, RT), lambda i, *_: (i, 0, 0)),
            pl.BlockSpec((RT, D), lambda i, *_: (i, 0)),
            pl.BlockSpec((None, 8, D), lambda i, *_: (jnp.minimum(i // RT_PER_SAMPLE, CTX_MOD_ROW), 0, 0)),
            pl.BlockSpec(memory_space=pl.ANY),
        ],
        out_specs=pl.BlockSpec((RT, D), lambda i, *_: (i, 0)),
        scratch_shapes=[pltpu.VMEM((2, N_EXPERTS, WIN, D), bf16), pltpu.VMEM((RT, D), f32),
                        pltpu.SemaphoreType.DMA((2, N_EXPERTS))],
    )
    return pl.pallas_call(
        functools.partial(_moe_combine_kernel, n_rt=n_rt),
        grid_spec=grid_spec,
        out_shape=jax.ShapeDtypeStruct((n_rt * RT, D), f32),
        compiler_params=_params(("arbitrary",)),
        name="moe_combine",
    )(seg, nseg, wmax, rank, h, mod, ys)


def _moe(h, mod, g, router, wg, wu, wd, nt):
    n_rt = nt * (TM // RT)
    xs, gs, rank, seg, counts = _moe_route(h, mod, g, router, n_rt)
    tiles = (counts + FT - 1) // FT
    ends = jnp.cumsum(tiles)
    n_tiles = ends[-1]
    max_tiles = (2 * n_rt * RT) // FT + N_EXPERTS
    j = jnp.minimum(jnp.arange(max_tiles, dtype=jnp.int32), n_tiles - 1)
    tile_expert = jnp.sum((j[:, None] >= ends[None, :]).astype(jnp.int32), axis=1)
    tile_block = tile_expert * CAP_BLOCKS + j - (ends - tiles)[tile_expert]
    ys = _moe_experts(xs, gs, wg, wu, wd, tile_expert, tile_block, n_tiles.reshape(1))
    seg2 = seg.reshape(n_rt, N_EXPERTS)
    nseg = (jnp.concatenate([seg2[1:], counts[None, :]], axis=0) - seg2).reshape(-1)
    wmax = jnp.maximum(tiles * FT - WIN, 0)
    return _moe_combine(ys, rank, seg, nseg, wmax, h, mod, n_rt)


def _ssd_in_kernel(h_ref, mod_ref, g_ref, wz_ref, wx_ref, wdt_ref, z_ref, x_ref, dt_ref):
    a = _norm_mod(h_ref[...], g_ref[...], mod_ref[0:1, :], mod_ref[1:2, :]).astype(bf16)
    for j in range(SSD_INNER // D):
        z_ref[:, j * D:(j + 1) * D] = jnp.dot(a, wz_ref[:, j * D:(j + 1) * D], preferred_element_type=f32).astype(bf16)
    for j in range(SSD_CONV_CH // D):
        x_ref[:, j * D:(j + 1) * D] = jnp.dot(a, wx_ref[:, j * D:(j + 1) * D], preferred_element_type=f32).astype(bf16)
    dt_ref[...] = jnp.dot(a, wdt_ref[...], preferred_element_type=f32)


def _ssd_in_proj(h, mod, g, wz, wx, wdt):
    return pl.pallas_call(
        _ssd_in_kernel,
        grid=(NT,),
        in_specs=[_tile_spec(D), _mod_spec(), _const_spec((1, D)), _const_spec((D, SSD_INNER)),
                  _const_spec((D, SSD_CONV_CH)), _const_spec((D, 256))],
        out_specs=[_tile_spec(SSD_INNER), _tile_spec(SSD_CONV_CH), _tile_spec(256)],
        out_shape=[jax.ShapeDtypeStruct((T, SSD_INNER), bf16), jax.ShapeDtypeStruct((T, SSD_CONV_CH), bf16),
                   jax.ShapeDtypeStruct((T, 256), f32)],
        compiler_params=_params(("parallel",)),
        name="ssd_in_proj",
    )(h, mod, g, wz, wx, wdt)


CONV_TM = 256
CONV_HALO = 16
CONV_TILES_PER_SAMPLE = SEQ // CONV_TM


def _conv_kernel(x_ref, prev_ref, next_ref, w_ref, b_ref, o_ref):
    i = pl.program_id(0)
    is_ctx = i >= N_LAT // CONV_TM
    seg_start = is_ctx | (i % CONV_TILES_PER_SAMPLE == 0)
    seg_end = is_ctx | (i % CONV_TILES_PER_SAMPLE == CONV_TILES_PER_SAMPLE - 1)
    x = x_ref[...].astype(f32)
    prev_row = jnp.where(seg_start, 0.0, prev_ref[CONV_HALO - 1:CONV_HALO, :].astype(f32))
    next_row = jnp.where(seg_end, 0.0, next_ref[0:1, :].astype(f32))
    row = lax.broadcasted_iota(jnp.int32, x.shape, 0)
    xm1 = jnp.where(row == 0, prev_row, pltpu.roll(x, 1, 0))
    xp1 = jnp.where(row == CONV_TM - 1, next_row, pltpu.roll(x, CONV_TM - 1, 0))
    y = w_ref[0:1, :] * xm1 + w_ref[1:2, :] * x + w_ref[2:3, :] * xp1 + b_ref[...]
    o_ref[...] = _silu(y).astype(bf16)


def _ssd_conv(xbc, conv_w, conv_b):
    n = T // CONV_TM
    per = CONV_TM // CONV_HALO
    last = T // CONV_HALO - 1
    return pl.pallas_call(
        _conv_kernel,
        grid=(n,),
        in_specs=[
            pl.BlockSpec((CONV_TM, SSD_CONV_CH), lambda i: (i, 0)),
            pl.BlockSpec((CONV_HALO, SSD_CONV_CH), lambda i: (jnp.maximum(i * per - 1, 0), 0)),
            pl.BlockSpec((CONV_HALO, SSD_CONV_CH), lambda i: (jnp.minimum((i + 1) * per, last), 0)),
            _const_spec((8, SSD_CONV_CH)), _const_spec((1, SSD_CONV_CH)),
        ],
        out_specs=pl.BlockSpec((CONV_TM, SSD_CONV_CH), lambda i: (i, 0)),
        out_shape=jax.ShapeDtypeStruct((T, SSD_CONV_CH), bf16),
        compiler_params=_params(("parallel",)),
        name="ssd_conv",
    )(xbc, xbc, xbc, conv_w, conv_b)


def _ssd_scan_kernel(x_ref, b_ref, c_ref, dt_ref, bias_ref, alog_ref, y_ref, state_sc):
    d = pl.program_id(0)
    c = pl.program_id(2)

    @pl.when(c == 0)
    def _():
        state_sc[...] = jnp.zeros_like(state_sc)

    L = SSD_CHUNK
    li = lax.broadcasted_iota(jnp.int32, (L, L), 0)
    si = lax.broadcasted_iota(jnp.int32, (L, L), 1)
    causal = (si - li) * (1 - 2 * d) <= 0
    tri = jnp.where(causal, 1.0, 0.0).astype(bf16)
    hi = lax.broadcasted_iota(jnp.int32, (128, SSD_INNER), 0)
    ci = lax.broadcasted_iota(jnp.int32, (128, SSD_INNER), 1)
    expand = jnp.where(ci // SSD_P == hi, 1.0, 0.0).astype(bf16)

    dt = jax.nn.softplus(dt_ref[...] + bias_ref[...])
    a_neg = -jnp.exp(alog_ref[...])
    da = dt * a_neg
    da_parts = _split3(da)
    cs = (jnp.dot(tri, da_parts[0], preferred_element_type=f32) + jnp.dot(tri, da_parts[1], preferred_element_type=f32)
          + jnp.dot(tri, da_parts[2], preferred_element_type=f32))
    dn_t = (((0,), (1,)), ((), ()))
    cs_t = (lax.dot_general(da_parts[0], tri, dn_t, preferred_element_type=f32)
            + lax.dot_general(da_parts[1], tri, dn_t, preferred_element_type=f32)
            + lax.dot_general(da_parts[2], tri, dn_t, preferred_element_type=f32))
    total = jnp.sum(da, axis=0, keepdims=True)

    e_out = jnp.exp(cs)
    e_in = jnp.exp(total - cs) * dt
    dt_x = jnp.dot(dt.astype(bf16), expand, preferred_element_type=f32).astype(bf16)
    e_in_x = jnp.dot(e_in.astype(bf16), expand, preferred_element_type=f32).astype(bf16)
    out_scale = _dot3(_split_bf16(e_out), expand)
    chunk_decay = _dot3(_split_bf16(jnp.broadcast_to(jnp.exp(total), (8, 128))), expand)[0:1]

    x = x_ref[...]
    xdt = x * dt_x
    xw = x * e_in_x

    for g in range(SSD_GROUPS):
        bg = b_ref[:, g * SSD_N:(g + 1) * SSD_N]
        cg = c_ref[:, g * SSD_N:(g + 1) * SSD_N]
        cb = lax.dot_general(cg, bg, (((1,), (1,)), ((), ())), preferred_element_type=f32)
        gsl = slice(g * SSD_HPG * SSD_P, (g + 1) * SSD_HPG * SSD_P)
        st = state_sc[g]
        y_off = jnp.dot(cg, st.astype(bf16), preferred_element_type=f32) * out_scale[:, gsl]
        ys = []
        for r in range(SSD_HPG):
            hd = g * SSD_HPG + r
            seg = cs[:, hd:hd + 1] - cs_t[hd:hd + 1, :]
            decay = jnp.exp(jnp.where(causal, seg, -jnp.inf))
            m = (cb * decay).astype(bf16)
            ys.append(jnp.dot(m, xdt[:, hd * SSD_P:(hd + 1) * SSD_P], preferred_element_type=f32))
        y_ref[:, gsl] = (jnp.concatenate(ys, axis=1) + y_off).astype(bf16)
        new = lax.dot_general(bg, xw[:, gsl], (((0,), (0,)), ((), ())), preferred_element_type=f32)
        state_sc[g] = st * chunk_decay[:, gsl] + new


SSD_NCHUNK = (SEQ + CTX) // SSD_CHUNK
SSD_CTX_CHUNKS = CTX // SSD_CHUNK
SSD_LAT_CHUNKS = SEQ // SSD_CHUNK


def _ssd_chunk_block(d, b, c):
    ctx_j = jnp.where(d == 0, c, SSD_CTX_CHUNKS - 1 - c)
    lat_j = jnp.where(d == 0, c - SSD_CTX_CHUNKS, SSD_LAT_CHUNKS - 1 - (c - SSD_CTX_CHUNKS))
    return jnp.where(c < SSD_CTX_CHUNKS, N_LAT // SSD_CHUNK + b * SSD_CTX_CHUNKS + ctx_j, b * SSD_LAT_CHUNKS + lat_j)


def _ssd_scan(xbc, dt, dt_bias, a_log):
    x_blk = SSD_INNER // SSD_BC
    return pl.pallas_call(
        _ssd_scan_kernel,
        grid=(2, BATCH, SSD_NCHUNK),
        in_specs=[
            pl.BlockSpec((SSD_CHUNK, SSD_INNER), lambda d, b, c: (_ssd_chunk_block(d, b, c), 0)),
            pl.BlockSpec((SSD_CHUNK, SSD_BC), lambda d, b, c: (_ssd_chunk_block(d, b, c), x_blk)),
            pl.BlockSpec((SSD_CHUNK, SSD_BC), lambda d, b, c: (_ssd_chunk_block(d, b, c), x_blk + 1)),
            pl.BlockSpec((SSD_CHUNK, 128), lambda d, b, c: (_ssd_chunk_block(d, b, c), d)),
            pl.BlockSpec((None, 1, 128), lambda d, b, c: (d, 0, 0)),
            pl.BlockSpec((None, 1, 128), lambda d, b, c: (d, 0, 0)),
        ],
        out_specs=pl.BlockSpec((None, SSD_CHUNK, SSD_INNER), lambda d, b, c: (d, _ssd_chunk_block(d, b, c), 0)),
        out_shape=jax.ShapeDtypeStruct((2, T, SSD_INNER), bf16),
        scratch_shapes=[pltpu.VMEM((SSD_GROUPS, SSD_N, SSD_HPG * SSD_P), f32)],
        compiler_params=_params(("parallel", "parallel", "arbitrary")),
        name="ssd_scan",
    )(xbc, xbc, xbc, dt, dt_bias, a_log)


def _ssd_out_kernel(yf_ref, yb_ref, x_ref, z_ref, dskip_ref, ng_ref, w_ref, h_ref, mod_ref, o_ref):
    y = yf_ref[...].astype(f32) + yb_ref[...].astype(f32) + x_ref[...].astype(f32) * dskip_ref[...]
    gated = y * _silu(z_ref[...].astype(f32))
    gw = SSD_INNER // SSD_GROUPS
    parts = []
    for g in range(SSD_GROUPS):
        s = gated[:, g * gw:(g + 1) * gw]
        parts.append(s * lax.rsqrt(jnp.mean(s * s, axis=-1, keepdims=True) + EPS))
    normed = (jnp.concatenate(parts, axis=1) * ng_ref[...]).astype(bf16)
    out = jnp.dot(normed, w_ref[...], preferred_element_type=f32)
    o_ref[...] = h_ref[...] + mod_ref[2:3, :] * out


def _ssd_out(y2, xbc, z, dskip, ng, w_out, h, mod, nt):
    return pl.pallas_call(
        _ssd_out_kernel,
        grid=(nt,),
        in_specs=[
            pl.BlockSpec((None, TM, SSD_INNER), lambda t: (0, t, 0)),
            pl.BlockSpec((None, TM, SSD_INNER), lambda t: (1, t, 0)),
            pl.BlockSpec((TM, SSD_INNER), lambda t: (t, 0)),
            _tile_spec(SSD_INNER), _const_spec((1, SSD_INNER)), _const_spec((1, SSD_INNER)),
            _const_spec((SSD_INNER, D)), _tile_spec(D), _mod_spec(),
        ],
        out_specs=_tile_spec(D),
        out_shape=jax.ShapeDtypeStruct((nt * TM, D), f32),
        compiler_params=_params(("parallel",)),
        name="ssd_out",
    )(y2, y2, xbc, z, dskip, ng, w_out, h, mod)


def _gelu(x):
    return 0.5 * x * (1.0 + lax.erf(x * math.sqrt(0.5)))


def _cmlp_kernel(h_ref, mod_ref, g_ref, wu_ref, wv_ref, bu_ref, bv_ref, vg_ref, ws_ref, bs_ref, wo_ref, o_ref, uv_sc):
    a = _norm_mod(h_ref[...], g_ref[...], mod_ref[0:1, :], mod_ref[1:2, :]).astype(bf16)
    v = _gelu(jnp.dot(a, wv_ref[...], preferred_element_type=f32) + bv_ref[...])
    v = (v * lax.rsqrt(jnp.mean(v * v, axis=-1, keepdims=True) + EPS) * vg_ref[...]).astype(bf16)
    u = _gelu(jnp.dot(a, wu_ref[...], preferred_element_type=f32) + bu_ref[...])
    for ck in range(TM // CMLP_CHUNK):
        rows = slice(ck * CMLP_CHUNK, (ck + 1) * CMLP_CHUNK)
        for g in range(CMLP_GROUPS):
            cols = slice(g * CMLP_GW, (g + 1) * CMLP_GW)
            mixed = jnp.dot(ws_ref[g], v[rows, cols], preferred_element_type=f32) + bs_ref[:, cols]
            uv_sc[rows, cols] = (u[rows, cols] * mixed).astype(bf16)
    out = jnp.dot(uv_sc[...], wo_ref[...], preferred_element_type=f32)
    o_ref[...] = h_ref[...] + mod_ref[2:3, :] * out


def _cmlp(h, mod, g, wu, wv, bu, bv, vg, ws, bs, wo, nt):
    return pl.pallas_call(
        _cmlp_kernel,
        grid=(nt,),
        in_specs=[
            _tile_spec(D), _mod_spec(), _const_spec((1, D)), _const_spec((D, CMLP_D)), _const_spec((D, CMLP_D)),
            _const_spec((1, CMLP_D)), _const_spec((1, CMLP_D)), _const_spec((1, CMLP_D)),
            _const_spec((CMLP_GROUPS, CMLP_CHUNK, CMLP_CHUNK)), _const_spec((CMLP_CHUNK, CMLP_D)),
            _const_spec((CMLP_D, D)),
        ],
        out_specs=_tile_spec(D),
        out_shape=jax.ShapeDtypeStruct((nt * TM, D), f32),
        scratch_shapes=[pltpu.VMEM((TM, CMLP_D), bf16)],
        compiler_params=_params(("parallel",)),
        name="cmlp",
    )(h, mod, g, wu, wv, bu, bv, vg, ws, bs, wo)


def _final_kernel(h_ref, g_ref, o_ref):
    x = h_ref[...]
    o_ref[...] = x * lax.rsqrt(jnp.mean(x * x, axis=-1, keepdims=True) + EPS) * g_ref[...]


def _final_norm(h, g):
    return pl.pallas_call(
        _final_kernel,
        grid=(NT_LAT,),
        in_specs=[_tile_spec(D), _const_spec((1, D))],
        out_specs=_tile_spec(D),
        out_shape=jax.ShapeDtypeStruct((N_LAT, D), f32),
        compiler_params=_params(("parallel",)),
        name="final_norm",
    )(h, g)


def _rope_tables():
    pos = jnp.arange(SEQ)
    inv_freq = 1.0 / (ROPE_THETA ** (jnp.arange(0, ROPE_AXIS_DIM, 2, dtype=f32) / ROPE_AXIS_DIM))
    ang_r = (pos // GRID_W).astype(f32)[:, None] * inv_freq
    ang_c = (pos % GRID_W).astype(f32)[:, None] * inv_freq
    cos = jnp.concatenate([jnp.cos(ang_r)] * 2 + [jnp.cos(ang_c)] * 2, axis=1)
    sin = jnp.concatenate([-jnp.sin(ang_r), jnp.sin(ang_r), -jnp.sin(ang_c), jnp.sin(ang_c)], axis=1)
    cos = jnp.concatenate([cos, jnp.ones((TM, HEAD_DIM), f32)], axis=0)
    sin = jnp.concatenate([sin, jnp.zeros((TM, HEAD_DIM), f32)], axis=0)
    return cos, sin


def _row(v):
    return v.reshape(1, -1)


def _pad_lanes(v, width=128):
    return jnp.pad(v, ((0, 0), (0, width - v.shape[1])))


def kernel(x, c, ctx, c_ctx, w_mod, b_mod, norm1_g, norm2_g, attn_w_qkv, attn_q_g, attn_k_g, attn_w_o, ssd_w_in, ssd_conv_w, ssd_conv_b, ssd_dt_bias_f, ssd_dt_bias_b, ssd_a_log_f, ssd_a_log_b, ssd_d_skip, ssd_norm_g, ssd_w_out, cmlp_w_in, cmlp_b_in, cmlp_v_g, cmlp_w_s, cmlp_b_s, cmlp_w_out, ffn_w_gate, ffn_w_up, ffn_w_down, moe_router, moe_w_gate, moe_w_up, moe_w_down, final_g):
    h = jnp.concatenate([x.reshape(N_LAT, D), ctx.reshape(N_CTX, D)], axis=0)
    cond = jnp.concatenate([c, c_ctx[None, :], jnp.zeros((8 - BATCH - 1, D), f32)], axis=0)
    mods = _modulation(cond, w_mod, b_mod)
    cos_t, sin_t = _rope_tables()

    for i in range(DEPTH):
        need_ctx = i < DEPTH - 1
        nt = NT if need_ctx else NT_LAT
        mod = mods[i]
        kind, j = i % 3, i // 3
        g1 = _row(norm1_g[i])
        if kind == 0:
            q, k, v = _qkv_proj(h, mod, g1, attn_w_qkv[j].astype(bf16), _row(attn_q_g[j]), _row(attn_k_g[j]), cos_t, sin_t)
            o = _attention(q, k, v, need_ctx)
            h = _proj_res(o, attn_w_o[j].astype(bf16), h, mod, nt)
        elif kind == 1:
            w_in = ssd_w_in[j]
            wz = w_in[:, :SSD_INNER].astype(bf16)
            wx = w_in[:, SSD_INNER:SSD_INNER + SSD_CONV_CH].astype(bf16)
            w_dt = w_in[:, SSD_INNER + SSD_CONV_CH:]
            wdt = jnp.concatenate([_pad_lanes(w_dt[:, :SSD_HEADS]), _pad_lanes(w_dt[:, SSD_HEADS:])], axis=1).astype(bf16)
            z, xbc, dt = _ssd_in_proj(h, mod, g1, wz, wx, wdt)
            conv_w = jnp.pad(ssd_conv_w[j], ((0, 8 - ssd_conv_w.shape[1]), (0, 0)))
            xbc = _ssd_conv(xbc, conv_w, _row(ssd_conv_b[j]))
            dt_bias = jnp.stack([_pad_lanes(_row(ssd_dt_bias_f[j])), _pad_lanes(_row(ssd_dt_bias_b[j]))])
            a_log = jnp.stack([_pad_lanes(_row(ssd_a_log_f[j])), _pad_lanes(_row(ssd_a_log_b[j]))])
            y2 = _ssd_scan(xbc, dt, dt_bias, a_log)
            dskip = _row(jnp.repeat(ssd_d_skip[j], SSD_P))
            h = _ssd_out(y2, xbc, z, dskip, _row(ssd_norm_g[j]), ssd_w_out[j].astype(bf16), h, mod, nt)
        else:
            w_in = cmlp_w_in[j]
            b_in = cmlp_b_in[j]
            bs = jnp.repeat(cmlp_b_s[j].T, CMLP_GW, axis=1)
            h = _cmlp(h, mod, g1, w_in[:, :CMLP_D].astype(bf16), w_in[:, CMLP_D:].astype(bf16),
                      _row(b_in[:CMLP_D]), _row(b_in[CMLP_D:]), _row(cmlp_v_g[j]), cmlp_w_s[j].astype(bf16), bs,
                      cmlp_w_out[j].astype(bf16), nt)
        kk = i // 2
        g2 = _row(norm2_g[i])
        if i % 2 == 0:
            h = _ffn(h, mod, g2, ffn_w_gate[kk].astype(bf16), ffn_w_up[kk].astype(bf16), ffn_w_down[kk].astype(bf16), nt)
        else:
            h = _moe(h, mod, g2, _pad_lanes(moe_router[kk]), moe_w_gate[kk].astype(bf16), moe_w_up[kk].astype(bf16),
                     moe_w_down[kk].astype(bf16), nt)
    return _final_norm(h, _row(final_g)).reshape(BATCH, SEQ, D)
```

```python
import functools
import math

import jax
import jax.numpy as jnp
from jax import lax
from jax.experimental import pallas as pl
from jax.experimental.pallas import tpu as pltpu

f32 = jnp.float32
bf16 = jnp.bfloat16

D = 1024
BATCH = 2
SEQ = 8192
CTX = 256
DEPTH = 4
GRID_W = 64
EPS = 1e-6
N_MOD = 6

HEAD_DIM = 128
N_HEADS = 8
N_KV = 2
Q_GROUP = 4
ROPE_AXIS_DIM = 64
ROPE_THETA = 10000.0

SSD_INNER = 2048
SSD_P = 64
SSD_HEADS = 32
SSD_GROUPS = 4
SSD_HPG = 8
SSD_N = 128
SSD_CHUNK = 128
SSD_BC = SSD_GROUPS * SSD_N
SSD_CONV_CH = SSD_INNER + 2 * SSD_BC

CMLP_D = 2048
CMLP_GROUPS = 8
CMLP_GW = 256
CMLP_CHUNK = 128

D_FF = 3584
N_EXPERTS = 8

N_LAT = BATCH * SEQ
N_CTX = BATCH * CTX
T = N_LAT + N_CTX
TM = 512
NT = T // TM
NT_LAT = N_LAT // TM
TILES_PER_SAMPLE = SEQ // TM
CTX_MOD_ROW = BATCH

TF = 1792
NF = D_FF // TF

TQ = 256
TK = 512
ATT_CHUNKS = SEQ // TK
VW = 2 * HEAD_DIM
LOG2E = math.log2(math.e)
NQ_LAT = SEQ // TQ

VMEM_LIMIT = 56 * 1024 * 1024


def _mod_row(t):
    return jnp.minimum(t // TILES_PER_SAMPLE, CTX_MOD_ROW)


def _tile_spec(width):
    return pl.BlockSpec((TM, width), lambda t: (t, 0))


def _mod_spec():
    return pl.BlockSpec((None, 8, D), lambda t: (_mod_row(t), 0, 0))


def _const_spec(shape):
    n = len(shape)
    return pl.BlockSpec(shape, lambda *_: (0,) * n)


def _params(semantics):
    return pltpu.CompilerParams(dimension_semantics=semantics, vmem_limit_bytes=VMEM_LIMIT)


def _silu(x):
    return x * jax.nn.sigmoid(x)


def _norm_mod(x, g, shift, scale):
    y = x * lax.rsqrt(jnp.mean(x * x, axis=-1, keepdims=True) + EPS) * g
    return y * (1.0 + scale) + shift


def _mod_kernel(c_ref, w_ref, b_ref, o_ref):
    s = _silu(c_ref[...])
    o_ref[...] = jnp.dot(s.astype(bf16), w_ref[...].astype(bf16), preferred_element_type=f32) + b_ref[...]


def _modulation(cond, w_mod, b_mod):
    nblk = N_MOD * D // D
    out = pl.pallas_call(
        _mod_kernel,
        grid=(DEPTH, nblk),
        in_specs=[
            pl.BlockSpec((8, D), lambda i, j: (0, 0)),
            pl.BlockSpec((None, D, D), lambda i, j: (i, 0, j)),
            pl.BlockSpec((None, 1, D), lambda i, j: (i, 0, j)),
        ],
        out_specs=pl.BlockSpec((None, 8, D), lambda i, j: (i, 0, j)),
        out_shape=jax.ShapeDtypeStruct((DEPTH, 8, N_MOD * D), f32),
        compiler_params=_params(("arbitrary", "arbitrary")),
        name="modulation",
    )(cond, w_mod, b_mod.reshape(DEPTH, 1, N_MOD * D))
    mod = out[:, :3].reshape(DEPTH, 3, N_MOD, D)
    return jnp.pad(mod, ((0, 0), (0, 0), (0, 8 - N_MOD), (0, 0)))


def _qkv_kernel(h_ref, mod_ref, g_ref, w_ref, qg_ref, kg_ref, cos_ref, sin_ref, q_ref, k_ref, v_ref):
    a = _norm_mod(h_ref[...], g_ref[...], mod_ref[0:1, :], mod_ref[1:2, :])
    qkv = jnp.dot(a.astype(bf16), w_ref[...], preferred_element_type=f32)
    cos = cos_ref[...]
    sin = sin_ref[...]
    lane = lax.broadcasted_iota(jnp.int32, (TM, HEAD_DIM), 1)
    first_half = (lane % ROPE_AXIS_DIM) < (ROPE_AXIS_DIM // 2)

    def head(x, g, scale):
        y = x * lax.rsqrt(jnp.mean(x * x, axis=-1, keepdims=True) + EPS) * g
        partner = jnp.where(first_half, pltpu.roll(y, HEAD_DIM - 32, 1), pltpu.roll(y, 32, 1))
        return (y * cos + partner * sin) * scale

    qg = qg_ref[...]
    kg = kg_ref[...]
    for hh in range(N_HEADS):
        sl = slice(hh * HEAD_DIM, (hh + 1) * HEAD_DIM)
        q_ref[:, sl] = head(qkv[:, sl], qg, HEAD_DIM ** -0.5 * LOG2E).astype(bf16)
    for hh in range(N_KV):
        sl = slice(hh * HEAD_DIM, (hh + 1) * HEAD_DIM)
        src = slice((N_HEADS + hh) * HEAD_DIM, (N_HEADS + hh + 1) * HEAD_DIM)
        k_ref[:, sl] = head(qkv[:, src], kg, 1.0).astype(bf16)
        v_ref[:, hh * VW:hh * VW + HEAD_DIM] = qkv[:, (N_HEADS + N_KV + hh) * HEAD_DIM:(N_HEADS + N_KV + hh + 1) * HEAD_DIM].astype(bf16)
        v_ref[:, hh * VW + HEAD_DIM:(hh + 1) * VW] = jnp.ones((TM, HEAD_DIM), bf16)


def _qkv_proj(h, mod, g, w_qkv, q_g, k_g, cos_t, sin_t):
    rope_spec = pl.BlockSpec((TM, HEAD_DIM), lambda t: (jnp.where(t < NT_LAT, t % TILES_PER_SAMPLE, TILES_PER_SAMPLE), 0))
    return pl.pallas_call(
        _qkv_kernel,
        grid=(NT,),
        in_specs=[
            _tile_spec(D), _mod_spec(), _const_spec((1, D)), _const_spec((D, (N_HEADS + 2 * N_KV) * HEAD_DIM)),
            _const_spec((1, HEAD_DIM)), _const_spec((1, HEAD_DIM)), rope_spec, rope_spec,
        ],
        out_specs=[_tile_spec(N_HEADS * HEAD_DIM), _tile_spec(N_KV * HEAD_DIM), _tile_spec(N_KV * VW)],
        out_shape=[
            jax.ShapeDtypeStruct((T, N_HEADS * HEAD_DIM), bf16),
            jax.ShapeDtypeStruct((T, N_KV * HEAD_DIM), bf16),
            jax.ShapeDtypeStruct((T, N_KV * VW), bf16),
        ],
        compiler_params=_params(("parallel",)),
        name="attn_qkv",
    )(h, mod, g, w_qkv, q_g, k_g, cos_t, sin_t)


def _attn_kernel(q_ref, kl_ref, vl_ref, kc_ref, vc_ref, o_ref, m_sc, acc_sc, s_a, s_b):
    qi = pl.program_id(2)
    q = q_ref[...]
    qs = jnp.concatenate([q[:, i * HEAD_DIM:(i + 1) * HEAD_DIM] for i in range(Q_GROUP)], axis=0)

    def scores(k):
        return lax.dot_general(qs, k, (((1,), (1,)), ((), ())), preferred_element_type=f32)

    def absorb(s, v):
        m_prev = m_sc[...]
        m_next = jnp.maximum(m_prev, jnp.max(s, axis=1, keepdims=True))
        alpha = jnp.exp2(m_prev - m_next)
        p = jnp.exp2(s - jnp.concatenate([m_next] * (s.shape[1] // HEAD_DIM), axis=1))
        acc_sc[...] = (jnp.concatenate([alpha, alpha], axis=1) * acc_sc[...]
                       + jnp.dot(p.astype(bf16), v, preferred_element_type=f32))
        m_sc[...] = m_next

    m_sc[...] = jnp.full_like(m_sc, -jnp.inf)
    acc_sc[...] = jnp.zeros_like(acc_sc)
    absorb(scores(kc_ref[...]), vc_ref[...])

    @pl.when(qi < NQ_LAT)
    def _():
        def chunk(ref, c):
            return ref[pl.ds(pl.multiple_of(c * TK, TK), TK), :]

        s_a[...] = scores(chunk(kl_ref, 0))

        def body(c2, carry):
            c = 2 * c2
            s_b[...] = scores(chunk(kl_ref, c + 1))
            absorb(s_a[...], chunk(vl_ref, c))
            s_a[...] = scores(chunk(kl_ref, jnp.minimum(c + 2, ATT_CHUNKS - 1)))
            absorb(s_b[...], chunk(vl_ref, c + 1))
            return carry
        lax.fori_loop(0, ATT_CHUNKS // 2, body, 0, unroll=2)

    acc = acc_sc[...]
    o = acc[:, :HEAD_DIM] / acc[:, HEAD_DIM:]
    for i in range(Q_GROUP):
        o_ref[:, i * HEAD_DIM:(i + 1) * HEAD_DIM] = o[i * TQ:(i + 1) * TQ].astype(bf16)


def _attention(q, k, v, need_ctx):
    nq = NQ_LAT + (1 if need_ctx else 0)
    ctx_blk = N_LAT // CTX

    def q_map(b, kh, qi):
        return (jnp.where(qi < NQ_LAT, b * NQ_LAT + qi, N_LAT // TQ + b), kh)

    def lat_spec(width):
        return pl.BlockSpec((SEQ, width), lambda b, kh, qi: (b, kh))

    def ctx_spec(width):
        return pl.BlockSpec((CTX, width), lambda b, kh, qi: (ctx_blk + b, kh))

    return pl.pallas_call(
        _attn_kernel,
        grid=(BATCH, N_KV, nq),
        in_specs=[pl.BlockSpec((TQ, Q_GROUP * HEAD_DIM), q_map), lat_spec(HEAD_DIM), lat_spec(VW),
                  ctx_spec(HEAD_DIM), ctx_spec(VW)],
        out_specs=pl.BlockSpec((TQ, Q_GROUP * HEAD_DIM), q_map),
        out_shape=jax.ShapeDtypeStruct((T if need_ctx else N_LAT, N_HEADS * HEAD_DIM), bf16),
        scratch_shapes=[pltpu.VMEM((Q_GROUP * TQ, HEAD_DIM), f32), pltpu.VMEM((Q_GROUP * TQ, VW), f32),
                        pltpu.VMEM((Q_GROUP * TQ, TK), f32), pltpu.VMEM((Q_GROUP * TQ, TK), f32)],
        compiler_params=_params(("parallel", "parallel", "arbitrary")),
        name="attn_core",
    )(q, k, v, k, v)


def _proj_res_kernel(y_ref, w_ref, h_ref, mod_ref, o_ref):
    y = jnp.dot(y_ref[...], w_ref[...], preferred_element_type=f32)
    o_ref[...] = h_ref[...] + mod_ref[2:3, :] * y


def _proj_res(y, w, h, mod, nt):
    kdim = y.shape[1]
    return pl.pallas_call(
        _proj_res_kernel,
        grid=(nt,),
        in_specs=[_tile_spec(kdim), _const_spec((kdim, D)), _tile_spec(D), _mod_spec()],
        out_specs=_tile_spec(D),
        out_shape=jax.ShapeDtypeStruct((nt * TM, D), f32),
        compiler_params=_params(("parallel",)),
        name="proj_res",
    )(y, w, h, mod)


def _ffn_kernel(h_ref, mod_ref, g_ref, wg_ref, wu_ref, wd_ref, o_ref, xn_sc, acc_sc):
    f = pl.program_id(1)

    @pl.when(f == 0)
    def _():
        xn_sc[...] = _norm_mod(h_ref[...], g_ref[...], mod_ref[3:4, :], mod_ref[4:5, :]).astype(bf16)
        acc_sc[...] = jnp.zeros_like(acc_sc)

    x = xn_sc[...]
    gate = jnp.dot(x, wg_ref[...], preferred_element_type=f32)
    up = jnp.dot(x, wu_ref[...], preferred_element_type=f32)
    hid = (_silu(gate) * up).astype(bf16)
    acc_sc[...] += jnp.dot(hid, wd_ref[...], preferred_element_type=f32)

    @pl.when(f == NF - 1)
    def _():
        o_ref[...] = h_ref[...] + mod_ref[5:6, :] * acc_sc[...]


def _ffn(h, mod, g, wg, wu, wd, layer, nt):
    return pl.pallas_call(
        _ffn_kernel,
        grid=(nt, NF),
        in_specs=[
            pl.BlockSpec((TM, D), lambda t, f: (t, 0)),
            pl.BlockSpec((None, 8, D), lambda t, f: (_mod_row(t), 0, 0)),
            pl.BlockSpec((1, D), lambda t, f: (0, 0)),
            pl.BlockSpec((None, D, TF), lambda t, f: (layer, 0, f)),
            pl.BlockSpec((None, D, TF), lambda t, f: (layer, 0, f)),
            pl.BlockSpec((None, TF, D), lambda t, f: (layer, f, 0)),
        ],
        out_specs=pl.BlockSpec((TM, D), lambda t, f: (t, 0)),
        out_shape=jax.ShapeDtypeStruct((nt * TM, D), f32),
        scratch_shapes=[pltpu.VMEM((TM, D), bf16), pltpu.VMEM((TM, D), f32)],
        compiler_params=_params(("parallel", "arbitrary")),
        name="ffn_dense",
    )(h, mod, g, wg, wu, wd)


def _split_bf16(x):
    hi = x.astype(bf16)
    lo = (x - hi.astype(f32)).astype(bf16)
    return hi, lo


def _split3(x):
    p0 = x.astype(bf16)
    r = x - p0.astype(f32)
    p1 = r.astype(bf16)
    p2 = (r - p1.astype(f32)).astype(bf16)
    return p0, p1, p2


def _dot3(a_parts, b):
    out = jnp.dot(a_parts[0], b, preferred_element_type=f32)
    for a in a_parts[1:]:
        out = out + jnp.dot(a, b, preferred_element_type=f32)
    return out


RT = 256
RT_PER_SAMPLE = SEQ // RT
ALIGN = 16
WIN = RT + ALIGN
FT = 512
HEAD_ROWS = 128
SLAB = 64
STAGE = FT + -(-WIN // SLAB) * SLAB
CAP = 17408
CAP_BLOCKS = CAP // FT
DN_T = (((0,), (0,)), ((), ()))
NO_ROW = -1e9


def _route_kernel(h_ref, mod_ref, g_ref, r_ref, xs_hbm, gs_hbm, rank_ref, seg_ref, cnt_ref,
                  x_stage, g_stage, cnt_sc, sems, *, n_rt):
    i = pl.program_id(0)

    @pl.when(i == 0)
    def _():
        x_stage[...] = jnp.zeros_like(x_stage)
        g_stage[...] = jnp.zeros_like(g_stage)
        for e in range(N_EXPERTS):
            cnt_sc[e] = 0

    x = _norm_mod(h_ref[...], g_ref[...], mod_ref[3:4, :], mod_ref[4:5, :])
    xh, xl = _split_bf16(x)
    rh, rl = _split_bf16(r_ref[...])
    logits = (jnp.dot(xh, rh, preferred_element_type=f32) + jnp.dot(xl, rh, preferred_element_type=f32)
              + jnp.dot(xh, rl, preferred_element_type=f32))
    lane = lax.broadcasted_iota(jnp.int32, (RT, 128), 1)
    lg = jnp.where(lane < N_EXPERTS, logits, -jnp.inf)
    m1 = jnp.max(lg, axis=1, keepdims=True)
    i1 = jnp.min(jnp.where(lg == m1, lane, 128), axis=1, keepdims=True)
    lg2 = jnp.where(lane == i1, -jnp.inf, lg)
    m2 = jnp.max(lg2, axis=1, keepdims=True)
    i2 = jnp.min(jnp.where(lg2 == m2, lane, 128), axis=1, keepdims=True)
    e2 = jnp.exp(m2 - m1)
    den = 1.0 + e2
    gates = jnp.where(lane == i1, 1.0 / den, 0.0) + jnp.where(lane == i2, e2 / den, 0.0)
    used = jnp.where((lane == i1) | (lane == i2), 1.0, 0.0).astype(bf16)

    tp = lax.broadcasted_iota(jnp.int32, (RT, RT), 0)
    tt = lax.broadcasted_iota(jnp.int32, (RT, RT), 1)
    earlier = jnp.where(tp < tt, 1.0, 0.0).astype(bf16)
    eye = jnp.where(tp == tt, 1.0, 0.0).astype(bf16)
    rank_t = lax.dot_general(used, earlier, DN_T, preferred_element_type=f32)
    used_t = lax.dot_general(used, eye, DN_T, preferred_element_type=f32)
    rank_t = jnp.where(used_t > 0, rank_t, NO_ROW)
    rank_ref[...] = rank_t[0:N_EXPERTS]

    g_hi, g_lo = _split_bf16(gates)
    xg = jnp.concatenate([xh, g_hi, g_lo], axis=1)
    head_row = lax.broadcasted_iota(jnp.int32, (HEAD_ROWS, RT), 0).astype(f32)
    slab_row = lax.broadcasted_iota(jnp.int32, (SLAB, RT), 0).astype(f32)

    def flush_copies(e, done_rows):
        dst = pl.multiple_of(e * CAP + done_rows, FT)
        return (pltpu.make_async_copy(x_stage.at[e, pl.ds(0, FT), :], xs_hbm.at[pl.ds(dst, FT), :], sems.at[e, 0]),
                pltpu.make_async_copy(g_stage.at[e, pl.ds(0, FT), :], gs_hbm.at[pl.ds(dst, FT), :], sems.at[e, 1]))

    full = []
    for e in range(N_EXPERTS):
        cnt = cnt_sc[e]
        seg_ref[i * N_EXPERTS + e] = cnt
        fill = cnt % FT
        start = pl.multiple_of((fill // ALIGN) * ALIGN, ALIGN)
        phase = fill - start
        n_e = jnp.sum(used_t[e:e + 1, :]).astype(jnp.int32)
        target = rank_t[e:e + 1, :] + phase.astype(f32)

        p = jnp.where(head_row == target, 1.0, 0.0).astype(bf16)
        rows = jnp.dot(p, xg, preferred_element_type=f32)
        grow = rows[:, D:D + 128] + rows[:, D + 128:]
        old = pl.ds(start, ALIGN)
        new = pl.ds(pl.multiple_of(start + ALIGN, ALIGN), HEAD_ROWS - ALIGN)
        x_stage[e, old, :] = (x_stage[e, old, :].astype(f32) + rows[:ALIGN, :D]).astype(bf16)
        x_stage[e, new, :] = rows[ALIGN:, :D].astype(bf16)
        g_stage[e, old, :] = g_stage[e, old, :] + grow[:ALIGN]
        g_stage[e, new, :] = grow[ALIGN:]

        def place(sl, carry, e=e, start=start, target=target):
            p = jnp.where(slab_row + (sl * SLAB).astype(f32) == target, 1.0, 0.0).astype(bf16)
            rows = jnp.dot(p, xg, preferred_element_type=f32)
            dst = pl.ds(pl.multiple_of(start + sl * SLAB, ALIGN), SLAB)
            x_stage[e, dst, :] = rows[:, :D].astype(bf16)
            g_stage[e, dst, :] = rows[:, D:D + 128] + rows[:, D + 128:]
            return carry

        lax.fori_loop(HEAD_ROWS // SLAB, (phase + n_e + SLAB - 1) // SLAB, place, 0)
        cnt_sc[e] = cnt + n_e
        full.append((fill + n_e >= FT, cnt - fill))

        @pl.when(full[e][0])
        def _():
            for copy in flush_copies(e, full[e][1]):
                copy.start()

    for e in range(N_EXPERTS):
        @pl.when(full[e][0])
        def _():
            for copy in flush_copies(e, full[e][1]):
                copy.wait()
            x_stage[e, 0:STAGE - FT, :] = x_stage[e, FT:STAGE, :]
            x_stage[e, STAGE - FT:STAGE, :] = jnp.zeros((FT, D), bf16)
            g_stage[e, 0:STAGE - FT, :] = g_stage[e, FT:STAGE, :]
            g_stage[e, STAGE - FT:STAGE, :] = jnp.zeros((FT, 128), f32)

    @pl.when(i == n_rt - 1)
    def _():
        for e in range(N_EXPERTS):
            c = cnt_sc[e]
            cnt_ref[e] = c
            for copy in flush_copies(e, c - c % FT):
                copy.start()
                copy.wait()


def _moe_route(h, mod, g, router, n_rt):
    return pl.pallas_call(
        functools.partial(_route_kernel, n_rt=n_rt),
        grid=(n_rt,),
        in_specs=[
            pl.BlockSpec((RT, D), lambda i: (i, 0)),
            pl.BlockSpec((None, 8, D), lambda i: (jnp.minimum(i // RT_PER_SAMPLE, CTX_MOD_ROW), 0, 0)),
            _const_spec((1, D)), _const_spec((D, 128)),
        ],
        out_specs=[
            pl.BlockSpec(memory_space=pl.ANY), pl.BlockSpec(memory_space=pl.ANY),
            pl.BlockSpec((None, N_EXPERTS, RT), lambda i: (i, 0, 0)),
            pl.BlockSpec(memory_space=pltpu.SMEM), pl.BlockSpec(memory_space=pltpu.SMEM),
        ],
        out_shape=[
            jax.ShapeDtypeStruct((N_EXPERTS * CAP, D), bf16),
            jax.ShapeDtypeStruct((N_EXPERTS * CAP, 128), f32),
            jax.ShapeDtypeStruct((n_rt, N_EXPERTS, RT), f32),
            jax.ShapeDtypeStruct((n_rt * N_EXPERTS,), jnp.int32),
            jax.ShapeDtypeStruct((N_EXPERTS,), jnp.int32),
        ],
        scratch_shapes=[
            pltpu.VMEM((N_EXPERTS, STAGE, D), bf16), pltpu.VMEM((N_EXPERTS, STAGE, 128), f32),
            pltpu.SMEM((N_EXPERTS,), jnp.int32), pltpu.SemaphoreType.DMA((N_EXPERTS, 2)),
        ],
        compiler_params=_params(("arbitrary",)),
        name="moe_route",
    )(h, mod, g, router)


def _moe_ffn_kernel(te_ref, tb_ref, nt_ref, x_ref, gs_ref, wg_ref, wu_ref, wd_ref, y_ref, acc_sc):
    j = pl.program_id(0)
    f = pl.program_id(1)

    @pl.when(j < nt_ref[0])
    def _():
        @pl.when(f == 0)
        def _():
            acc_sc[...] = jnp.zeros_like(acc_sc)

        x = x_ref[...]
        gate = jnp.dot(x, wg_ref[...], preferred_element_type=f32)
        up = jnp.dot(x, wu_ref[...], preferred_element_type=f32)
        hid = (_silu(gate) * up).astype(bf16)
        acc_sc[...] += jnp.dot(hid, wd_ref[...], preferred_element_type=f32)

        @pl.when(f == NF - 1)
        def _():
            lane = lax.broadcasted_iota(jnp.int32, (FT, 128), 1)
            ge = jnp.sum(jnp.where(lane == te_ref[j], gs_ref[...], 0.0), axis=1, keepdims=True)
            y_ref[...] = (ge * acc_sc[...]).astype(bf16)


def _moe_experts(xs, gs, wg, wu, wd, layer, tile_expert, tile_block, n_tiles):
    max_tiles = tile_expert.shape[0]

    def f_eff(j, f, nt):
        return jnp.where(j < nt[0], f, NF - 1)

    grid_spec = pltpu.PrefetchScalarGridSpec(
        num_scalar_prefetch=3,
        grid=(max_tiles, NF),
        in_specs=[
            pl.BlockSpec((FT, D), lambda j, f, te, tb, nt: (tb[j], 0)),
            pl.BlockSpec((FT, 128), lambda j, f, te, tb, nt: (tb[j], 0)),
            pl.BlockSpec((None, None, D, TF), lambda j, f, te, tb, nt: (layer, te[j], 0, f_eff(j, f, nt))),
            pl.BlockSpec((None, None, D, TF), lambda j, f, te, tb, nt: (layer, te[j], 0, f_eff(j, f, nt))),
            pl.BlockSpec((None, None, TF, D), lambda j, f, te, tb, nt: (layer, te[j], f_eff(j, f, nt), 0)),
        ],
        out_specs=pl.BlockSpec((FT, D), lambda j, f, te, tb, nt: (tb[j], 0)),
        scratch_shapes=[pltpu.VMEM((FT, D), f32)],
    )
    return pl.pallas_call(
        _moe_ffn_kernel,
        grid_spec=grid_spec,
        out_shape=jax.ShapeDtypeStruct((N_EXPERTS * CAP, D), bf16),
        compiler_params=_params(("arbitrary", "arbitrary")),
        name="moe_experts",
    )(tile_expert, tile_block, n_tiles, xs, gs, wg, wu, wd)


def _moe_combine_kernel(seg_ref, nseg_ref, wmax_ref, rank_ref, h_ref, mod_ref, fg_ref, ys_hbm, o_ref, ybuf, acc_sc,
                        sems, *, n_rt, apply_final):
    i = pl.program_id(0)
    main_row = lax.broadcasted_iota(jnp.int32, (RT, RT), 0).astype(f32)
    last_row = lax.broadcasted_iota(jnp.int32, (ALIGN, RT), 0).astype(f32) + RT

    def window(step, e):
        seg = seg_ref[step * N_EXPERTS + e]
        start = pl.multiple_of(jnp.minimum((seg // ALIGN) * ALIGN, wmax_ref[e]), ALIGN)
        return seg, start

    def window_copy(step, e):
        _, start = window(step, e)
        return pltpu.make_async_copy(ys_hbm.at[pl.ds(e * CAP + start, WIN), :], ybuf.at[step % 2, e],
                                     sems.at[step % 2, e])

    def fetch(step):
        for e in range(N_EXPERTS):
            @pl.when(nseg_ref[step * N_EXPERTS + e] > 0)
            def _():
                window_copy(step, e).start()

    @pl.when(i == 0)
    def _():
        ybuf[...] = jnp.zeros_like(ybuf)
        fetch(0)

    @pl.when(i + 1 < n_rt)
    def _():
        fetch(i + 1)

    acc = None
    spill = False
    for e in range(N_EXPERTS):
        seg, start = window(i, e)
        n = nseg_ref[i * N_EXPERTS + e]

        @pl.when(n > 0)
        def _():
            window_copy(i, e).wait()

        target = rank_ref[e:e + 1, :] + (seg - start).astype(f32)
        p = jnp.where(main_row == target, 1.0, 0.0).astype(bf16)
        part = lax.dot_general(p, ybuf[i % 2, e, 0:RT, :], DN_T, preferred_element_type=f32)
        acc = part if acc is None else acc + part
        spill = spill | (seg - start + n > RT)
    acc_sc[...] = acc

    @pl.when(spill)
    def _():
        for e in range(N_EXPERTS):
            seg, start = window(i, e)
            target = rank_ref[e:e + 1, :] + (seg - start).astype(f32)
            p = jnp.where(last_row == target, 1.0, 0.0).astype(bf16)
            acc_sc[...] += lax.dot_general(p, ybuf[i % 2, e, RT:WIN, :], DN_T, preferred_element_type=f32)

    out = h_ref[...] + mod_ref[5:6, :] * acc_sc[...]
    if apply_final:
        out = out * lax.rsqrt(jnp.mean(out * out, axis=-1, keepdims=True) + EPS) * fg_ref[...]
    o_ref[...] = out


def _moe_combine(ys, rank, seg, nseg, wmax, h, mod, final_g, n_rt, apply_final):
    grid_spec = pltpu.PrefetchScalarGridSpec(
        num_scalar_prefetch=3,
        grid=(n_rt,),
        in_specs=[
            pl.BlockSpec((None, N_EXPERTS, RT), lambda i, *_: (i, 0, 0)),
            pl.BlockSpec((RT, D), lambda i, *_: (i, 0)),
            pl.BlockSpec((None, 8, D), lambda i, *_: (jnp.minimum(i // RT_PER_SAMPLE, CTX_MOD_ROW), 0, 0)),
            pl.BlockSpec((1, D), lambda i, *_: (0, 0)),
            pl.BlockSpec(memory_space=pl.ANY),
        ],
        out_specs=pl.BlockSpec((RT, D), lambda i, *_: (i, 0)),
        scratch_shapes=[pltpu.VMEM((2, N_EXPERTS, WIN, D), bf16), pltpu.VMEM((RT, D), f32),
                        pltpu.SemaphoreType.DMA((2, N_EXPERTS))],
    )
    return pl.pallas_call(
        functools.partial(_moe_combine_kernel, n_rt=n_rt, apply_final=apply_final),
        grid_spec=grid_spec,
        out_shape=jax.ShapeDtypeStruct((n_rt * RT, D), f32),
        compiler_params=_params(("arbitrary",)),
        name="moe_combine",
    )(seg, nseg, wmax, rank, h, mod, final_g, ys)


def _moe(h, mod, g, router, wg, wu, wd, layer, nt, final_g, apply_final):
    n_rt = nt * (TM // RT)
    xs, gs, rank, seg, counts = _moe_route(h, mod, g, router, n_rt)
    tiles = (counts + FT - 1) // FT
    ends = jnp.cumsum(tiles)
    n_tiles = ends[-1]
    max_tiles = (2 * n_rt * RT) // FT + N_EXPERTS
    j = jnp.minimum(jnp.arange(max_tiles, dtype=jnp.int32), n_tiles - 1)
    tile_expert = jnp.sum((j[:, None] >= ends[None, :]).astype(jnp.int32), axis=1)
    tile_block = tile_expert * CAP_BLOCKS + j - (ends - tiles)[tile_expert]
    ys = _moe_experts(xs, gs, wg, wu, wd, layer, tile_expert, tile_block, n_tiles.reshape(1))
    seg2 = seg.reshape(n_rt, N_EXPERTS)
    nseg = (jnp.concatenate([seg2[1:], counts[None, :]], axis=0) - seg2).reshape(-1)
    wmax = jnp.maximum(tiles * FT - WIN, 0)
    return _moe_combine(ys, rank, seg, nseg, wmax, h, mod, final_g, n_rt, apply_final)


def _ssd_in_kernel(h_ref, mod_ref, g_ref, wz_ref, wx_ref, wdt_ref, z_ref, x_ref, dt_ref):
    a = _norm_mod(h_ref[...], g_ref[...], mod_ref[0:1, :], mod_ref[1:2, :]).astype(bf16)
    for j in range(SSD_INNER // D):
        z_ref[:, j * D:(j + 1) * D] = jnp.dot(a, wz_ref[:, j * D:(j + 1) * D], preferred_element_type=f32).astype(bf16)
    for j in range(SSD_CONV_CH // D):
        x_ref[:, j * D:(j + 1) * D] = jnp.dot(a, wx_ref[:, j * D:(j + 1) * D], preferred_element_type=f32).astype(bf16)
    dt_ref[...] = jnp.dot(a, wdt_ref[...], preferred_element_type=f32)


def _ssd_in_proj(h, mod, g, wz, wx, wdt):
    return pl.pallas_call(
        _ssd_in_kernel,
        grid=(NT,),
        in_specs=[_tile_spec(D), _mod_spec(), _const_spec((1, D)), _const_spec((D, SSD_INNER)),
                  _const_spec((D, SSD_CONV_CH)), _const_spec((D, 256))],
        out_specs=[_tile_spec(SSD_INNER), _tile_spec(SSD_CONV_CH), _tile_spec(256)],
        out_shape=[jax.ShapeDtypeStruct((T, SSD_INNER), bf16), jax.ShapeDtypeStruct((T, SSD_CONV_CH), bf16),
                   jax.ShapeDtypeStruct((T, 256), f32)],
        compiler_params=_params(("parallel",)),
        name="ssd_in_proj",
    )(h, mod, g, wz, wx, wdt)


CONV_TM = 256
CONV_HALO = 16
CONV_TILES_PER_SAMPLE = SEQ // CONV_TM


def _conv_kernel(x_ref, prev_ref, next_ref, w_ref, b_ref, o_ref):
    i = pl.program_id(0)
    is_ctx = i >= N_LAT // CONV_TM
    seg_start = is_ctx | (i % CONV_TILES_PER_SAMPLE == 0)
    seg_end = is_ctx | (i % CONV_TILES_PER_SAMPLE == CONV_TILES_PER_SAMPLE - 1)
    x = x_ref[...].astype(f32)
    prev_row = jnp.where(seg_start, 0.0, prev_ref[CONV_HALO - 1:CONV_HALO, :].astype(f32))
    next_row = jnp.where(seg_end, 0.0, next_ref[0:1, :].astype(f32))
    row = lax.broadcasted_iota(jnp.int32, x.shape, 0)
    xm1 = jnp.where(row == 0, prev_row, pltpu.roll(x, 1, 0))
    xp1 = jnp.where(row == CONV_TM - 1, next_row, pltpu.roll(x, CONV_TM - 1, 0))
    y = w_ref[0:1, :] * xm1 + w_ref[1:2, :] * x + w_ref[2:3, :] * xp1 + b_ref[...]
    o_ref[...] = _silu(y).astype(bf16)


def _ssd_conv(xbc, conv_w, conv_b):
    n = T // CONV_TM
    per = CONV_TM // CONV_HALO
    last = T // CONV_HALO - 1
    return pl.pallas_call(
        _conv_kernel,
        grid=(n,),
        in_specs=[
            pl.BlockSpec((CONV_TM, SSD_CONV_CH), lambda i: (i, 0)),
            pl.BlockSpec((CONV_HALO, SSD_CONV_CH), lambda i: (jnp.maximum(i * per - 1, 0), 0)),
            pl.BlockSpec((CONV_HALO, SSD_CONV_CH), lambda i: (jnp.minimum((i + 1) * per, last), 0)),
            _const_spec((8, SSD_CONV_CH)), _const_spec((1, SSD_CONV_CH)),
        ],
        out_specs=pl.BlockSpec((CONV_TM, SSD_CONV_CH), lambda i: (i, 0)),
        out_shape=jax.ShapeDtypeStruct((T, SSD_CONV_CH), bf16),
        compiler_params=_params(("parallel",)),
        name="ssd_conv",
    )(xbc, xbc, xbc, conv_w, conv_b)


def _ssd_scan_kernel(x_ref, b_ref, c_ref, dt_ref, bias_ref, alog_ref, y_ref, state_sc):
    d = pl.program_id(0)
    c = pl.program_id(2)

    @pl.when(c == 0)
    def _():
        state_sc[...] = jnp.zeros_like(state_sc)

    L = SSD_CHUNK
    li = lax.broadcasted_iota(jnp.int32, (L, L), 0)
    si = lax.broadcasted_iota(jnp.int32, (L, L), 1)
    causal = (si - li) * (1 - 2 * d) <= 0
    tri = jnp.where(causal, 1.0, 0.0).astype(bf16)
    hi = lax.broadcasted_iota(jnp.int32, (128, SSD_INNER), 0)
    ci = lax.broadcasted_iota(jnp.int32, (128, SSD_INNER), 1)
    expand = jnp.where(ci // SSD_P == hi, 1.0, 0.0).astype(bf16)

    dt = jax.nn.softplus(dt_ref[...] + bias_ref[...])
    a_neg = -jnp.exp(alog_ref[...])
    da = dt * a_neg
    da_parts = _split3(da)
    cs = (jnp.dot(tri, da_parts[0], preferred_element_type=f32) + jnp.dot(tri, da_parts[1], preferred_element_type=f32)
          + jnp.dot(tri, da_parts[2], preferred_element_type=f32))
    dn_t = (((0,), (1,)), ((), ()))
    cs_t = (lax.dot_general(da_parts[0], tri, dn_t, preferred_element_type=f32)
            + lax.dot_general(da_parts[1], tri, dn_t, preferred_element_type=f32)
            + lax.dot_general(da_parts[2], tri, dn_t, preferred_element_type=f32))
    total = jnp.sum(da, axis=0, keepdims=True)

    e_out = jnp.exp(cs)
    e_in = jnp.exp(total - cs) * dt
    dt_x = jnp.dot(dt.astype(bf16), expand, preferred_element_type=f32).astype(bf16)
    e_in_x = jnp.dot(e_in.astype(bf16), expand, preferred_element_type=f32).astype(bf16)
    out_scale = _dot3(_split_bf16(e_out), expand)
    chunk_decay = _dot3(_split_bf16(jnp.broadcast_to(jnp.exp(total), (8, 128))), expand)[0:1]

    x = x_ref[...]
    xdt = x * dt_x
    xw = x * e_in_x

    for g in range(SSD_GROUPS):
        bg = b_ref[:, g * SSD_N:(g + 1) * SSD_N]
        cg = c_ref[:, g * SSD_N:(g + 1) * SSD_N]
        cb = lax.dot_general(cg, bg, (((1,), (1,)), ((), ())), preferred_element_type=f32)
        gsl = slice(g * SSD_HPG * SSD_P, (g + 1) * SSD_HPG * SSD_P)
        st = state_sc[g]
        y_off = jnp.dot(cg, st.astype(bf16), preferred_element_type=f32) * out_scale[:, gsl]
        ys = []
        for r in range(SSD_HPG):
            hd = g * SSD_HPG + r
            seg = cs[:, hd:hd + 1] - cs_t[hd:hd + 1, :]
            decay = jnp.exp(jnp.where(causal, seg, -jnp.inf))
            m = (cb * decay).astype(bf16)
            ys.append(jnp.dot(m, xdt[:, hd * SSD_P:(hd + 1) * SSD_P], preferred_element_type=f32))
        y_ref[:, gsl] = (jnp.concatenate(ys, axis=1) + y_off).astype(bf16)
        new = lax.dot_general(bg, xw[:, gsl], (((0,), (0,)), ((), ())), preferred_element_type=f32)
        state_sc[g] = st * chunk_decay[:, gsl] + new


SSD_NCHUNK = (SEQ + CTX) // SSD_CHUNK
SSD_CTX_CHUNKS = CTX // SSD_CHUNK
SSD_LAT_CHUNKS = SEQ // SSD_CHUNK


def _ssd_chunk_block(d, b, c):
    ctx_j = jnp.where(d == 0, c, SSD_CTX_CHUNKS - 1 - c)
    lat_j = jnp.where(d == 0, c - SSD_CTX_CHUNKS, SSD_LAT_CHUNKS - 1 - (c - SSD_CTX_CHUNKS))
    return jnp.where(c < SSD_CTX_CHUNKS, N_LAT // SSD_CHUNK + b * SSD_CTX_CHUNKS + ctx_j, b * SSD_LAT_CHUNKS + lat_j)


def _ssd_scan(xbc, dt, dt_bias, a_log):
    x_blk = SSD_INNER // SSD_BC
    return pl.pallas_call(
        _ssd_scan_kernel,
        grid=(2, BATCH, SSD_NCHUNK),
        in_specs=[
            pl.BlockSpec((SSD_CHUNK, SSD_INNER), lambda d, b, c: (_ssd_chunk_block(d, b, c), 0)),
            pl.BlockSpec((SSD_CHUNK, SSD_BC), lambda d, b, c: (_ssd_chunk_block(d, b, c), x_blk)),
            pl.BlockSpec((SSD_CHUNK, SSD_BC), lambda d, b, c: (_ssd_chunk_block(d, b, c), x_blk + 1)),
            pl.BlockSpec((SSD_CHUNK, 128), lambda d, b, c: (_ssd_chunk_block(d, b, c), d)),
            pl.BlockSpec((None, 1, 128), lambda d, b, c: (d, 0, 0)),
            pl.BlockSpec((None, 1, 128), lambda d, b, c: (d, 0, 0)),
        ],
        out_specs=pl.BlockSpec((None, SSD_CHUNK, SSD_INNER), lambda d, b, c: (d, _ssd_chunk_block(d, b, c), 0)),
        out_shape=jax.ShapeDtypeStruct((2, T, SSD_INNER), bf16),
        scratch_shapes=[pltpu.VMEM((SSD_GROUPS, SSD_N, SSD_HPG * SSD_P), f32)],
        compiler_params=_params(("parallel", "parallel", "arbitrary")),
        name="ssd_scan",
    )(xbc, xbc, xbc, dt, dt_bias, a_log)


def _ssd_out_kernel(yf_ref, yb_ref, x_ref, z_ref, dskip_ref, ng_ref, w_ref, h_ref, mod_ref, o_ref):
    y = yf_ref[...].astype(f32) + yb_ref[...].astype(f32) + x_ref[...].astype(f32) * dskip_ref[...]
    gated = y * _silu(z_ref[...].astype(f32))
    gw = SSD_INNER // SSD_GROUPS
    parts = []
    for g in range(SSD_GROUPS):
        s = gated[:, g * gw:(g + 1) * gw]
        parts.append(s * lax.rsqrt(jnp.mean(s * s, axis=-1, keepdims=True) + EPS))
    normed = (jnp.concatenate(parts, axis=1) * ng_ref[...]).astype(bf16)
    out = jnp.dot(normed, w_ref[...], preferred_element_type=f32)
    o_ref[...] = h_ref[...] + mod_ref[2:3, :] * out


def _ssd_out(y2, xbc, z, dskip, ng, w_out, h, mod, nt):
    return pl.pallas_call(
        _ssd_out_kernel,
        grid=(nt,),
        in_specs=[
            pl.BlockSpec((None, TM, SSD_INNER), lambda t: (0, t, 0)),
            pl.BlockSpec((None, TM, SSD_INNER), lambda t: (1, t, 0)),
            pl.BlockSpec((TM, SSD_INNER), lambda t: (t, 0)),
            _tile_spec(SSD_INNER), _const_spec((1, SSD_INNER)), _const_spec((1, SSD_INNER)),
            _const_spec((SSD_INNER, D)), _tile_spec(D), _mod_spec(),
        ],
        out_specs=_tile_spec(D),
        out_shape=jax.ShapeDtypeStruct((nt * TM, D), f32),
        compiler_params=_params(("parallel",)),
        name="ssd_out",
    )(y2, y2, xbc, z, dskip, ng, w_out, h, mod)


def _gelu(x):
    return 0.5 * x * (1.0 + lax.erf(x * math.sqrt(0.5)))


def _cmlp_kernel(h_ref, mod_ref, g_ref, wu_ref, wv_ref, bu_ref, bv_ref, vg_ref, ws_ref, bs_ref, wo_ref, o_ref, uv_sc):
    a = _norm_mod(h_ref[...], g_ref[...], mod_ref[0:1, :], mod_ref[1:2, :]).astype(bf16)
    v = _gelu(jnp.dot(a, wv_ref[...], preferred_element_type=f32) + bv_ref[...])
    v = (v * lax.rsqrt(jnp.mean(v * v, axis=-1, keepdims=True) + EPS) * vg_ref[...]).astype(bf16)
    u = _gelu(jnp.dot(a, wu_ref[...], preferred_element_type=f32) + bu_ref[...])
    for ck in range(TM // CMLP_CHUNK):
        rows = slice(ck * CMLP_CHUNK, (ck + 1) * CMLP_CHUNK)
        for g in range(CMLP_GROUPS):
            cols = slice(g * CMLP_GW, (g + 1) * CMLP_GW)
            mixed = jnp.dot(ws_ref[g], v[rows, cols], preferred_element_type=f32) + bs_ref[:, cols]
            uv_sc[rows, cols] = (u[rows, cols] * mixed).astype(bf16)
    out = jnp.dot(uv_sc[...], wo_ref[...], preferred_element_type=f32)
    o_ref[...] = h_ref[...] + mod_ref[2:3, :] * out


def _cmlp(h, mod, g, wu, wv, bu, bv, vg, ws, bs, wo, nt):
    return pl.pallas_call(
        _cmlp_kernel,
        grid=(nt,),
        in_specs=[
            _tile_spec(D), _mod_spec(), _const_spec((1, D)), _const_spec((D, CMLP_D)), _const_spec((D, CMLP_D)),
            _const_spec((1, CMLP_D)), _const_spec((1, CMLP_D)), _const_spec((1, CMLP_D)),
            _const_spec((CMLP_GROUPS, CMLP_CHUNK, CMLP_CHUNK)), _const_spec((CMLP_CHUNK, CMLP_D)),
            _const_spec((CMLP_D, D)),
        ],
        out_specs=_tile_spec(D),
        out_shape=jax.ShapeDtypeStruct((nt * TM, D), f32),
        scratch_shapes=[pltpu.VMEM((TM, CMLP_D), bf16)],
        compiler_params=_params(("parallel",)),
        name="cmlp",
    )(h, mod, g, wu, wv, bu, bv, vg, ws, bs, wo)


def _rope_tables():
    pos = jnp.arange(SEQ)
    inv_freq = 1.0 / (ROPE_THETA ** (jnp.arange(0, ROPE_AXIS_DIM, 2, dtype=f32) / ROPE_AXIS_DIM))
    ang_r = (pos // GRID_W).astype(f32)[:, None] * inv_freq
    ang_c = (pos % GRID_W).astype(f32)[:, None] * inv_freq
    cos = jnp.concatenate([jnp.cos(ang_r)] * 2 + [jnp.cos(ang_c)] * 2, axis=1)
    sin = jnp.concatenate([-jnp.sin(ang_r), jnp.sin(ang_r), -jnp.sin(ang_c), jnp.sin(ang_c)], axis=1)
    cos = jnp.concatenate([cos, jnp.ones((TM, HEAD_DIM), f32)], axis=0)
    sin = jnp.concatenate([sin, jnp.zeros((TM, HEAD_DIM), f32)], axis=0)
    return cos, sin


def _row(v):
    return v.reshape(1, -1)


def _pad_lanes(v, width=128):
    return jnp.pad(v, ((0, 0), (0, width - v.shape[1])))


def kernel(x, c, ctx, c_ctx, w_mod, b_mod, norm1_g, norm2_g, attn_w_qkv, attn_q_g, attn_k_g, attn_w_o, ssd_w_in, ssd_conv_w, ssd_conv_b, ssd_dt_bias_f, ssd_dt_bias_b, ssd_a_log_f, ssd_a_log_b, ssd_d_skip, ssd_norm_g, ssd_w_out, cmlp_w_in, cmlp_b_in, cmlp_v_g, cmlp_w_s, cmlp_b_s, cmlp_w_out, ffn_w_gate, ffn_w_up, ffn_w_down, moe_router, moe_w_gate, moe_w_up, moe_w_down, final_g):
    h = jnp.concatenate([x.reshape(N_LAT, D), ctx.reshape(N_CTX, D)], axis=0)
    cond = jnp.concatenate([c, c_ctx[None, :], jnp.zeros((8 - BATCH - 1, D), f32)], axis=0)
    mods = _modulation(cond, w_mod, b_mod)
    cos_t, sin_t = _rope_tables()
    ffn_w = [w.astype(bf16) for w in (ffn_w_gate, ffn_w_up, ffn_w_down)]
    moe_w = [w.astype(bf16) for w in (moe_w_gate, moe_w_up, moe_w_down)]
    final_row = _row(final_g)

    for i in range(DEPTH):
        need_ctx = i < DEPTH - 1
        nt = NT if need_ctx else NT_LAT
        mod = mods[i]
        kind, j = i % 3, i // 3
        g1 = _row(norm1_g[i])
        if kind == 0:
            q, k, v = _qkv_proj(h, mod, g1, attn_w_qkv[j].astype(bf16), _row(attn_q_g[j]), _row(attn_k_g[j]), cos_t, sin_t)
            o = _attention(q, k, v, need_ctx)
            h = _proj_res(o, attn_w_o[j].astype(bf16), h, mod, nt)
        elif kind == 1:
            w_in = ssd_w_in[j]
            wz = w_in[:, :SSD_INNER].astype(bf16)
            wx = w_in[:, SSD_INNER:SSD_INNER + SSD_CONV_CH].astype(bf16)
            w_dt = w_in[:, SSD_INNER + SSD_CONV_CH:]
            wdt = jnp.concatenate([_pad_lanes(w_dt[:, :SSD_HEADS]), _pad_lanes(w_dt[:, SSD_HEADS:])], axis=1).astype(bf16)
            z, xbc, dt = _ssd_in_proj(h, mod, g1, wz, wx, wdt)
            conv_w = jnp.pad(ssd_conv_w[j], ((0, 8 - ssd_conv_w.shape[1]), (0, 0)))
            xbc = _ssd_conv(xbc, conv_w, _row(ssd_conv_b[j]))
            dt_bias = jnp.stack([_pad_lanes(_row(ssd_dt_bias_f[j])), _pad_lanes(_row(ssd_dt_bias_b[j]))])
            a_log = jnp.stack([_pad_lanes(_row(ssd_a_log_f[j])), _pad_lanes(_row(ssd_a_log_b[j]))])
            y2 = _ssd_scan(xbc, dt, dt_bias, a_log)
            dskip = _row(jnp.repeat(ssd_d_skip[j], SSD_P))
            h = _ssd_out(y2, xbc, z, dskip, _row(ssd_norm_g[j]), ssd_w_out[j].astype(bf16), h, mod, nt)
        else:
            w_in = cmlp_w_in[j]
            b_in = cmlp_b_in[j]
            bs = jnp.repeat(cmlp_b_s[j].T, CMLP_GW, axis=1)
            h = _cmlp(h, mod, g1, w_in[:, :CMLP_D].astype(bf16), w_in[:, CMLP_D:].astype(bf16),
                      _row(b_in[:CMLP_D]), _row(b_in[CMLP_D:]), _row(cmlp_v_g[j]), cmlp_w_s[j].astype(bf16), bs,
                      cmlp_w_out[j].astype(bf16), nt)
        kk = i // 2
        g2 = _row(norm2_g[i])
        if i % 2 == 0:
            h = _ffn(h, mod, g2, *ffn_w, kk, nt)
        else:
            h = _moe(h, mod, g2, _pad_lanes(moe_router[kk]), *moe_w, kk, nt, final_row, i == DEPTH - 1)
    return h.reshape(BATCH, SEQ, D)
```

```python
import functools
import math

import jax
import jax.numpy as jnp
from jax import lax
from jax.experimental import pallas as pl
from jax.experimental.pallas import tpu as pltpu

f32 = jnp.float32
bf16 = jnp.bfloat16

D = 1024
BATCH = 2
SEQ = 8192
CTX = 256
DEPTH = 4
GRID_W = 64
EPS = 1e-6
N_MOD = 6

HEAD_DIM = 128
N_HEADS = 8
N_KV = 2
Q_GROUP = 4
ROPE_AXIS_DIM = 64
ROPE_THETA = 10000.0

SSD_INNER = 2048
SSD_P = 64
SSD_HEADS = 32
SSD_GROUPS = 4
SSD_HPG = 8
SSD_N = 128
SSD_CHUNK = 128
SSD_BC = SSD_GROUPS * SSD_N
SSD_CONV_CH = SSD_INNER + 2 * SSD_BC

CMLP_D = 2048
CMLP_GROUPS = 8
CMLP_GW = 256
CMLP_CHUNK = 128

D_FF = 3584
N_EXPERTS = 8

N_LAT = BATCH * SEQ
N_CTX = BATCH * CTX
T = N_LAT + N_CTX
TM = 512
NT = T // TM
NT_LAT = N_LAT // TM
TILES_PER_SAMPLE = SEQ // TM
CTX_MOD_ROW = BATCH

TF = 1792
NF = D_FF // TF

TQ = 512
TK = 512
ATT_CHUNKS = SEQ // TK
VW = 2 * HEAD_DIM
LOG2E = math.log2(math.e)
NQ_LAT = SEQ // TQ

VMEM_LIMIT = 56 * 1024 * 1024


def _mod_row(t):
    return jnp.minimum(t // TILES_PER_SAMPLE, CTX_MOD_ROW)


def _tile_spec(width):
    return pl.BlockSpec((TM, width), lambda t: (t, 0))


def _mod_spec():
    return pl.BlockSpec((None, 8, D), lambda t: (_mod_row(t), 0, 0))


def _const_spec(shape):
    n = len(shape)
    return pl.BlockSpec(shape, lambda *_: (0,) * n)


def _params(semantics):
    return pltpu.CompilerParams(dimension_semantics=semantics, vmem_limit_bytes=VMEM_LIMIT)


def _silu(x):
    return x * jax.nn.sigmoid(x)


def _norm_mod(x, g, shift, scale):
    y = x * lax.rsqrt(jnp.mean(x * x, axis=-1, keepdims=True) + EPS) * g
    return y * (1.0 + scale) + shift


def _mod_kernel(c_ref, w_ref, b_ref, o_ref):
    s = _silu(c_ref[...])
    o_ref[...] = jnp.dot(s.astype(bf16), w_ref[...].astype(bf16), preferred_element_type=f32) + b_ref[...]


def _modulation(cond, w_mod, b_mod):
    nblk = N_MOD * D // D
    out = pl.pallas_call(
        _mod_kernel,
        grid=(DEPTH, nblk),
        in_specs=[
            pl.BlockSpec((8, D), lambda i, j: (0, 0)),
            pl.BlockSpec((None, D, D), lambda i, j: (i, 0, j)),
            pl.BlockSpec((None, 1, D), lambda i, j: (i, 0, j)),
        ],
        out_specs=pl.BlockSpec((None, 8, D), lambda i, j: (i, 0, j)),
        out_shape=jax.ShapeDtypeStruct((DEPTH, 8, N_MOD * D), f32),
        compiler_params=_params(("arbitrary", "arbitrary")),
        name="modulation",
    )(cond, w_mod, b_mod.reshape(DEPTH, 1, N_MOD * D))
    mod = out[:, :3].reshape(DEPTH, 3, N_MOD, D)
    return jnp.pad(mod, ((0, 0), (0, 0), (0, 8 - N_MOD), (0, 0)))


def _qkv_kernel(h_ref, mod_ref, g_ref, w_ref, qg_ref, kg_ref, cos_ref, sin_ref, q_ref, k_ref, v_ref):
    a = _norm_mod(h_ref[...], g_ref[...], mod_ref[0:1, :], mod_ref[1:2, :])
    qkv = jnp.dot(a.astype(bf16), w_ref[...], preferred_element_type=f32)
    cos = cos_ref[...]
    sin = sin_ref[...]
    lane = lax.broadcasted_iota(jnp.int32, (TM, HEAD_DIM), 1)
    first_half = (lane % ROPE_AXIS_DIM) < (ROPE_AXIS_DIM // 2)

    def head(x, g, scale):
        y = x * lax.rsqrt(jnp.mean(x * x, axis=-1, keepdims=True) + EPS) * g
        partner = jnp.where(first_half, pltpu.roll(y, HEAD_DIM - 32, 1), pltpu.roll(y, 32, 1))
        return (y * cos + partner * sin) * scale

    qg = qg_ref[...]
    kg = kg_ref[...]
    for hh in range(N_HEADS):
        sl = slice(hh * HEAD_DIM, (hh + 1) * HEAD_DIM)
        q_ref[:, sl] = head(qkv[:, sl], qg, HEAD_DIM ** -0.5 * LOG2E).astype(bf16)
    for hh in range(N_KV):
        sl = slice(hh * HEAD_DIM, (hh + 1) * HEAD_DIM)
        src = slice((N_HEADS + hh) * HEAD_DIM, (N_HEADS + hh + 1) * HEAD_DIM)
        k_ref[:, sl] = head(qkv[:, src], kg, 1.0).astype(bf16)
        v_ref[:, hh * VW:hh * VW + HEAD_DIM] = qkv[:, (N_HEADS + N_KV + hh) * HEAD_DIM:(N_HEADS + N_KV + hh + 1) * HEAD_DIM].astype(bf16)
        v_ref[:, hh * VW + HEAD_DIM:(hh + 1) * VW] = jnp.ones((TM, HEAD_DIM), bf16)


def _qkv_proj(h, mod, g, w_qkv, q_g, k_g, cos_t, sin_t):
    rope_spec = pl.BlockSpec((TM, HEAD_DIM), lambda t: (jnp.where(t < NT_LAT, t % TILES_PER_SAMPLE, TILES_PER_SAMPLE), 0))
    return pl.pallas_call(
        _qkv_kernel,
        grid=(NT,),
        in_specs=[
            _tile_spec(D), _mod_spec(), _const_spec((1, D)), _const_spec((D, (N_HEADS + 2 * N_KV) * HEAD_DIM)),
            _const_spec((1, HEAD_DIM)), _const_spec((1, HEAD_DIM)), rope_spec, rope_spec,
        ],
        out_specs=[_tile_spec(N_HEADS * HEAD_DIM), _tile_spec(N_KV * HEAD_DIM), _tile_spec(N_KV * VW)],
        out_shape=[
            jax.ShapeDtypeStruct((T, N_HEADS * HEAD_DIM), bf16),
            jax.ShapeDtypeStruct((T, N_KV * HEAD_DIM), bf16),
            jax.ShapeDtypeStruct((T, N_KV * VW), bf16),
        ],
        compiler_params=_params(("parallel",)),
        name="attn_qkv",
    )(h, mod, g, w_qkv, q_g, k_g, cos_t, sin_t)


def _stack_heads(q):
    return jnp.concatenate([q[:, i * HEAD_DIM:(i + 1) * HEAD_DIM] for i in range(Q_GROUP)], axis=0)


def _scores(qs, k):
    return lax.dot_general(qs, k, (((1,), (1,)), ((), ())), preferred_element_type=f32)


def _attn_kernel(q_ref, kl_ref, vl_ref, kc_ref, vc_ref, o_ref, m_sc, acc_sc, s_a, s_b, s_c):
    qs = _stack_heads(q_ref[...])

    def absorb(s, v):
        m_prev = m_sc[...]
        m_next = jnp.maximum(m_prev, jnp.max(s, axis=1, keepdims=True))
        alpha = jnp.exp2(m_prev - m_next)
        p = jnp.exp2(s - jnp.concatenate([m_next] * (s.shape[1] // HEAD_DIM), axis=1))
        acc_sc[...] = (jnp.concatenate([alpha, alpha], axis=1) * acc_sc[...]
                       + jnp.dot(p.astype(bf16), v, preferred_element_type=f32))
        m_sc[...] = m_next

    def chunk(ref, c):
        return ref[pl.ds(pl.multiple_of(c * TK, TK), TK), :]

    m_sc[...] = jnp.full_like(m_sc, -jnp.inf)
    acc_sc[...] = jnp.zeros_like(acc_sc)
    s_c[...] = _scores(qs, kc_ref[...])
    s_a[...] = _scores(qs, chunk(kl_ref, 0))

    def body(c2, carry):
        c = 2 * c2
        s_b[...] = _scores(qs, chunk(kl_ref, c + 1))
        absorb(s_a[...], chunk(vl_ref, c))
        s_a[...] = _scores(qs, chunk(kl_ref, c + 2))
        absorb(s_b[...], chunk(vl_ref, c + 1))
        return carry
    lax.fori_loop(0, ATT_CHUNKS // 2 - 1, body, 0)

    s_b[...] = _scores(qs, chunk(kl_ref, ATT_CHUNKS - 1))
    absorb(s_a[...], chunk(vl_ref, ATT_CHUNKS - 2))
    absorb(s_b[...], chunk(vl_ref, ATT_CHUNKS - 1))
    absorb(s_c[...], vc_ref[...])

    acc = acc_sc[...]
    o = acc[:, :HEAD_DIM] / acc[:, HEAD_DIM:]
    for i in range(Q_GROUP):
        o_ref[:, i * HEAD_DIM:(i + 1) * HEAD_DIM] = o[i * TQ:(i + 1) * TQ].astype(bf16)


def _attn_ctx_kernel(q_ref, kc_ref, vc_ref, o_ref):
    s = _scores(_stack_heads(q_ref[...]), kc_ref[...])
    p = jnp.exp2(s - jnp.max(s, axis=1, keepdims=True))
    acc = jnp.dot(p.astype(bf16), vc_ref[...], preferred_element_type=f32)
    o = acc[:, :HEAD_DIM] / acc[:, HEAD_DIM:]
    for i in range(Q_GROUP):
        o_ref[:, i * HEAD_DIM:(i + 1) * HEAD_DIM] = o[i * CTX:(i + 1) * CTX].astype(bf16)


def _attention(q, k, v):
    ctx_blk = N_LAT // CTX

    def q_map(b, kh, qi):
        return (b * NQ_LAT + qi, kh)

    def lat_spec(width):
        return pl.BlockSpec((SEQ, width), lambda b, kh, qi: (b, kh))

    def ctx_spec(width):
        return pl.BlockSpec((CTX, width), lambda b, kh, qi: (ctx_blk + b, kh))

    rows = Q_GROUP * TQ
    return pl.pallas_call(
        _attn_kernel,
        grid=(BATCH, N_KV, NQ_LAT),
        in_specs=[pl.BlockSpec((TQ, Q_GROUP * HEAD_DIM), q_map), lat_spec(HEAD_DIM), lat_spec(VW),
                  ctx_spec(HEAD_DIM), ctx_spec(VW)],
        out_specs=pl.BlockSpec((TQ, Q_GROUP * HEAD_DIM), q_map),
        out_shape=jax.ShapeDtypeStruct((N_LAT, N_HEADS * HEAD_DIM), bf16),
        scratch_shapes=[pltpu.VMEM((rows, HEAD_DIM), f32), pltpu.VMEM((rows, VW), f32),
                        pltpu.VMEM((rows, TK), f32), pltpu.VMEM((rows, TK), f32), pltpu.VMEM((rows, CTX), f32)],
        compiler_params=_params(("parallel", "parallel", "arbitrary")),
        name="attn_core",
    )(q, k, v, k, v)


def _attention_ctx(q, k, v):
    ctx_blk = N_LAT // CTX
    return pl.pallas_call(
        _attn_ctx_kernel,
        grid=(BATCH, N_KV),
        in_specs=[pl.BlockSpec((CTX, Q_GROUP * HEAD_DIM), lambda b, kh: (ctx_blk + b, kh)),
                  pl.BlockSpec((CTX, HEAD_DIM), lambda b, kh: (ctx_blk + b, kh)),
                  pl.BlockSpec((CTX, VW), lambda b, kh: (ctx_blk + b, kh))],
        out_specs=pl.BlockSpec((CTX, Q_GROUP * HEAD_DIM), lambda b, kh: (b, kh)),
        out_shape=jax.ShapeDtypeStruct((N_CTX, N_HEADS * HEAD_DIM), bf16),
        compiler_params=_params(("parallel", "parallel")),
        name="attn_ctx",
    )(q, k, v)


def _proj_res_kernel(y_ref, w_ref, h_ref, mod_ref, o_ref):
    y = jnp.dot(y_ref[...], w_ref[...], preferred_element_type=f32)
    o_ref[...] = h_ref[...] + mod_ref[2:3, :] * y


def _proj_res2_kernel(yl_ref, yc_ref, w_ref, h_ref, mod_ref, o_ref):
    y_in = jnp.where(pl.program_id(0) < NT_LAT, yl_ref[...], yc_ref[...])
    y = jnp.dot(y_in, w_ref[...], preferred_element_type=f32)
    o_ref[...] = h_ref[...] + mod_ref[2:3, :] * y


def _proj_res(y, w, h, mod, nt):
    kdim = y.shape[1]
    return pl.pallas_call(
        _proj_res_kernel,
        grid=(nt,),
        in_specs=[_tile_spec(kdim), _const_spec((kdim, D)), _tile_spec(D), _mod_spec()],
        out_specs=_tile_spec(D),
        out_shape=jax.ShapeDtypeStruct((nt * TM, D), f32),
        compiler_params=_params(("parallel",)),
        name="proj_res",
    )(y, w, h, mod)


def _proj_res2(y_lat, y_ctx, w, h, mod):
    kdim = y_lat.shape[1]
    return pl.pallas_call(
        _proj_res2_kernel,
        grid=(NT,),
        in_specs=[pl.BlockSpec((TM, kdim), lambda t: (jnp.minimum(t, NT_LAT - 1), 0)), _const_spec((TM, kdim)),
                  _const_spec((kdim, D)), _tile_spec(D), _mod_spec()],
        out_specs=_tile_spec(D),
        out_shape=jax.ShapeDtypeStruct((T, D), f32),
        compiler_params=_params(("parallel",)),
        name="proj_res2",
    )(y_lat, y_ctx, w, h, mod)


def _ffn_kernel(h_ref, mod_ref, g_ref, wg_ref, wu_ref, wd_ref, o_ref, xn_sc, acc_sc):
    f = pl.program_id(1)

    @pl.when(f == 0)
    def _():
        xn_sc[...] = _norm_mod(h_ref[...], g_ref[...], mod_ref[3:4, :], mod_ref[4:5, :]).astype(bf16)
        acc_sc[...] = jnp.zeros_like(acc_sc)

    x = xn_sc[...]
    gate = jnp.dot(x, wg_ref[...], preferred_element_type=f32)
    up = jnp.dot(x, wu_ref[...], preferred_element_type=f32)
    hid = (_silu(gate) * up).astype(bf16)
    acc_sc[...] += jnp.dot(hid, wd_ref[...], preferred_element_type=f32)

    @pl.when(f == NF - 1)
    def _():
        o_ref[...] = h_ref[...] + mod_ref[5:6, :] * acc_sc[...]


def _ffn(h, mod, g, wg, wu, wd, layer, nt):
    return pl.pallas_call(
        _ffn_kernel,
        grid=(nt, NF),
        in_specs=[
            pl.BlockSpec((TM, D), lambda t, f: (t, 0)),
            pl.BlockSpec((None, 8, D), lambda t, f: (_mod_row(t), 0, 0)),
            pl.BlockSpec((1, D), lambda t, f: (0, 0)),
            pl.BlockSpec((None, D, TF), lambda t, f: (layer, 0, f)),
            pl.BlockSpec((None, D, TF), lambda t, f: (layer, 0, f)),
            pl.BlockSpec((None, TF, D), lambda t, f: (layer, f, 0)),
        ],
        out_specs=pl.BlockSpec((TM, D), lambda t, f: (t, 0)),
        out_shape=jax.ShapeDtypeStruct((nt * TM, D), f32),
        scratch_shapes=[pltpu.VMEM((TM, D), bf16), pltpu.VMEM((TM, D), f32)],
        compiler_params=_params(("parallel", "arbitrary")),
        name="ffn_dense",
    )(h, mod, g, wg, wu, wd)


def _split_bf16(x):
    hi = x.astype(bf16)
    lo = (x - hi.astype(f32)).astype(bf16)
    return hi, lo


def _split3(x):
    p0 = x.astype(bf16)
    r = x - p0.astype(f32)
    p1 = r.astype(bf16)
    p2 = (r - p1.astype(f32)).astype(bf16)
    return p0, p1, p2


def _dot3(a_parts, b):
    out = jnp.dot(a_parts[0], b, preferred_element_type=f32)
    for a in a_parts[1:]:
        out = out + jnp.dot(a, b, preferred_element_type=f32)
    return out


RT = 256
RT_PER_SAMPLE = SEQ // RT
ALIGN = 16
WIN = RT + ALIGN
FT = 512
HEAD_ROWS = 128
SLAB = 64
STAGE = FT + -(-WIN // SLAB) * SLAB
CAP = 17408
CAP_BLOCKS = CAP // FT
DN_T = (((0,), (0,)), ((), ()))
NO_ROW = -1e9


def _route_kernel(h_ref, mod_ref, g_ref, r_ref, xs_hbm, gs_hbm, rank_ref, seg_ref, cnt_ref,
                  x_stage, g_stage, cnt_sc, sems, *, n_rt):
    i = pl.program_id(0)

    @pl.when(i == 0)
    def _():
        x_stage[...] = jnp.zeros_like(x_stage)
        g_stage[...] = jnp.zeros_like(g_stage)
        for e in range(N_EXPERTS):
            cnt_sc[e] = 0

    x = _norm_mod(h_ref[...], g_ref[...], mod_ref[3:4, :], mod_ref[4:5, :])
    xh, xl = _split_bf16(x)
    rh, rl = _split_bf16(r_ref[...])
    logits = (jnp.dot(xh, rh, preferred_element_type=f32) + jnp.dot(xl, rh, preferred_element_type=f32)
              + jnp.dot(xh, rl, preferred_element_type=f32))
    lane = lax.broadcasted_iota(jnp.int32, (RT, 128), 1)
    lg = jnp.where(lane < N_EXPERTS, logits, -jnp.inf)
    m1 = jnp.max(lg, axis=1, keepdims=True)
    i1 = jnp.min(jnp.where(lg == m1, lane, 128), axis=1, keepdims=True)
    lg2 = jnp.where(lane == i1, -jnp.inf, lg)
    m2 = jnp.max(lg2, axis=1, keepdims=True)
    i2 = jnp.min(jnp.where(lg2 == m2, lane, 128), axis=1, keepdims=True)
    e2 = jnp.exp(m2 - m1)
    den = 1.0 + e2
    gates = jnp.where(lane == i1, 1.0 / den, 0.0) + jnp.where(lane == i2, e2 / den, 0.0)
    used = jnp.where((lane == i1) | (lane == i2), 1.0, 0.0).astype(bf16)

    tp = lax.broadcasted_iota(jnp.int32, (RT, RT), 0)
    tt = lax.broadcasted_iota(jnp.int32, (RT, RT), 1)
    earlier = jnp.where(tp < tt, 1.0, 0.0).astype(bf16)
    eye = jnp.where(tp == tt, 1.0, 0.0).astype(bf16)
    rank_t = lax.dot_general(used, earlier, DN_T, preferred_element_type=f32)
    used_t = lax.dot_general(used, eye, DN_T, preferred_element_type=f32)
    rank_t = jnp.where(used_t > 0, rank_t, NO_ROW)
    rank_ref[...] = rank_t[0:N_EXPERTS]

    g_hi, g_lo = _split_bf16(gates)
    xg = jnp.concatenate([xh, g_hi, g_lo], axis=1)
    head_row = lax.broadcasted_iota(jnp.int32, (HEAD_ROWS, RT), 0).astype(f32)
    slab_row = lax.broadcasted_iota(jnp.int32, (SLAB, RT), 0).astype(f32)

    def flush_copies(e, done_rows):
        dst = pl.multiple_of(e * CAP + done_rows, FT)
        return (pltpu.make_async_copy(x_stage.at[e, pl.ds(0, FT), :], xs_hbm.at[pl.ds(dst, FT), :], sems.at[e, 0]),
                pltpu.make_async_copy(g_stage.at[e, pl.ds(0, FT), :], gs_hbm.at[pl.ds(dst, FT), :], sems.at[e, 1]))

    state = []
    n_of = []
    for e in range(N_EXPERTS):
        cnt = cnt_sc[e]
        seg_ref[i * N_EXPERTS + e] = cnt
        fill = cnt % FT
        start = pl.multiple_of((fill // ALIGN) * ALIGN, ALIGN)
        n_e = jnp.sum(used_t[e:e + 1, :]).astype(jnp.int32)
        cnt_sc[e] = cnt + n_e
        n_of.append(n_e)
        state.append((start, fill - start, fill + n_e >= FT, cnt - fill))

    def target_rows(e):
        return rank_t[e:e + 1, :] + state[e][1].astype(f32)

    for e in range(N_EXPERTS):
        start = state[e][0]
        p = jnp.where(head_row == target_rows(e), 1.0, 0.0).astype(bf16)
        rows = jnp.dot(p, xg, preferred_element_type=f32)
        grow = rows[:, D:D + 128] + rows[:, D + 128:]
        old = pl.ds(start, ALIGN)
        new = pl.ds(pl.multiple_of(start + ALIGN, ALIGN), HEAD_ROWS - ALIGN)
        x_stage[e, old, :] = (x_stage[e, old, :].astype(f32) + rows[:ALIGN, :D]).astype(bf16)
        x_stage[e, new, :] = rows[ALIGN:, :D].astype(bf16)
        g_stage[e, old, :] = g_stage[e, old, :] + grow[:ALIGN]
        g_stage[e, new, :] = grow[ALIGN:]

    crowded = False
    for e in range(N_EXPERTS):
        crowded = crowded | (state[e][1] + n_of[e] > HEAD_ROWS)

    @pl.when(crowded)
    def _():
        for e in range(N_EXPERTS):
            def place(sl, carry, e=e):
                p = jnp.where(slab_row + (sl * SLAB).astype(f32) == target_rows(e), 1.0, 0.0).astype(bf16)
                rows = jnp.dot(p, xg, preferred_element_type=f32)
                dst = pl.ds(pl.multiple_of(state[e][0] + sl * SLAB, ALIGN), SLAB)
                x_stage[e, dst, :] = rows[:, :D].astype(bf16)
                g_stage[e, dst, :] = rows[:, D:D + 128] + rows[:, D + 128:]
                return carry
            lax.fori_loop(HEAD_ROWS // SLAB, (state[e][1] + n_of[e] + SLAB - 1) // SLAB, place, 0)

    full = [(st[2], st[3]) for st in state]
    for e in range(N_EXPERTS):
        @pl.when(full[e][0])
        def _():
            for copy in flush_copies(e, full[e][1]):
                copy.start()

    for e in range(N_EXPERTS):
        @pl.when(full[e][0])
        def _():
            for copy in flush_copies(e, full[e][1]):
                copy.wait()
            x_stage[e, 0:STAGE - FT, :] = x_stage[e, FT:STAGE, :]
            x_stage[e, STAGE - FT:STAGE, :] = jnp.zeros((FT, D), bf16)
            g_stage[e, 0:STAGE - FT, :] = g_stage[e, FT:STAGE, :]
            g_stage[e, STAGE - FT:STAGE, :] = jnp.zeros((FT, 128), f32)

    @pl.when(i == n_rt - 1)
    def _():
        for e in range(N_EXPERTS):
            c = cnt_sc[e]
            cnt_ref[e] = c
            for copy in flush_copies(e, c - c % FT):
                copy.start()
                copy.wait()


def _moe_route(h, mod, g, router, n_rt):
    return pl.pallas_call(
        functools.partial(_route_kernel, n_rt=n_rt),
        grid=(n_rt,),
        in_specs=[
            pl.BlockSpec((RT, D), lambda i: (i, 0)),
            pl.BlockSpec((None, 8, D), lambda i: (jnp.minimum(i // RT_PER_SAMPLE, CTX_MOD_ROW), 0, 0)),
            _const_spec((1, D)), _const_spec((D, 128)),
        ],
        out_specs=[
            pl.BlockSpec(memory_space=pl.ANY), pl.BlockSpec(memory_space=pl.ANY),
            pl.BlockSpec((None, N_EXPERTS, RT), lambda i: (i, 0, 0)),
            pl.BlockSpec(memory_space=pltpu.SMEM), pl.BlockSpec(memory_space=pltpu.SMEM),
        ],
        out_shape=[
            jax.ShapeDtypeStruct((N_EXPERTS * CAP, D), bf16),
            jax.ShapeDtypeStruct((N_EXPERTS * CAP, 128), f32),
            jax.ShapeDtypeStruct((n_rt, N_EXPERTS, RT), f32),
            jax.ShapeDtypeStruct((n_rt * N_EXPERTS,), jnp.int32),
            jax.ShapeDtypeStruct((N_EXPERTS,), jnp.int32),
        ],
        scratch_shapes=[
            pltpu.VMEM((N_EXPERTS, STAGE, D), bf16), pltpu.VMEM((N_EXPERTS, STAGE, 128), f32),
            pltpu.SMEM((N_EXPERTS,), jnp.int32), pltpu.SemaphoreType.DMA((N_EXPERTS, 2)),
        ],
        compiler_params=_params(("arbitrary",)),
        name="moe_route",
    )(h, mod, g, router)


def _moe_ffn_kernel(te_ref, tb_ref, nt_ref, x_ref, gs_ref, wg_ref, wu_ref, wd_ref, y_ref, acc_sc):
    j = pl.program_id(0)
    f = pl.program_id(1)

    @pl.when(j < nt_ref[0])
    def _():
        @pl.when(f == 0)
        def _():
            acc_sc[...] = jnp.zeros_like(acc_sc)

        x = x_ref[...]
        gate = jnp.dot(x, wg_ref[...], preferred_element_type=f32)
        up = jnp.dot(x, wu_ref[...], preferred_element_type=f32)
        hid = (_silu(gate) * up).astype(bf16)
        acc_sc[...] += jnp.dot(hid, wd_ref[...], preferred_element_type=f32)

        @pl.when(f == NF - 1)
        def _():
            lane = lax.broadcasted_iota(jnp.int32, (FT, 128), 1)
            ge = jnp.sum(jnp.where(lane == te_ref[j], gs_ref[...], 0.0), axis=1, keepdims=True)
            y_ref[...] = (ge * acc_sc[...]).astype(bf16)


def _moe_experts(xs, gs, wg, wu, wd, layer, tile_expert, tile_block, n_tiles):
    max_tiles = tile_expert.shape[0]

    def f_eff(j, f, nt):
        return jnp.where(j < nt[0], f, NF - 1)

    grid_spec = pltpu.PrefetchScalarGridSpec(
        num_scalar_prefetch=3,
        grid=(max_tiles, NF),
        in_specs=[
            pl.BlockSpec((FT, D), lambda j, f, te, tb, nt: (tb[j], 0)),
            pl.BlockSpec((FT, 128), lambda j, f, te, tb, nt: (tb[j], 0)),
            pl.BlockSpec((None, None, D, TF), lambda j, f, te, tb, nt: (layer, te[j], 0, f_eff(j, f, nt))),
            pl.BlockSpec((None, None, D, TF), lambda j, f, te, tb, nt: (layer, te[j], 0, f_eff(j, f, nt))),
            pl.BlockSpec((None, None, TF, D), lambda j, f, te, tb, nt: (layer, te[j], f_eff(j, f, nt), 0)),
        ],
        out_specs=pl.BlockSpec((FT, D), lambda j, f, te, tb, nt: (tb[j], 0)),
        scratch_shapes=[pltpu.VMEM((FT, D), f32)],
    )
    return pl.pallas_call(
        _moe_ffn_kernel,
        grid_spec=grid_spec,
        out_shape=jax.ShapeDtypeStruct((N_EXPERTS * CAP, D), bf16),
        compiler_params=_params(("arbitrary", "arbitrary")),
        name="moe_experts",
    )(tile_expert, tile_block, n_tiles, xs, gs, wg, wu, wd)


def _moe_combine_kernel(seg_ref, nseg_ref, wmax_ref, rank_ref, h_ref, mod_ref, fg_ref, ys_hbm, o_ref, ybuf, acc_sc,
                        sems, *, n_rt, apply_final):
    i = pl.program_id(0)
    main_row = lax.broadcasted_iota(jnp.int32, (RT, RT), 0).astype(f32)
    last_row = lax.broadcasted_iota(jnp.int32, (ALIGN, RT), 0).astype(f32) + RT

    def window(step, e):
        seg = seg_ref[step * N_EXPERTS + e]
        start = pl.multiple_of(jnp.minimum((seg // ALIGN) * ALIGN, wmax_ref[e]), ALIGN)
        return seg, start

    def window_copy(step, e):
        _, start = window(step, e)
        return pltpu.make_async_copy(ys_hbm.at[pl.ds(e * CAP + start, WIN), :], ybuf.at[step % 2, e],
                                     sems.at[step % 2, e])

    def fetch(step):
        for e in range(N_EXPERTS):
            @pl.when(nseg_ref[step * N_EXPERTS + e] > 0)
            def _():
                window_copy(step, e).start()

    @pl.when(i == 0)
    def _():
        ybuf[...] = jnp.zeros_like(ybuf)
        fetch(0)

    @pl.when(i + 1 < n_rt)
    def _():
        fetch(i + 1)

    acc = None
    spill = False
    for e in range(N_EXPERTS):
        seg, start = window(i, e)
        n = nseg_ref[i * N_EXPERTS + e]

        @pl.when(n > 0)
        def _():
            window_copy(i, e).wait()

        target = rank_ref[e:e + 1, :] + (seg - start).astype(f32)
        p = jnp.where(main_row == target, 1.0, 0.0).astype(bf16)
        part = lax.dot_general(p, ybuf[i % 2, e, 0:RT, :], DN_T, preferred_element_type=f32)
        acc = part if acc is None else acc + part
        spill = spill | (seg - start + n > RT)
    acc_sc[...] = acc

    @pl.when(spill)
    def _():
        for e in range(N_EXPERTS):
            seg, start = window(i, e)
            target = rank_ref[e:e + 1, :] + (seg - start).astype(f32)
            p = jnp.where(last_row == target, 1.0, 0.0).astype(bf16)
            acc_sc[...] += lax.dot_general(p, ybuf[i % 2, e, RT:WIN, :], DN_T, preferred_element_type=f32)

    out = h_ref[...] + mod_ref[5:6, :] * acc_sc[...]
    if apply_final:
        out = out * lax.rsqrt(jnp.mean(out * out, axis=-1, keepdims=True) + EPS) * fg_ref[...]
    o_ref[...] = out


def _moe_combine(ys, rank, seg, nseg, wmax, h, mod, final_g, n_rt, apply_final):
    grid_spec = pltpu.PrefetchScalarGridSpec(
        num_scalar_prefetch=3,
        grid=(n_rt,),
        in_specs=[
            pl.BlockSpec((None, N_EXPERTS, RT), lambda i, *_: (i, 0, 0)),
            pl.BlockSpec((RT, D), lambda i, *_: (i, 0)),
            pl.BlockSpec((None, 8, D), lambda i, *_: (jnp.minimum(i // RT_PER_SAMPLE, CTX_MOD_ROW), 0, 0)),
            pl.BlockSpec((1, D), lambda i, *_: (0, 0)),
            pl.BlockSpec(memory_space=pl.ANY),
        ],
        out_specs=pl.BlockSpec((RT, D), lambda i, *_: (i, 0)),
        scratch_shapes=[pltpu.VMEM((2, N_EXPERTS, WIN, D), bf16), pltpu.VMEM((RT, D), f32),
                        pltpu.SemaphoreType.DMA((2, N_EXPERTS))],
    )
    return pl.pallas_call(
        functools.partial(_moe_combine_kernel, n_rt=n_rt, apply_final=apply_final),
        grid_spec=grid_spec,
        out_shape=jax.ShapeDtypeStruct((n_rt * RT, D), f32),
        compiler_params=_params(("arbitrary",)),
        name="moe_combine",
    )(seg, nseg, wmax, rank, h, mod, final_g, ys)


def _moe(h, mod, g, router, wg, wu, wd, layer, nt, final_g, apply_final):
    n_rt = nt * (TM // RT)
    xs, gs, rank, seg, counts = _moe_route(h, mod, g, router, n_rt)
    tiles = (counts + FT - 1) // FT
    ends = jnp.cumsum(tiles)
    n_tiles = ends[-1]
    max_tiles = (2 * n_rt * RT) // FT + N_EXPERTS
    j = jnp.minimum(jnp.arange(max_tiles, dtype=jnp.int32), n_tiles - 1)
    tile_expert = jnp.sum((j[:, None] >= ends[None, :]).astype(jnp.int32), axis=1)
    tile_block = tile_expert * CAP_BLOCKS + j - (ends - tiles)[tile_expert]
    ys = _moe_experts(xs, gs, wg, wu, wd, layer, tile_expert, tile_block, n_tiles.reshape(1))
    seg2 = seg.reshape(n_rt, N_EXPERTS)
    nseg = (jnp.concatenate([seg2[1:], counts[None, :]], axis=0) - seg2).reshape(-1)
    wmax = jnp.maximum(tiles * FT - WIN, 0)
    return _moe_combine(ys, rank, seg, nseg, wmax, h, mod, final_g, n_rt, apply_final)


def _ssd_in_kernel(h_ref, mod_ref, g_ref, wz_ref, wx_ref, wdt_ref, z_ref, x_ref, dt_ref):
    a = _norm_mod(h_ref[...], g_ref[...], mod_ref[0:1, :], mod_ref[1:2, :]).astype(bf16)
    for j in range(SSD_INNER // D):
        z_ref[:, j * D:(j + 1) * D] = jnp.dot(a, wz_ref[:, j * D:(j + 1) * D], preferred_element_type=f32).astype(bf16)
    for j in range(SSD_CONV_CH // D):
        x_ref[:, j * D:(j + 1) * D] = jnp.dot(a, wx_ref[:, j * D:(j + 1) * D], preferred_element_type=f32).astype(bf16)
    dt_ref[...] = jnp.dot(a, wdt_ref[...], preferred_element_type=f32)


def _ssd_in_proj(h, mod, g, wz, wx, wdt):
    return pl.pallas_call(
        _ssd_in_kernel,
        grid=(NT,),
        in_specs=[_tile_spec(D), _mod_spec(), _const_spec((1, D)), _const_spec((D, SSD_INNER)),
                  _const_spec((D, SSD_CONV_CH)), _const_spec((D, 256))],
        out_specs=[_tile_spec(SSD_INNER), _tile_spec(SSD_CONV_CH), _tile_spec(256)],
        out_shape=[jax.ShapeDtypeStruct((T, SSD_INNER), bf16), jax.ShapeDtypeStruct((T, SSD_CONV_CH), bf16),
                   jax.ShapeDtypeStruct((T, 256), f32)],
        compiler_params=_params(("parallel",)),
        name="ssd_in_proj",
    )(h, mod, g, wz, wx, wdt)


CONV_TM = 256
CONV_HALO = 16
CONV_TILES_PER_SAMPLE = SEQ // CONV_TM


def _conv_kernel(x_ref, prev_ref, next_ref, w_ref, b_ref, o_ref):
    i = pl.program_id(0)
    is_ctx = i >= N_LAT // CONV_TM
    seg_start = is_ctx | (i % CONV_TILES_PER_SAMPLE == 0)
    seg_end = is_ctx | (i % CONV_TILES_PER_SAMPLE == CONV_TILES_PER_SAMPLE - 1)
    x = x_ref[...].astype(f32)
    prev_row = jnp.where(seg_start, 0.0, prev_ref[CONV_HALO - 1:CONV_HALO, :].astype(f32))
    next_row = jnp.where(seg_end, 0.0, next_ref[0:1, :].astype(f32))
    row = lax.broadcasted_iota(jnp.int32, x.shape, 0)
    xm1 = jnp.where(row == 0, prev_row, pltpu.roll(x, 1, 0))
    xp1 = jnp.where(row == CONV_TM - 1, next_row, pltpu.roll(x, CONV_TM - 1, 0))
    y = w_ref[0:1, :] * xm1 + w_ref[1:2, :] * x + w_ref[2:3, :] * xp1 + b_ref[...]
    o_ref[...] = _silu(y).astype(bf16)


def _ssd_conv(xbc, conv_w, conv_b):
    n = T // CONV_TM
    per = CONV_TM // CONV_HALO
    last = T // CONV_HALO - 1
    return pl.pallas_call(
        _conv_kernel,
        grid=(n,),
        in_specs=[
            pl.BlockSpec((CONV_TM, SSD_CONV_CH), lambda i: (i, 0)),
            pl.BlockSpec((CONV_HALO, SSD_CONV_CH), lambda i: (jnp.maximum(i * per - 1, 0), 0)),
            pl.BlockSpec((CONV_HALO, SSD_CONV_CH), lambda i: (jnp.minimum((i + 1) * per, last), 0)),
            _const_spec((8, SSD_CONV_CH)), _const_spec((1, SSD_CONV_CH)),
        ],
        out_specs=pl.BlockSpec((CONV_TM, SSD_CONV_CH), lambda i: (i, 0)),
        out_shape=jax.ShapeDtypeStruct((T, SSD_CONV_CH), bf16),
        compiler_params=_params(("parallel",)),
        name="ssd_conv",
    )(xbc, xbc, xbc, conv_w, conv_b)


def _ssd_scan_kernel(x_ref, b_ref, c_ref, dt_ref, bias_ref, alog_ref, y_ref, state_sc):
    d = pl.program_id(0)
    c = pl.program_id(2)

    @pl.when(c == 0)
    def _():
        state_sc[...] = jnp.zeros_like(state_sc)

    L = SSD_CHUNK
    li = lax.broadcasted_iota(jnp.int32, (L, L), 0)
    si = lax.broadcasted_iota(jnp.int32, (L, L), 1)
    causal = (si - li) * (1 - 2 * d) <= 0
    tri = jnp.where(causal, 1.0, 0.0).astype(bf16)
    hi = lax.broadcasted_iota(jnp.int32, (128, SSD_INNER), 0)
    ci = lax.broadcasted_iota(jnp.int32, (128, SSD_INNER), 1)
    expand = jnp.where(ci // SSD_P == hi, 1.0, 0.0).astype(bf16)

    dt = jax.nn.softplus(dt_ref[...] + bias_ref[...])
    a_neg = -jnp.exp(alog_ref[...])
    da = dt * a_neg
    da_parts = _split3(da)
    cs = (jnp.dot(tri, da_parts[0], preferred_element_type=f32) + jnp.dot(tri, da_parts[1], preferred_element_type=f32)
          + jnp.dot(tri, da_parts[2], preferred_element_type=f32))
    dn_t = (((0,), (1,)), ((), ()))
    cs_t = (lax.dot_general(da_parts[0], tri, dn_t, preferred_element_type=f32)
            + lax.dot_general(da_parts[1], tri, dn_t, preferred_element_type=f32)
            + lax.dot_general(da_parts[2], tri, dn_t, preferred_element_type=f32))
    total = jnp.sum(da, axis=0, keepdims=True)

    e_out = jnp.exp(cs)
    e_in = jnp.exp(total - cs) * dt
    dt_x = jnp.dot(dt.astype(bf16), expand, preferred_element_type=f32).astype(bf16)
    e_in_x = jnp.dot(e_in.astype(bf16), expand, preferred_element_type=f32).astype(bf16)
    out_scale = _dot3(_split_bf16(e_out), expand)
    chunk_decay = _dot3(_split_bf16(jnp.broadcast_to(jnp.exp(total), (8, 128))), expand)[0:1]

    x = x_ref[...]
    xdt = x * dt_x
    xw = x * e_in_x

    for g in range(SSD_GROUPS):
        bg = b_ref[:, g * SSD_N:(g + 1) * SSD_N]
        cg = c_ref[:, g * SSD_N:(g + 1) * SSD_N]
        cb = lax.dot_general(cg, bg, (((1,), (1,)), ((), ())), preferred_element_type=f32)
        gsl = slice(g * SSD_HPG * SSD_P, (g + 1) * SSD_HPG * SSD_P)
        st = state_sc[g]
        y_off = jnp.dot(cg, st.astype(bf16), preferred_element_type=f32) * out_scale[:, gsl]
        ys = []
        for r in range(SSD_HPG):
            hd = g * SSD_HPG + r
            seg = cs[:, hd:hd + 1] - cs_t[hd:hd + 1, :]
            decay = jnp.exp(jnp.where(causal, seg, -jnp.inf))
            m = (cb * decay).astype(bf16)
            ys.append(jnp.dot(m, xdt[:, hd * SSD_P:(hd + 1) * SSD_P], preferred_element_type=f32))
        y_ref[:, gsl] = (jnp.concatenate(ys, axis=1) + y_off).astype(bf16)
        new = lax.dot_general(bg, xw[:, gsl], (((0,), (0,)), ((), ())), preferred_element_type=f32)
        state_sc[g] = st * chunk_decay[:, gsl] + new


SSD_NCHUNK = (SEQ + CTX) // SSD_CHUNK
SSD_CTX_CHUNKS = CTX // SSD_CHUNK
SSD_LAT_CHUNKS = SEQ // SSD_CHUNK


def _ssd_chunk_block(d, b, c):
    ctx_j = jnp.where(d == 0, c, SSD_CTX_CHUNKS - 1 - c)
    lat_j = jnp.where(d == 0, c - SSD_CTX_CHUNKS, SSD_LAT_CHUNKS - 1 - (c - SSD_CTX_CHUNKS))
    return jnp.where(c < SSD_CTX_CHUNKS, N_LAT // SSD_CHUNK + b * SSD_CTX_CHUNKS + ctx_j, b * SSD_LAT_CHUNKS + lat_j)


def _ssd_scan(xbc, dt, dt_bias, a_log):
    x_blk = SSD_INNER // SSD_BC
    return pl.pallas_call(
        _ssd_scan_kernel,
        grid=(2, BATCH, SSD_NCHUNK),
        in_specs=[
            pl.BlockSpec((SSD_CHUNK, SSD_INNER), lambda d, b, c: (_ssd_chunk_block(d, b, c), 0)),
            pl.BlockSpec((SSD_CHUNK, SSD_BC), lambda d, b, c: (_ssd_chunk_block(d, b, c), x_blk)),
            pl.BlockSpec((SSD_CHUNK, SSD_BC), lambda d, b, c: (_ssd_chunk_block(d, b, c), x_blk + 1)),
            pl.BlockSpec((SSD_CHUNK, 128), lambda d, b, c: (_ssd_chunk_block(d, b, c), d)),
            pl.BlockSpec((None, 1, 128), lambda d, b, c: (d, 0, 0)),
            pl.BlockSpec((None, 1, 128), lambda d, b, c: (d, 0, 0)),
        ],
        out_specs=pl.BlockSpec((None, SSD_CHUNK, SSD_INNER), lambda d, b, c: (d, _ssd_chunk_block(d, b, c), 0)),
        out_shape=jax.ShapeDtypeStruct((2, T, SSD_INNER), bf16),
        scratch_shapes=[pltpu.VMEM((SSD_GROUPS, SSD_N, SSD_HPG * SSD_P), f32)],
        compiler_params=_params(("parallel", "parallel", "arbitrary")),
        name="ssd_scan",
    )(xbc, xbc, xbc, dt, dt_bias, a_log)


def _ssd_out_kernel(yf_ref, yb_ref, x_ref, z_ref, dskip_ref, ng_ref, w_ref, h_ref, mod_ref, o_ref):
    y = yf_ref[...].astype(f32) + yb_ref[...].astype(f32) + x_ref[...].astype(f32) * dskip_ref[...]
    gated = y * _silu(z_ref[...].astype(f32))
    gw = SSD_INNER // SSD_GROUPS
    parts = []
    for g in range(SSD_GROUPS):
        s = gated[:, g * gw:(g + 1) * gw]
        parts.append(s * lax.rsqrt(jnp.mean(s * s, axis=-1, keepdims=True) + EPS))
    normed = (jnp.concatenate(parts, axis=1) * ng_ref[...]).astype(bf16)
    out = jnp.dot(normed, w_ref[...], preferred_element_type=f32)
    o_ref[...] = h_ref[...] + mod_ref[2:3, :] * out


def _ssd_out(y2, xbc, z, dskip, ng, w_out, h, mod, nt):
    return pl.pallas_call(
        _ssd_out_kernel,
        grid=(nt,),
        in_specs=[
            pl.BlockSpec((None, TM, SSD_INNER), lambda t: (0, t, 0)),
            pl.BlockSpec((None, TM, SSD_INNER), lambda t: (1, t, 0)),
            pl.BlockSpec((TM, SSD_INNER), lambda t: (t, 0)),
            _tile_spec(SSD_INNER), _const_spec((1, SSD_INNER)), _const_spec((1, SSD_INNER)),
            _const_spec((SSD_INNER, D)), _tile_spec(D), _mod_spec(),
        ],
        out_specs=_tile_spec(D),
        out_shape=jax.ShapeDtypeStruct((nt * TM, D), f32),
        compiler_params=_params(("parallel",)),
        name="ssd_out",
    )(y2, y2, xbc, z, dskip, ng, w_out, h, mod)


def _gelu(x):
    return 0.5 * x * (1.0 + lax.erf(x * math.sqrt(0.5)))


def _cmlp_kernel(h_ref, mod_ref, g_ref, wu_ref, wv_ref, bu_ref, bv_ref, vg_ref, ws_ref, bs_ref, wo_ref, o_ref, uv_sc):
    a = _norm_mod(h_ref[...], g_ref[...], mod_ref[0:1, :], mod_ref[1:2, :]).astype(bf16)
    v = _gelu(jnp.dot(a, wv_ref[...], preferred_element_type=f32) + bv_ref[...])
    v = (v * lax.rsqrt(jnp.mean(v * v, axis=-1, keepdims=True) + EPS) * vg_ref[...]).astype(bf16)
    u = _gelu(jnp.dot(a, wu_ref[...], preferred_element_type=f32) + bu_ref[...])
    for ck in range(TM // CMLP_CHUNK):
        rows = slice(ck * CMLP_CHUNK, (ck + 1) * CMLP_CHUNK)
        for g in range(CMLP_GROUPS):
            cols = slice(g * CMLP_GW, (g + 1) * CMLP_GW)
            mixed = jnp.dot(ws_ref[g], v[rows, cols], preferred_element_type=f32) + bs_ref[:, cols]
            uv_sc[rows, cols] = (u[rows, cols] * mixed).astype(bf16)
    out = jnp.dot(uv_sc[...], wo_ref[...], preferred_element_type=f32)
    o_ref[...] = h_ref[...] + mod_ref[2:3, :] * out


def _cmlp(h, mod, g, wu, wv, bu, bv, vg, ws, bs, wo, nt):
    return pl.pallas_call(
        _cmlp_kernel,
        grid=(nt,),
        in_specs=[
            _tile_spec(D), _mod_spec(), _const_spec((1, D)), _const_spec((D, CMLP_D)), _const_spec((D, CMLP_D)),
            _const_spec((1, CMLP_D)), _const_spec((1, CMLP_D)), _const_spec((1, CMLP_D)),
            _const_spec((CMLP_GROUPS, CMLP_CHUNK, CMLP_CHUNK)), _const_spec((CMLP_CHUNK, CMLP_D)),
            _const_spec((CMLP_D, D)),
        ],
        out_specs=_tile_spec(D),
        out_shape=jax.ShapeDtypeStruct((nt * TM, D), f32),
        scratch_shapes=[pltpu.VMEM((TM, CMLP_D), bf16)],
        compiler_params=_params(("parallel",)),
        name="cmlp",
    )(h, mod, g, wu, wv, bu, bv, vg, ws, bs, wo)


def _rope_tables():
    pos = jnp.arange(SEQ)
    inv_freq = 1.0 / (ROPE_THETA ** (jnp.arange(0, ROPE_AXIS_DIM, 2, dtype=f32) / ROPE_AXIS_DIM))
    ang_r = (pos // GRID_W).astype(f32)[:, None] * inv_freq
    ang_c = (pos % GRID_W).astype(f32)[:, None] * inv_freq
    cos = jnp.concatenate([jnp.cos(ang_r)] * 2 + [jnp.cos(ang_c)] * 2, axis=1)
    sin = jnp.concatenate([-jnp.sin(ang_r), jnp.sin(ang_r), -jnp.sin(ang_c), jnp.sin(ang_c)], axis=1)
    cos = jnp.concatenate([cos, jnp.ones((TM, HEAD_DIM), f32)], axis=0)
    sin = jnp.concatenate([sin, jnp.zeros((TM, HEAD_DIM), f32)], axis=0)
    return cos, sin


def _row(v):
    return v.reshape(1, -1)


def _pad_lanes(v, width=128):
    return jnp.pad(v, ((0, 0), (0, width - v.shape[1])))


def kernel(x, c, ctx, c_ctx, w_mod, b_mod, norm1_g, norm2_g, attn_w_qkv, attn_q_g, attn_k_g, attn_w_o, ssd_w_in, ssd_conv_w, ssd_conv_b, ssd_dt_bias_f, ssd_dt_bias_b, ssd_a_log_f, ssd_a_log_b, ssd_d_skip, ssd_norm_g, ssd_w_out, cmlp_w_in, cmlp_b_in, cmlp_v_g, cmlp_w_s, cmlp_b_s, cmlp_w_out, ffn_w_gate, ffn_w_up, ffn_w_down, moe_router, moe_w_gate, moe_w_up, moe_w_down, final_g):
    h = jnp.concatenate([x.reshape(N_LAT, D), ctx.reshape(N_CTX, D)], axis=0)
    cond = jnp.concatenate([c, c_ctx[None, :], jnp.zeros((8 - BATCH - 1, D), f32)], axis=0)
    mods = _modulation(cond, w_mod, b_mod)
    cos_t, sin_t = _rope_tables()
    ffn_w = [w.astype(bf16) for w in (ffn_w_gate, ffn_w_up, ffn_w_down)]
    moe_w = [w.astype(bf16) for w in (moe_w_gate, moe_w_up, moe_w_down)]
    final_row = _row(final_g)

    for i in range(DEPTH):
        need_ctx = i < DEPTH - 1
        nt = NT if need_ctx else NT_LAT
        mod = mods[i]
        kind, j = i % 3, i // 3
        g1 = _row(norm1_g[i])
        if kind == 0:
            q, k, v = _qkv_proj(h, mod, g1, attn_w_qkv[j].astype(bf16), _row(attn_q_g[j]), _row(attn_k_g[j]), cos_t, sin_t)
            o = _attention(q, k, v)
            w_o = attn_w_o[j].astype(bf16)
            h = _proj_res2(o, _attention_ctx(q, k, v), w_o, h, mod) if need_ctx else _proj_res(o, w_o, h, mod, nt)
        elif kind == 1:
            w_in = ssd_w_in[j]
            wz = w_in[:, :SSD_INNER].astype(bf16)
            wx = w_in[:, SSD_INNER:SSD_INNER + SSD_CONV_CH].astype(bf16)
            w_dt = w_in[:, SSD_INNER + SSD_CONV_CH:]
            wdt = jnp.concatenate([_pad_lanes(w_dt[:, :SSD_HEADS]), _pad_lanes(w_dt[:, SSD_HEADS:])], axis=1).astype(bf16)
            z, xbc, dt = _ssd_in_proj(h, mod, g1, wz, wx, wdt)
            conv_w = jnp.pad(ssd_conv_w[j], ((0, 8 - ssd_conv_w.shape[1]), (0, 0)))
            xbc = _ssd_conv(xbc, conv_w, _row(ssd_conv_b[j]))
            dt_bias = jnp.stack([_pad_lanes(_row(ssd_dt_bias_f[j])), _pad_lanes(_row(ssd_dt_bias_b[j]))])
            a_log = jnp.stack([_pad_lanes(_row(ssd_a_log_f[j])), _pad_lanes(_row(ssd_a_log_b[j]))])
            y2 = _ssd_scan(xbc, dt, dt_bias, a_log)
            dskip = _row(jnp.repeat(ssd_d_skip[j], SSD_P))
            h = _ssd_out(y2, xbc, z, dskip, _row(ssd_norm_g[j]), ssd_w_out[j].astype(bf16), h, mod, nt)
        else:
            w_in = cmlp_w_in[j]
            b_in = cmlp_b_in[j]
            bs = jnp.repeat(cmlp_b_s[j].T, CMLP_GW, axis=1)
            h = _cmlp(h, mod, g1, w_in[:, :CMLP_D].astype(bf16), w_in[:, CMLP_D:].astype(bf16),
                      _row(b_in[:CMLP_D]), _row(b_in[CMLP_D:]), _row(cmlp_v_g[j]), cmlp_w_s[j].astype(bf16), bs,
                      cmlp_w_out[j].astype(bf16), nt)
        kk = i // 2
        g2 = _row(norm2_g[i])
        if i % 2 == 0:
            h = _ffn(h, mod, g2, *ffn_w, kk, nt)
        else:
            h = _moe(h, mod, g2, _pad_lanes(moe_router[kk]), *moe_w, kk, nt, final_row, i == DEPTH - 1)
    return h.reshape(BATCH, SEQ, D)
```

```python
import functools
import math

import jax
import jax.numpy as jnp
import numpy as np
from jax import lax
from jax.experimental import pallas as pl
from jax.experimental.pallas import tpu as pltpu

f32 = jnp.float32
bf16 = jnp.bfloat16

D = 1024
BATCH = 2
SEQ = 8192
CTX = 256
DEPTH = 4
GRID_W = 64
EPS = 1e-6
N_MOD = 6

HEAD_DIM = 128
N_HEADS = 8
N_KV = 2
Q_GROUP = 4
ROPE_AXIS_DIM = 64
ROPE_THETA = 10000.0

SSD_INNER = 2048
SSD_P = 64
SSD_HEADS = 32
SSD_GROUPS = 4
SSD_HPG = 8
SSD_N = 128
SSD_CHUNK = 128
SSD_BC = SSD_GROUPS * SSD_N
SSD_CONV_CH = SSD_INNER + 2 * SSD_BC

CMLP_D = 2048
CMLP_GROUPS = 8
CMLP_GW = 256
CMLP_CHUNK = 128

D_FF = 3584
N_EXPERTS = 8

N_LAT = BATCH * SEQ
N_CTX = BATCH * CTX
T = N_LAT + N_CTX
TM = 512
NT = T // TM
NT_LAT = N_LAT // TM
TILES_PER_SAMPLE = SEQ // TM
CTX_MOD_ROW = BATCH

TF = 1792
NF = D_FF // TF

TQ = 512
TK = 512
ATT_CHUNKS = SEQ // TK
VW = 2 * HEAD_DIM
LOG2E = math.log2(math.e)
NQ_LAT = SEQ // TQ

VMEM_LIMIT = 56 * 1024 * 1024


def _mod_row(t):
    return jnp.minimum(t // TILES_PER_SAMPLE, CTX_MOD_ROW)


def _tile_spec(width):
    return pl.BlockSpec((TM, width), lambda t: (t, 0))


def _mod_spec():
    return pl.BlockSpec((None, 8, D), lambda t: (_mod_row(t), 0, 0))


def _const_spec(shape):
    n = len(shape)
    return pl.BlockSpec(shape, lambda *_: (0,) * n)


def _params(semantics):
    return pltpu.CompilerParams(dimension_semantics=semantics, vmem_limit_bytes=VMEM_LIMIT)


def _silu(x):
    return x * jax.nn.sigmoid(x)


def _norm_mod(x, g, shift, scale):
    y = x * lax.rsqrt(jnp.mean(x * x, axis=-1, keepdims=True) + EPS) * g
    return y * (1.0 + scale) + shift


def _mod_kernel(c_ref, w_ref, b_ref, o_ref):
    s = _silu(c_ref[...])
    o_ref[...] = jnp.dot(s.astype(bf16), w_ref[...].astype(bf16), preferred_element_type=f32) + b_ref[...]


def _modulation(cond, w_mod, b_mod):
    nblk = N_MOD * D // D
    out = pl.pallas_call(
        _mod_kernel,
        grid=(DEPTH, nblk),
        in_specs=[
            pl.BlockSpec((8, D), lambda i, j: (0, 0)),
            pl.BlockSpec((None, D, D), lambda i, j: (i, 0, j)),
            pl.BlockSpec((None, 1, D), lambda i, j: (i, 0, j)),
        ],
        out_specs=pl.BlockSpec((None, 8, D), lambda i, j: (i, 0, j)),
        out_shape=jax.ShapeDtypeStruct((DEPTH, 8, N_MOD * D), f32),
        compiler_params=_params(("arbitrary", "arbitrary")),
        name="modulation",
    )(cond, w_mod, b_mod.reshape(DEPTH, 1, N_MOD * D))
    mod = out[:, :3].reshape(DEPTH, 3, N_MOD, D)
    return jnp.pad(mod, ((0, 0), (0, 0), (0, 8 - N_MOD), (0, 0)))


def _qkv_kernel(h_ref, mod_ref, g_ref, w_ref, wp_ref, qg_ref, kg_ref, cos_ref, sin_ref, q_ref, k_ref, v_ref):
    a = _norm_mod(h_ref[...], g_ref[...], mod_ref[0:1, :], mod_ref[1:2, :]).astype(bf16)
    qkv = jnp.dot(a, w_ref[...], preferred_element_type=f32)
    qkp = jnp.dot(a, wp_ref[...], preferred_element_type=f32)
    cos = cos_ref[...]
    sin = sin_ref[...]

    def head(x, xp, g, scale):
        r = lax.rsqrt(jnp.mean(x * x, axis=-1, keepdims=True) + EPS) * scale
        return (x * g[0:1, :] * cos + xp * g[1:2, :] * sin) * r

    qg = qg_ref[...]
    kg = kg_ref[...]
    for hh in range(N_HEADS):
        sl = slice(hh * HEAD_DIM, (hh + 1) * HEAD_DIM)
        q_ref[:, sl] = head(qkv[:, sl], qkp[:, sl], qg, HEAD_DIM ** -0.5 * LOG2E).astype(bf16)
    for hh in range(N_KV):
        sl = slice(hh * HEAD_DIM, (hh + 1) * HEAD_DIM)
        src = slice((N_HEADS + hh) * HEAD_DIM, (N_HEADS + hh + 1) * HEAD_DIM)
        k_ref[:, sl] = head(qkv[:, src], qkp[:, src], kg, 1.0).astype(bf16)
        v_ref[:, hh * VW:hh * VW + HEAD_DIM] = qkv[:, (N_HEADS + N_KV + hh) * HEAD_DIM:(N_HEADS + N_KV + hh + 1) * HEAD_DIM].astype(bf16)
        v_ref[:, hh * VW + HEAD_DIM:(hh + 1) * VW] = jnp.ones((TM, HEAD_DIM), bf16)


def _rope_partner(w):
    lead = w.shape[:-1]
    w5 = w.reshape(lead + (-1, 2, 2, ROPE_AXIS_DIM // 2))
    return w5[..., ::-1, :].reshape(w.shape)


def _qkv_proj(h, mod, g, w_qkv, w_partner, q_g, k_g, cos_t, sin_t):
    rope_spec = pl.BlockSpec((TM, HEAD_DIM), lambda t: (jnp.where(t < NT_LAT, t % TILES_PER_SAMPLE, TILES_PER_SAMPLE), 0))
    return pl.pallas_call(
        _qkv_kernel,
        grid=(NT,),
        in_specs=[
            _tile_spec(D), _mod_spec(), _const_spec((1, D)), _const_spec((D, (N_HEADS + 2 * N_KV) * HEAD_DIM)),
            _const_spec((D, (N_HEADS + N_KV) * HEAD_DIM)),
            _const_spec((2, HEAD_DIM)), _const_spec((2, HEAD_DIM)), rope_spec, rope_spec,
        ],
        out_specs=[_tile_spec(N_HEADS * HEAD_DIM), _tile_spec(N_KV * HEAD_DIM), _tile_spec(N_KV * VW)],
        out_shape=[
            jax.ShapeDtypeStruct((T, N_HEADS * HEAD_DIM), bf16),
            jax.ShapeDtypeStruct((T, N_KV * HEAD_DIM), bf16),
            jax.ShapeDtypeStruct((T, N_KV * VW), bf16),
        ],
        compiler_params=_params(("parallel",)),
        name="attn_qkv",
    )(h, mod, g, w_qkv, w_partner, q_g, k_g, cos_t, sin_t)


def _stack_heads(q):
    return jnp.concatenate([q[:, i * HEAD_DIM:(i + 1) * HEAD_DIM] for i in range(Q_GROUP)], axis=0)


def _scores(qs, k):
    return lax.dot_general(qs, k, (((1,), (1,)), ((), ())), preferred_element_type=f32)


def _attn_kernel(q_ref, kl_ref, vl_ref, kc_ref, vc_ref, wg_ref, wu_ref, wd_ref, o_ref, wg_o, wu_o, wd_o,
                 m_sc, acc_sc, s_a, s_b, s_c):
    qs = _stack_heads(q_ref[...])

    wg_o[...] = wg_ref[...].astype(bf16)
    wu_o[...] = wu_ref[...].astype(bf16)
    wd_o[...] = wd_ref[...].astype(bf16)

    def absorb(s, v):
        m_prev = m_sc[...]
        m_next = jnp.maximum(m_prev, jnp.max(s, axis=1, keepdims=True))
        alpha = jnp.exp2(m_prev - m_next)
        p = jnp.exp2(s - jnp.concatenate([m_next] * (s.shape[1] // HEAD_DIM), axis=1))
        acc_sc[...] = (jnp.concatenate([alpha, alpha], axis=1) * acc_sc[...]
                       + jnp.dot(p.astype(bf16), v, preferred_element_type=f32))
        m_sc[...] = m_next

    def chunk(ref, c):
        return ref[pl.ds(pl.multiple_of(c * TK, TK), TK), :]

    m_sc[...] = jnp.full_like(m_sc, -jnp.inf)
    acc_sc[...] = jnp.zeros_like(acc_sc)
    s_c[...] = _scores(qs, kc_ref[...])
    s_a[...] = _scores(qs, chunk(kl_ref, 0))

    def body(c2, carry):
        c = 2 * c2
        s_b[...] = _scores(qs, chunk(kl_ref, c + 1))
        absorb(s_a[...], chunk(vl_ref, c))
        s_a[...] = _scores(qs, chunk(kl_ref, c + 2))
        absorb(s_b[...], chunk(vl_ref, c + 1))
        return carry
    lax.fori_loop(0, ATT_CHUNKS // 2 - 1, body, 0)

    s_b[...] = _scores(qs, chunk(kl_ref, ATT_CHUNKS - 1))
    absorb(s_a[...], chunk(vl_ref, ATT_CHUNKS - 2))
    absorb(s_b[...], chunk(vl_ref, ATT_CHUNKS - 1))
    absorb(s_c[...], vc_ref[...])

    acc = acc_sc[...]
    o = acc[:, :HEAD_DIM] / acc[:, HEAD_DIM:]
    for i in range(Q_GROUP):
        o_ref[:, i * HEAD_DIM:(i + 1) * HEAD_DIM] = o[i * TQ:(i + 1) * TQ].astype(bf16)


def _attn_ctx_kernel(q_ref, kc_ref, vc_ref, o_ref):
    s = _scores(_stack_heads(q_ref[...]), kc_ref[...])
    p = jnp.exp2(s - jnp.max(s, axis=1, keepdims=True))
    acc = jnp.dot(p.astype(bf16), vc_ref[...], preferred_element_type=f32)
    o = acc[:, :HEAD_DIM] / acc[:, HEAD_DIM:]
    for i in range(Q_GROUP):
        o_ref[:, i * HEAD_DIM:(i + 1) * HEAD_DIM] = o[i * CTX:(i + 1) * CTX].astype(bf16)


def _attention(q, k, v, wg, wu, wd, layer):
    ctx_blk = N_LAT // CTX
    steps = BATCH * N_KV * NQ_LAT
    n_layers = wg.shape[0]
    slabs = [w.reshape(n_layers * steps, -1, w.shape[-1]) for w in (wg, wu, wd)]

    def slab_spec(w, first):
        return pl.BlockSpec((None,) + w.shape[1:],
                            lambda b, kh, qi: (first + (b * N_KV + kh) * NQ_LAT + qi, 0, 0))

    def q_map(b, kh, qi):
        return (b * NQ_LAT + qi, kh)

    def lat_spec(width):
        return pl.BlockSpec((SEQ, width), lambda b, kh, qi: (b, kh))

    def ctx_spec(width):
        return pl.BlockSpec((CTX, width), lambda b, kh, qi: (ctx_blk + b, kh))

    rows = Q_GROUP * TQ
    out = pl.pallas_call(
        _attn_kernel,
        grid=(BATCH, N_KV, NQ_LAT),
        in_specs=[pl.BlockSpec((TQ, Q_GROUP * HEAD_DIM), q_map), lat_spec(HEAD_DIM), lat_spec(VW),
                  ctx_spec(HEAD_DIM), ctx_spec(VW)] + [slab_spec(w, layer * steps) for w in slabs],
        out_specs=[pl.BlockSpec((TQ, Q_GROUP * HEAD_DIM), q_map)] + [slab_spec(w, 0) for w in slabs],
        out_shape=[jax.ShapeDtypeStruct((N_LAT, N_HEADS * HEAD_DIM), bf16)]
        + [jax.ShapeDtypeStruct((steps,) + w.shape[1:], bf16) for w in slabs],
        scratch_shapes=[pltpu.VMEM((rows, HEAD_DIM), f32), pltpu.VMEM((rows, VW), f32),
                        pltpu.VMEM((rows, TK), f32), pltpu.VMEM((rows, TK), f32), pltpu.VMEM((rows, CTX), f32)],
        compiler_params=_params(("parallel", "parallel", "arbitrary")),
        name="attn_core",
    )(q, k, v, k, v, *slabs)
    return out[0], [o.reshape(w.shape[1:]) for o, w in zip(out[1:], (wg, wu, wd))]


def _attention_ctx(q, k, v):
    ctx_blk = N_LAT // CTX
    return pl.pallas_call(
        _attn_ctx_kernel,
        grid=(BATCH, N_KV),
        in_specs=[pl.BlockSpec((CTX, Q_GROUP * HEAD_DIM), lambda b, kh: (ctx_blk + b, kh)),
                  pl.BlockSpec((CTX, HEAD_DIM), lambda b, kh: (ctx_blk + b, kh)),
                  pl.BlockSpec((CTX, VW), lambda b, kh: (ctx_blk + b, kh))],
        out_specs=pl.BlockSpec((CTX, Q_GROUP * HEAD_DIM), lambda b, kh: (b, kh)),
        out_shape=jax.ShapeDtypeStruct((N_CTX, N_HEADS * HEAD_DIM), bf16),
        compiler_params=_params(("parallel", "parallel")),
        name="attn_ctx",
    )(q, k, v)


def _proj_res_kernel(y_ref, w_ref, h_ref, mod_ref, o_ref):
    y = jnp.dot(y_ref[...], w_ref[...], preferred_element_type=f32)
    o_ref[...] = h_ref[...] + mod_ref[2:3, :] * y


def _proj_res2_kernel(yl_ref, yc_ref, w_ref, h_ref, mod_ref, o_ref):
    y_in = jnp.where(pl.program_id(0) < NT_LAT, yl_ref[...], yc_ref[...])
    y = jnp.dot(y_in, w_ref[...], preferred_element_type=f32)
    o_ref[...] = h_ref[...] + mod_ref[2:3, :] * y


def _proj_res(y, w, h, mod, nt):
    kdim = y.shape[1]
    return pl.pallas_call(
        _proj_res_kernel,
        grid=(nt,),
        in_specs=[_tile_spec(kdim), _const_spec((kdim, D)), _tile_spec(D), _mod_spec()],
        out_specs=_tile_spec(D),
        out_shape=jax.ShapeDtypeStruct((nt * TM, D), f32),
        compiler_params=_params(("parallel",)),
        name="proj_res",
    )(y, w, h, mod)


def _proj_res2(y_lat, y_ctx, w, h, mod):
    kdim = y_lat.shape[1]
    return pl.pallas_call(
        _proj_res2_kernel,
        grid=(NT,),
        in_specs=[pl.BlockSpec((TM, kdim), lambda t: (jnp.minimum(t, NT_LAT - 1), 0)), _const_spec((TM, kdim)),
                  _const_spec((kdim, D)), _tile_spec(D), _mod_spec()],
        out_specs=_tile_spec(D),
        out_shape=jax.ShapeDtypeStruct((T, D), f32),
        compiler_params=_params(("parallel",)),
        name="proj_res2",
    )(y_lat, y_ctx, w, h, mod)


def _ffn_kernel(h_ref, mod_ref, g_ref, wg_ref, wu_ref, wd_ref, o_ref, xn_sc, acc_sc):
    f = pl.program_id(1)

    @pl.when(f == 0)
    def _():
        xn_sc[...] = _norm_mod(h_ref[...], g_ref[...], mod_ref[3:4, :], mod_ref[4:5, :]).astype(bf16)
        acc_sc[...] = jnp.zeros_like(acc_sc)

    x = xn_sc[...]
    gate = jnp.dot(x, wg_ref[...], preferred_element_type=f32)
    up = jnp.dot(x, wu_ref[...], preferred_element_type=f32)
    hid = (_silu(gate) * up).astype(bf16)
    acc_sc[...] += jnp.dot(hid, wd_ref[...], preferred_element_type=f32)

    @pl.when(f == NF - 1)
    def _():
        o_ref[...] = h_ref[...] + mod_ref[5:6, :] * acc_sc[...]


def _ffn(h, mod, g, wg, wu, wd, layer, nt):
    return pl.pallas_call(
        _ffn_kernel,
        grid=(nt, NF),
        in_specs=[
            pl.BlockSpec((TM, D), lambda t, f: (t, 0)),
            pl.BlockSpec((None, 8, D), lambda t, f: (_mod_row(t), 0, 0)),
            pl.BlockSpec((1, D), lambda t, f: (0, 0)),
            pl.BlockSpec((None, D, TF), lambda t, f: (layer, 0, f)),
            pl.BlockSpec((None, D, TF), lambda t, f: (layer, 0, f)),
            pl.BlockSpec((None, TF, D), lambda t, f: (layer, f, 0)),
        ],
        out_specs=pl.BlockSpec((TM, D), lambda t, f: (t, 0)),
        out_shape=jax.ShapeDtypeStruct((nt * TM, D), f32),
        scratch_shapes=[pltpu.VMEM((TM, D), bf16), pltpu.VMEM((TM, D), f32)],
        compiler_params=_params(("parallel", "arbitrary")),
        name="ffn_dense",
    )(h, mod, g, wg, wu, wd)


def _split_bf16(x):
    hi = x.astype(bf16)
    lo = (x - hi.astype(f32)).astype(bf16)
    return hi, lo


def _split3(x):
    p0 = x.astype(bf16)
    r = x - p0.astype(f32)
    p1 = r.astype(bf16)
    p2 = (r - p1.astype(f32)).astype(bf16)
    return p0, p1, p2


def _dot3(a_parts, b):
    out = jnp.dot(a_parts[0], b, preferred_element_type=f32)
    for a in a_parts[1:]:
        out = out + jnp.dot(a, b, preferred_element_type=f32)
    return out


RT = 256
RT_PER_SAMPLE = SEQ // RT
ALIGN = 16
WIN = RT + ALIGN
FT = 512
HEAD_ROWS = 128
SLAB = 64
STAGE = FT + -(-WIN // SLAB) * SLAB
CAP = 17408
CAP_BLOCKS = CAP // FT
DN_T = (((0,), (0,)), ((), ()))
NO_ROW = -1e9


def _route_kernel(h_ref, mod_ref, g_ref, r_ref, xs_hbm, gs_hbm, rank_ref, seg_ref, cnt_ref,
                  x_stage, g_stage, cnt_sc, sems, *, n_rt):
    i = pl.program_id(0)

    @pl.when(i == 0)
    def _():
        x_stage[...] = jnp.zeros_like(x_stage)
        g_stage[...] = jnp.zeros_like(g_stage)
        for e in range(N_EXPERTS):
            cnt_sc[e] = 0

    x = _norm_mod(h_ref[...], g_ref[...], mod_ref[3:4, :], mod_ref[4:5, :])
    xh, xl = _split_bf16(x)
    rh, rl = _split_bf16(r_ref[...])
    logits = (jnp.dot(xh, rh, preferred_element_type=f32) + jnp.dot(xl, rh, preferred_element_type=f32)
              + jnp.dot(xh, rl, preferred_element_type=f32))
    lane = lax.broadcasted_iota(jnp.int32, (RT, 128), 1)
    lg = jnp.where(lane < N_EXPERTS, logits, -jnp.inf)
    m1 = jnp.max(lg, axis=1, keepdims=True)
    i1 = jnp.min(jnp.where(lg == m1, lane, 128), axis=1, keepdims=True)
    lg2 = jnp.where(lane == i1, -jnp.inf, lg)
    m2 = jnp.max(lg2, axis=1, keepdims=True)
    i2 = jnp.min(jnp.where(lg2 == m2, lane, 128), axis=1, keepdims=True)
    e2 = jnp.exp(m2 - m1)
    den = 1.0 + e2
    gates = jnp.where(lane == i1, 1.0 / den, 0.0) + jnp.where(lane == i2, e2 / den, 0.0)
    used = jnp.where((lane == i1) | (lane == i2), 1.0, 0.0).astype(bf16)

    tp = lax.broadcasted_iota(jnp.int32, (RT, RT), 0)
    tt = lax.broadcasted_iota(jnp.int32, (RT, RT), 1)
    earlier = jnp.where(tp < tt, 1.0, 0.0).astype(bf16)
    eye = jnp.where(tp == tt, 1.0, 0.0).astype(bf16)
    rank_t = lax.dot_general(used, earlier, DN_T, preferred_element_type=f32)
    used_t = lax.dot_general(used, eye, DN_T, preferred_element_type=f32)
    rank_t = jnp.where(used_t > 0, rank_t, NO_ROW)
    rank_ref[...] = rank_t[0:N_EXPERTS]

    g_hi, g_lo = _split_bf16(gates)
    xg = jnp.concatenate([xh, g_hi, g_lo], axis=1)
    head_row = lax.broadcasted_iota(jnp.int32, (HEAD_ROWS, RT), 0).astype(f32)
    slab_row = lax.broadcasted_iota(jnp.int32, (SLAB, RT), 0).astype(f32)

    def flush_copies(e, done_rows):
        dst = pl.multiple_of(e * CAP + done_rows, FT)
        return (pltpu.make_async_copy(x_stage.at[e, pl.ds(0, FT), :], xs_hbm.at[pl.ds(dst, FT), :], sems.at[e, 0]),
                pltpu.make_async_copy(g_stage.at[e, pl.ds(0, FT), :], gs_hbm.at[pl.ds(dst, FT), :], sems.at[e, 1]))

    state = []
    n_of = []
    for e in range(N_EXPERTS):
        cnt = cnt_sc[e]
        seg_ref[i * N_EXPERTS + e] = cnt
        fill = cnt % FT
        start = pl.multiple_of((fill // ALIGN) * ALIGN, ALIGN)
        n_e = jnp.sum(used_t[e:e + 1, :]).astype(jnp.int32)
        cnt_sc[e] = cnt + n_e
        n_of.append(n_e)
        state.append((start, fill - start, fill + n_e >= FT, cnt - fill))

    def target_rows(e):
        return rank_t[e:e + 1, :] + state[e][1].astype(f32)

    for e in range(N_EXPERTS):
        start = state[e][0]
        p = jnp.where(head_row == target_rows(e), 1.0, 0.0).astype(bf16)
        rows = jnp.dot(p, xg, preferred_element_type=f32)
        grow = rows[:, D:D + 128] + rows[:, D + 128:]
        old = pl.ds(start, ALIGN)
        new = pl.ds(pl.multiple_of(start + ALIGN, ALIGN), HEAD_ROWS - ALIGN)
        x_stage[e, old, :] = (x_stage[e, old, :].astype(f32) + rows[:ALIGN, :D]).astype(bf16)
        x_stage[e, new, :] = rows[ALIGN:, :D].astype(bf16)
        g_stage[e, old, :] = g_stage[e, old, :] + grow[:ALIGN]
        g_stage[e, new, :] = grow[ALIGN:]

    crowded = False
    for e in range(N_EXPERTS):
        crowded = crowded | (state[e][1] + n_of[e] > HEAD_ROWS)

    @pl.when(crowded)
    def _():
        for e in range(N_EXPERTS):
            def place(sl, carry, e=e):
                p = jnp.where(slab_row + (sl * SLAB).astype(f32) == target_rows(e), 1.0, 0.0).astype(bf16)
                rows = jnp.dot(p, xg, preferred_element_type=f32)
                dst = pl.ds(pl.multiple_of(state[e][0] + sl * SLAB, ALIGN), SLAB)
                x_stage[e, dst, :] = rows[:, :D].astype(bf16)
                g_stage[e, dst, :] = rows[:, D:D + 128] + rows[:, D + 128:]
                return carry
            lax.fori_loop(HEAD_ROWS // SLAB, (state[e][1] + n_of[e] + SLAB - 1) // SLAB, place, 0)

    full = [(st[2], st[3]) for st in state]
    for e in range(N_EXPERTS):
        @pl.when(full[e][0])
        def _():
            for copy in flush_copies(e, full[e][1]):
                copy.start()

    for e in range(N_EXPERTS):
        @pl.when(full[e][0])
        def _():
            for copy in flush_copies(e, full[e][1]):
                copy.wait()
            x_stage[e, 0:STAGE - FT, :] = x_stage[e, FT:STAGE, :]
            x_stage[e, STAGE - FT:STAGE, :] = jnp.zeros((FT, D), bf16)
            g_stage[e, 0:STAGE - FT, :] = g_stage[e, FT:STAGE, :]
            g_stage[e, STAGE - FT:STAGE, :] = jnp.zeros((FT, 128), f32)

    @pl.when(i == n_rt - 1)
    def _():
        for e in range(N_EXPERTS):
            c = cnt_sc[e]
            cnt_ref[e] = c
            for copy in flush_copies(e, c - c % FT):
                copy.start()
                copy.wait()


def _moe_route(h, mod, g, router, n_rt):
    return pl.pallas_call(
        functools.partial(_route_kernel, n_rt=n_rt),
        grid=(n_rt,),
        in_specs=[
            pl.BlockSpec((RT, D), lambda i: (i, 0)),
            pl.BlockSpec((None, 8, D), lambda i: (jnp.minimum(i // RT_PER_SAMPLE, CTX_MOD_ROW), 0, 0)),
            _const_spec((1, D)), _const_spec((D, 128)),
        ],
        out_specs=[
            pl.BlockSpec(memory_space=pl.ANY), pl.BlockSpec(memory_space=pl.ANY),
            pl.BlockSpec((None, N_EXPERTS, RT), lambda i: (i, 0, 0)),
            pl.BlockSpec(memory_space=pltpu.SMEM), pl.BlockSpec(memory_space=pltpu.SMEM),
        ],
        out_shape=[
            jax.ShapeDtypeStruct((N_EXPERTS * CAP, D), bf16),
            jax.ShapeDtypeStruct((N_EXPERTS * CAP, 128), f32),
            jax.ShapeDtypeStruct((n_rt, N_EXPERTS, RT), f32),
            jax.ShapeDtypeStruct((n_rt * N_EXPERTS,), jnp.int32),
            jax.ShapeDtypeStruct((N_EXPERTS,), jnp.int32),
        ],
        scratch_shapes=[
            pltpu.VMEM((N_EXPERTS, STAGE, D), bf16), pltpu.VMEM((N_EXPERTS, STAGE, 128), f32),
            pltpu.SMEM((N_EXPERTS,), jnp.int32), pltpu.SemaphoreType.DMA((N_EXPERTS, 2)),
        ],
        compiler_params=_params(("arbitrary",)),
        name="moe_route",
    )(h, mod, g, router)


def _moe_ffn_kernel(te_ref, tb_ref, nt_ref, x_ref, gs_ref, wg_ref, wu_ref, wd_ref, y_ref, acc_sc):
    j = pl.program_id(0)
    f = pl.program_id(1)

    @pl.when(j < nt_ref[0])
    def _():
        @pl.when(f == 0)
        def _():
            acc_sc[...] = jnp.zeros_like(acc_sc)

        x = x_ref[...]
        gate = jnp.dot(x, wg_ref[...], preferred_element_type=f32)
        up = jnp.dot(x, wu_ref[...], preferred_element_type=f32)
        hid = (_silu(gate) * up).astype(bf16)
        acc_sc[...] += jnp.dot(hid, wd_ref[...], preferred_element_type=f32)

        @pl.when(f == NF - 1)
        def _():
            lane = lax.broadcasted_iota(jnp.int32, (FT, 128), 1)
            ge = jnp.sum(jnp.where(lane == te_ref[j], gs_ref[...], 0.0), axis=1, keepdims=True)
            y_ref[...] = (ge * acc_sc[...]).astype(bf16)


def _moe_experts(xs, gs, wg, wu, wd, tile_expert, tile_block, n_tiles):
    max_tiles = tile_expert.shape[0]

    def f_eff(j, f, nt):
        return jnp.where(j < nt[0], f, NF - 1)

    grid_spec = pltpu.PrefetchScalarGridSpec(
        num_scalar_prefetch=3,
        grid=(max_tiles, NF),
        in_specs=[
            pl.BlockSpec((FT, D), lambda j, f, te, tb, nt: (tb[j], 0)),
            pl.BlockSpec((FT, 128), lambda j, f, te, tb, nt: (tb[j], 0)),
            pl.BlockSpec((None, D, TF), lambda j, f, te, tb, nt: (te[j], 0, f_eff(j, f, nt))),
            pl.BlockSpec((None, D, TF), lambda j, f, te, tb, nt: (te[j], 0, f_eff(j, f, nt))),
            pl.BlockSpec((None, TF, D), lambda j, f, te, tb, nt: (te[j], f_eff(j, f, nt), 0)),
        ],
        out_specs=pl.BlockSpec((FT, D), lambda j, f, te, tb, nt: (tb[j], 0)),
        scratch_shapes=[pltpu.VMEM((FT, D), f32)],
    )
    return pl.pallas_call(
        _moe_ffn_kernel,
        grid_spec=grid_spec,
        out_shape=jax.ShapeDtypeStruct((N_EXPERTS * CAP, D), bf16),
        compiler_params=_params(("arbitrary", "arbitrary")),
        name="moe_experts",
    )(tile_expert, tile_block, n_tiles, xs, gs, wg, wu, wd)


def _moe_combine_kernel(seg_ref, nseg_ref, wmax_ref, rank_ref, h_ref, mod_ref, fg_ref, ys_hbm, o_ref, ybuf, acc_sc,
                        sems, *, n_rt, apply_final):
    i = pl.program_id(0)
    main_row = lax.broadcasted_iota(jnp.int32, (RT, RT), 0).astype(f32)
    last_row = lax.broadcasted_iota(jnp.int32, (ALIGN, RT), 0).astype(f32) + RT

    def window(step, e):
        seg = seg_ref[step * N_EXPERTS + e]
        start = pl.multiple_of(jnp.minimum((seg // ALIGN) * ALIGN, wmax_ref[e]), ALIGN)
        return seg, start

    def window_copy(step, e):
        _, start = window(step, e)
        return pltpu.make_async_copy(ys_hbm.at[pl.ds(e * CAP + start, WIN), :], ybuf.at[step % 2, e],
                                     sems.at[step % 2, e])

    def fetch(step):
        for e in range(N_EXPERTS):
            @pl.when(nseg_ref[step * N_EXPERTS + e] > 0)
            def _():
                window_copy(step, e).start()

    @pl.when(i == 0)
    def _():
        ybuf[...] = jnp.zeros_like(ybuf)
        fetch(0)

    @pl.when(i + 1 < n_rt)
    def _():
        fetch(i + 1)

    acc = None
    spill = False
    for e in range(N_EXPERTS):
        seg, start = window(i, e)
        n = nseg_ref[i * N_EXPERTS + e]

        @pl.when(n > 0)
        def _():
            window_copy(i, e).wait()

        target = rank_ref[e:e + 1, :] + (seg - start).astype(f32)
        p = jnp.where(main_row == target, 1.0, 0.0).astype(bf16)
        part = lax.dot_general(p, ybuf[i % 2, e, 0:RT, :], DN_T, preferred_element_type=f32)
        acc = part if acc is None else acc + part
        spill = spill | (seg - start + n > RT)
    acc_sc[...] = acc

    @pl.when(spill)
    def _():
        for e in range(N_EXPERTS):
            seg, start = window(i, e)
            target = rank_ref[e:e + 1, :] + (seg - start).astype(f32)
            p = jnp.where(last_row == target, 1.0, 0.0).astype(bf16)
            acc_sc[...] += lax.dot_general(p, ybuf[i % 2, e, RT:WIN, :], DN_T, preferred_element_type=f32)

    out = h_ref[...] + mod_ref[5:6, :] * acc_sc[...]
    if apply_final:
        out = out * lax.rsqrt(jnp.mean(out * out, axis=-1, keepdims=True) + EPS) * fg_ref[...]
    o_ref[...] = out


def _moe_combine(ys, rank, seg, nseg, wmax, h, mod, final_g, n_rt, apply_final):
    grid_spec = pltpu.PrefetchScalarGridSpec(
        num_scalar_prefetch=3,
        grid=(n_rt,),
        in_specs=[
            pl.BlockSpec((None, N_EXPERTS, RT), lambda i, *_: (i, 0, 0)),
            pl.BlockSpec((RT, D), lambda i, *_: (i, 0)),
            pl.BlockSpec((None, 8, D), lambda i, *_: (jnp.minimum(i // RT_PER_SAMPLE, CTX_MOD_ROW), 0, 0)),
            pl.BlockSpec((1, D), lambda i, *_: (0, 0)),
            pl.BlockSpec(memory_space=pl.ANY),
        ],
        out_specs=pl.BlockSpec((RT, D), lambda i, *_: (i, 0)),
        scratch_shapes=[pltpu.VMEM((2, N_EXPERTS, WIN, D), bf16), pltpu.VMEM((RT, D), f32),
                        pltpu.SemaphoreType.DMA((2, N_EXPERTS))],
    )
    return pl.pallas_call(
        functools.partial(_moe_combine_kernel, n_rt=n_rt, apply_final=apply_final),
        grid_spec=grid_spec,
        out_shape=jax.ShapeDtypeStruct((n_rt * RT, D), f32),
        compiler_params=_params(("arbitrary",)),
        name="moe_combine",
    )(seg, nseg, wmax, rank, h, mod, final_g, ys)


def _moe(h, mod, g, router, wg, wu, wd, nt, final_g, apply_final):
    n_rt = nt * (TM // RT)
    xs, gs, rank, seg, counts = _moe_route(h, mod, g, router, n_rt)
    tiles = (counts + FT - 1) // FT
    ends = jnp.cumsum(tiles)
    n_tiles = ends[-1]
    max_tiles = (2 * n_rt * RT) // FT + N_EXPERTS
    j = jnp.minimum(jnp.arange(max_tiles, dtype=jnp.int32), n_tiles - 1)
    tile_expert = jnp.sum((j[:, None] >= ends[None, :]).astype(jnp.int32), axis=1)
    tile_block = tile_expert * CAP_BLOCKS + j - (ends - tiles)[tile_expert]
    ys = _moe_experts(xs, gs, wg, wu, wd, tile_expert, tile_block, n_tiles.reshape(1))
    seg2 = seg.reshape(n_rt, N_EXPERTS)
    nseg = (jnp.concatenate([seg2[1:], counts[None, :]], axis=0) - seg2).reshape(-1)
    wmax = jnp.maximum(tiles * FT - WIN, 0)
    return _moe_combine(ys, rank, seg, nseg, wmax, h, mod, final_g, n_rt, apply_final)


def _ssd_in_kernel(h_ref, mod_ref, g_ref, wz_ref, wx_ref, wdt_ref, z_ref, x_ref, dt_ref):
    a = _norm_mod(h_ref[...], g_ref[...], mod_ref[0:1, :], mod_ref[1:2, :]).astype(bf16)
    for j in range(SSD_INNER // D):
        z_ref[:, j * D:(j + 1) * D] = jnp.dot(a, wz_ref[:, j * D:(j + 1) * D], preferred_element_type=f32).astype(bf16)
    for j in range(SSD_CONV_CH // D):
        x_ref[:, j * D:(j + 1) * D] = jnp.dot(a, wx_ref[:, j * D:(j + 1) * D], preferred_element_type=f32).astype(bf16)
    dt_ref[...] = jnp.dot(a, wdt_ref[...], preferred_element_type=f32)


def _ssd_in_proj(h, mod, g, wz, wx, wdt):
    return pl.pallas_call(
        _ssd_in_kernel,
        grid=(NT,),
        in_specs=[_tile_spec(D), _mod_spec(), _const_spec((1, D)), _const_spec((D, SSD_INNER)),
                  _const_spec((D, SSD_CONV_CH)), _const_spec((D, 256))],
        out_specs=[_tile_spec(SSD_INNER), _tile_spec(SSD_CONV_CH), _tile_spec(256)],
        out_shape=[jax.ShapeDtypeStruct((T, SSD_INNER), bf16), jax.ShapeDtypeStruct((T, SSD_CONV_CH), bf16),
                   jax.ShapeDtypeStruct((T, 256), f32)],
        compiler_params=_params(("parallel",)),
        name="ssd_in_proj",
    )(h, mod, g, wz, wx, wdt)


CONV_TM = 256
CONV_HALO = 16
CONV_TILES_PER_SAMPLE = SEQ // CONV_TM


def _conv_kernel(x_ref, prev_ref, next_ref, w_ref, b_ref, o_ref):
    i = pl.program_id(0)
    is_ctx = i >= N_LAT // CONV_TM
    seg_start = is_ctx | (i % CONV_TILES_PER_SAMPLE == 0)
    seg_end = is_ctx | (i % CONV_TILES_PER_SAMPLE == CONV_TILES_PER_SAMPLE - 1)
    x = x_ref[...].astype(f32)
    prev_row = jnp.where(seg_start, 0.0, prev_ref[CONV_HALO - 1:CONV_HALO, :].astype(f32))
    next_row = jnp.where(seg_end, 0.0, next_ref[0:1, :].astype(f32))
    row = lax.broadcasted_iota(jnp.int32, x.shape, 0)
    xm1 = jnp.where(row == 0, prev_row, pltpu.roll(x, 1, 0))
    xp1 = jnp.where(row == CONV_TM - 1, next_row, pltpu.roll(x, CONV_TM - 1, 0))
    y = w_ref[0:1, :] * xm1 + w_ref[1:2, :] * x + w_ref[2:3, :] * xp1 + b_ref[...]
    o_ref[...] = _silu(y).astype(bf16)


def _ssd_conv(xbc, conv_w, conv_b):
    n = T // CONV_TM
    per = CONV_TM // CONV_HALO
    last = T // CONV_HALO - 1
    return pl.pallas_call(
        _conv_kernel,
        grid=(n,),
        in_specs=[
            pl.BlockSpec((CONV_TM, SSD_CONV_CH), lambda i: (i, 0)),
            pl.BlockSpec((CONV_HALO, SSD_CONV_CH), lambda i: (jnp.maximum(i * per - 1, 0), 0)),
            pl.BlockSpec((CONV_HALO, SSD_CONV_CH), lambda i: (jnp.minimum((i + 1) * per, last), 0)),
            _const_spec((8, SSD_CONV_CH)), _const_spec((1, SSD_CONV_CH)),
        ],
        out_specs=pl.BlockSpec((CONV_TM, SSD_CONV_CH), lambda i: (i, 0)),
        out_shape=jax.ShapeDtypeStruct((T, SSD_CONV_CH), bf16),
        compiler_params=_params(("parallel",)),
        name="ssd_conv",
    )(xbc, xbc, xbc, conv_w, conv_b)


def _ssd_scan_kernel(x_ref, b_ref, c_ref, dt_ref, bias_ref, alog_ref, y_ref, state_sc):
    d = pl.program_id(0)
    c = pl.program_id(2)

    @pl.when(c == 0)
    def _():
        state_sc[...] = jnp.zeros_like(state_sc)

    L = SSD_CHUNK
    li = lax.broadcasted_iota(jnp.int32, (L, L), 0)
    si = lax.broadcasted_iota(jnp.int32, (L, L), 1)
    causal = (si - li) * (1 - 2 * d) <= 0
    tri = jnp.where(causal, 1.0, 0.0).astype(bf16)
    hi = lax.broadcasted_iota(jnp.int32, (128, SSD_INNER), 0)
    ci = lax.broadcasted_iota(jnp.int32, (128, SSD_INNER), 1)
    expand = jnp.where(ci // SSD_P == hi, 1.0, 0.0).astype(bf16)

    dt = jax.nn.softplus(dt_ref[...] + bias_ref[...])
    a_neg = -jnp.exp(alog_ref[...])
    da = dt * a_neg
    da_parts = _split3(da)
    cs = (jnp.dot(tri, da_parts[0], preferred_element_type=f32) + jnp.dot(tri, da_parts[1], preferred_element_type=f32)
          + jnp.dot(tri, da_parts[2], preferred_element_type=f32))
    cs_t = cs.T
    total = jnp.sum(da, axis=0, keepdims=True)

    e_out = jnp.exp(cs)
    e_in = jnp.exp(total - cs) * dt
    dt_x = jnp.dot(dt.astype(bf16), expand, preferred_element_type=f32).astype(bf16)
    e_in_x = jnp.dot(e_in.astype(bf16), expand, preferred_element_type=f32).astype(bf16)
    out_scale = jnp.dot(e_out.astype(bf16), expand, preferred_element_type=f32)
    chunk_decay = _dot3(_split_bf16(jnp.broadcast_to(jnp.exp(total), (8, 128))), expand)[0:1]

    x = x_ref[...]
    first_head = lax.broadcasted_iota(jnp.int32, (L, 2 * SSD_P), 1) < SSD_P
    xdt = x * dt_x
    xw = x * e_in_x

    for g in range(SSD_GROUPS):
        bg = b_ref[:, g * SSD_N:(g + 1) * SSD_N]
        cg = c_ref[:, g * SSD_N:(g + 1) * SSD_N]
        cb = lax.dot_general(cg, bg, (((1,), (1,)), ((), ())), preferred_element_type=f32)
        gsl = slice(g * SSD_HPG * SSD_P, (g + 1) * SSD_HPG * SSD_P)
        st = state_sc[g]
        y_off = jnp.dot(cg, st.astype(bf16), preferred_element_type=f32) * out_scale[:, gsl]
        def decay_matrix(hd):
            seg = cs[:, hd:hd + 1] - cs_t[hd:hd + 1, :]
            return (cb * jnp.exp(jnp.where(causal, seg, -jnp.inf))).astype(bf16)

        ys = []
        for r in range(0, SSD_HPG, 2):
            hd = g * SSD_HPG + r
            m2 = jnp.concatenate([decay_matrix(hd), decay_matrix(hd + 1)], axis=1)
            x2 = xdt[:, hd * SSD_P:(hd + 2) * SSD_P]
            rhs = jnp.concatenate([jnp.where(first_head, x2, 0), jnp.where(first_head, 0, x2)], axis=0)
            ys.append(jnp.dot(m2, rhs, preferred_element_type=f32))
        y_ref[:, gsl] = (jnp.concatenate(ys, axis=1) + y_off).astype(bf16)
        new = lax.dot_general(bg, xw[:, gsl], (((0,), (0,)), ((), ())), preferred_element_type=f32)
        state_sc[g] = st * chunk_decay[:, gsl] + new


SSD_NCHUNK = (SEQ + CTX) // SSD_CHUNK
SSD_CTX_CHUNKS = CTX // SSD_CHUNK
SSD_LAT_CHUNKS = SEQ // SSD_CHUNK


def _ssd_chunk_block(d, b, c):
    ctx_j = jnp.where(d == 0, c, SSD_CTX_CHUNKS - 1 - c)
    lat_j = jnp.where(d == 0, c - SSD_CTX_CHUNKS, SSD_LAT_CHUNKS - 1 - (c - SSD_CTX_CHUNKS))
    return jnp.where(c < SSD_CTX_CHUNKS, N_LAT // SSD_CHUNK + b * SSD_CTX_CHUNKS + ctx_j, b * SSD_LAT_CHUNKS + lat_j)


def _ssd_scan(xbc, dt, dt_bias, a_log):
    x_blk = SSD_INNER // SSD_BC
    return pl.pallas_call(
        _ssd_scan_kernel,
        grid=(2, BATCH, SSD_NCHUNK),
        in_specs=[
            pl.BlockSpec((SSD_CHUNK, SSD_INNER), lambda d, b, c: (_ssd_chunk_block(d, b, c), 0)),
            pl.BlockSpec((SSD_CHUNK, SSD_BC), lambda d, b, c: (_ssd_chunk_block(d, b, c), x_blk)),
            pl.BlockSpec((SSD_CHUNK, SSD_BC), lambda d, b, c: (_ssd_chunk_block(d, b, c), x_blk + 1)),
            pl.BlockSpec((SSD_CHUNK, 128), lambda d, b, c: (_ssd_chunk_block(d, b, c), d)),
            pl.BlockSpec((None, 1, 128), lambda d, b, c: (d, 0, 0)),
            pl.BlockSpec((None, 1, 128), lambda d, b, c: (d, 0, 0)),
        ],
        out_specs=pl.BlockSpec((None, SSD_CHUNK, SSD_INNER), lambda d, b, c: (d, _ssd_chunk_block(d, b, c), 0)),
        out_shape=jax.ShapeDtypeStruct((2, T, SSD_INNER), bf16),
        scratch_shapes=[pltpu.VMEM((SSD_GROUPS, SSD_N, SSD_HPG * SSD_P), f32)],
        compiler_params=_params(("parallel", "parallel", "arbitrary")),
        name="ssd_scan",
    )(xbc, xbc, xbc, dt, dt_bias, a_log)


def _ssd_out_kernel(yf_ref, yb_ref, x_ref, z_ref, dskip_ref, ng_ref, w_ref, h_ref, mod_ref, o_ref):
    y = yf_ref[...].astype(f32) + yb_ref[...].astype(f32) + x_ref[...].astype(f32) * dskip_ref[...]
    gated = y * _silu(z_ref[...].astype(f32))
    gw = SSD_INNER // SSD_GROUPS
    parts = []
    for g in range(SSD_GROUPS):
        s = gated[:, g * gw:(g + 1) * gw]
        parts.append(s * lax.rsqrt(jnp.mean(s * s, axis=-1, keepdims=True) + EPS))
    normed = (jnp.concatenate(parts, axis=1) * ng_ref[...]).astype(bf16)
    out = jnp.dot(normed, w_ref[...], preferred_element_type=f32)
    o_ref[...] = h_ref[...] + mod_ref[2:3, :] * out


def _ssd_out(y2, xbc, z, dskip, ng, w_out, h, mod, nt):
    return pl.pallas_call(
        _ssd_out_kernel,
        grid=(nt,),
        in_specs=[
            pl.BlockSpec((None, TM, SSD_INNER), lambda t: (0, t, 0)),
            pl.BlockSpec((None, TM, SSD_INNER), lambda t: (1, t, 0)),
            pl.BlockSpec((TM, SSD_INNER), lambda t: (t, 0)),
            _tile_spec(SSD_INNER), _const_spec((1, SSD_INNER)), _const_spec((1, SSD_INNER)),
            _const_spec((SSD_INNER, D)), _tile_spec(D), _mod_spec(),
        ],
        out_specs=_tile_spec(D),
        out_shape=jax.ShapeDtypeStruct((nt * TM, D), f32),
        compiler_params=_params(("parallel",)),
        name="ssd_out",
    )(y2, y2, xbc, z, dskip, ng, w_out, h, mod)


def _gelu(x):
    return 0.5 * x * (1.0 + lax.erf(x * math.sqrt(0.5)))


def _cmlp_kernel(h_ref, mod_ref, g_ref, wu_ref, wv_ref, bu_ref, bv_ref, vg_ref, ws_ref, bs_ref, wo_ref, o_ref, uv_sc):
    a = _norm_mod(h_ref[...], g_ref[...], mod_ref[0:1, :], mod_ref[1:2, :]).astype(bf16)
    v = _gelu(jnp.dot(a, wv_ref[...], preferred_element_type=f32) + bv_ref[...])
    v = (v * lax.rsqrt(jnp.mean(v * v, axis=-1, keepdims=True) + EPS) * vg_ref[...]).astype(bf16)
    u = _gelu(jnp.dot(a, wu_ref[...], preferred_element_type=f32) + bu_ref[...])
    for ck in range(TM // CMLP_CHUNK):
        rows = slice(ck * CMLP_CHUNK, (ck + 1) * CMLP_CHUNK)
        for g in range(CMLP_GROUPS):
            cols = slice(g * CMLP_GW, (g + 1) * CMLP_GW)
            mixed = jnp.dot(ws_ref[g], v[rows, cols], preferred_element_type=f32) + bs_ref[:, cols]
            uv_sc[rows, cols] = (u[rows, cols] * mixed).astype(bf16)
    out = jnp.dot(uv_sc[...], wo_ref[...], preferred_element_type=f32)
    o_ref[...] = h_ref[...] + mod_ref[2:3, :] * out


def _cmlp(h, mod, g, wu, wv, bu, bv, vg, ws, bs, wo, nt):
    return pl.pallas_call(
        _cmlp_kernel,
        grid=(nt,),
        in_specs=[
            _tile_spec(D), _mod_spec(), _const_spec((1, D)), _const_spec((D, CMLP_D)), _const_spec((D, CMLP_D)),
            _const_spec((1, CMLP_D)), _const_spec((1, CMLP_D)), _const_spec((1, CMLP_D)),
            _const_spec((CMLP_GROUPS, CMLP_CHUNK, CMLP_CHUNK)), _const_spec((CMLP_CHUNK, CMLP_D)),
            _const_spec((CMLP_D, D)),
        ],
        out_specs=_tile_spec(D),
        out_shape=jax.ShapeDtypeStruct((nt * TM, D), f32),
        scratch_shapes=[pltpu.VMEM((TM, CMLP_D), bf16)],
        compiler_params=_params(("parallel",)),
        name="cmlp",
    )(h, mod, g, wu, wv, bu, bv, vg, ws, bs, wo)


def _rope_tables():
    pos = np.arange(SEQ)
    inv_freq = (1.0 / (np.float32(ROPE_THETA) ** (np.arange(0, ROPE_AXIS_DIM, 2, dtype=np.float32) / ROPE_AXIS_DIM)))
    inv_freq = inv_freq.astype(np.float32)
    ang_r = (pos // GRID_W).astype(np.float32)[:, None] * inv_freq
    ang_c = (pos % GRID_W).astype(np.float32)[:, None] * inv_freq
    cos = np.concatenate([np.cos(ang_r)] * 2 + [np.cos(ang_c)] * 2, axis=1)
    sin = np.concatenate([-np.sin(ang_r), np.sin(ang_r), -np.sin(ang_c), np.sin(ang_c)], axis=1)
    cos = np.concatenate([cos, np.ones((TM, HEAD_DIM), np.float32)], axis=0).astype(np.float32)
    sin = np.concatenate([sin, np.zeros((TM, HEAD_DIM), np.float32)], axis=0).astype(np.float32)
    return jnp.asarray(cos), jnp.asarray(sin)


def _row(v):
    return v.reshape(1, -1)


def _pad_lanes(v, width=128):
    return jnp.pad(v, ((0, 0), (0, width - v.shape[1])))


def kernel(x, c, ctx, c_ctx, w_mod, b_mod, norm1_g, norm2_g, attn_w_qkv, attn_q_g, attn_k_g, attn_w_o, ssd_w_in, ssd_conv_w, ssd_conv_b, ssd_dt_bias_f, ssd_dt_bias_b, ssd_a_log_f, ssd_a_log_b, ssd_d_skip, ssd_norm_g, ssd_w_out, cmlp_w_in, cmlp_b_in, cmlp_v_g, cmlp_w_s, cmlp_b_s, cmlp_w_out, ffn_w_gate, ffn_w_up, ffn_w_down, moe_router, moe_w_gate, moe_w_up, moe_w_down, final_g):
    h = jnp.concatenate([x.reshape(N_LAT, D), ctx.reshape(N_CTX, D)], axis=0)
    cond = jnp.concatenate([c, c_ctx[None, :], jnp.zeros((8 - BATCH - 1, D), f32)], axis=0)
    mods = _modulation(cond, w_mod, b_mod)
    cos_t, sin_t = _rope_tables()
    ffn_w = [w.astype(bf16) for w in (ffn_w_gate, ffn_w_up, ffn_w_down)]
    moe_w = None
    final_row = _row(final_g)

    for i in range(DEPTH):
        need_ctx = i < DEPTH - 1
        nt = NT if need_ctx else NT_LAT
        mod = mods[i]
        kind, j = i % 3, i // 3
        g1 = _row(norm1_g[i])
        if kind == 0:
            w_qkv = attn_w_qkv[j]
            w_partner = _rope_partner(w_qkv[:, :(N_HEADS + N_KV) * HEAD_DIM]).astype(bf16)
            q_g = jnp.stack([attn_q_g[j], _rope_partner(attn_q_g[j])])
            k_g = jnp.stack([attn_k_g[j], _rope_partner(attn_k_g[j])])
            q, k, v = _qkv_proj(h, mod, g1, w_qkv.astype(bf16), w_partner, q_g, k_g, cos_t, sin_t)
            moe_layer = i if i % 2 == 1 else i + 1
            nxt = moe_layer // 2
            o, moe_w = _attention(q, k, v, moe_w_gate, moe_w_up, moe_w_down, nxt)
            w_o = attn_w_o[j].astype(bf16)
            h = _proj_res2(o, _attention_ctx(q, k, v), w_o, h, mod) if need_ctx else _proj_res(o, w_o, h, mod, nt)
        elif kind == 1:
            w_in = ssd_w_in[j]
            wz = w_in[:, :SSD_INNER].astype(bf16)
            wx = w_in[:, SSD_INNER:SSD_INNER + SSD_CONV_CH].astype(bf16)
            w_dt = w_in[:, SSD_INNER + SSD_CONV_CH:]
            wdt = jnp.concatenate([_pad_lanes(w_dt[:, :SSD_HEADS]), _pad_lanes(w_dt[:, SSD_HEADS:])], axis=1).astype(bf16)
            z, xbc, dt = _ssd_in_proj(h, mod, g1, wz, wx, wdt)
            conv_w = jnp.pad(ssd_conv_w[j], ((0, 8 - ssd_conv_w.shape[1]), (0, 0)))
            xbc = _ssd_conv(xbc, conv_w, _row(ssd_conv_b[j]))
            dt_bias = jnp.stack([_pad_lanes(_row(ssd_dt_bias_f[j])), _pad_lanes(_row(ssd_dt_bias_b[j]))])
            a_log = jnp.stack([_pad_lanes(_row(ssd_a_log_f[j])), _pad_lanes(_row(ssd_a_log_b[j]))])
            y2 = _ssd_scan(xbc, dt, dt_bias, a_log)
            dskip = _row(jnp.repeat(ssd_d_skip[j], SSD_P))
            h = _ssd_out(y2, xbc, z, dskip, _row(ssd_norm_g[j]), ssd_w_out[j].astype(bf16), h, mod, nt)
        else:
            w_in = cmlp_w_in[j]
            b_in = cmlp_b_in[j]
            bs = jnp.repeat(cmlp_b_s[j].T, CMLP_GW, axis=1)
            h = _cmlp(h, mod, g1, w_in[:, :CMLP_D].astype(bf16), w_in[:, CMLP_D:].astype(bf16),
                      _row(b_in[:CMLP_D]), _row(b_in[CMLP_D:]), _row(cmlp_v_g[j]), cmlp_w_s[j].astype(bf16), bs,
                      cmlp_w_out[j].astype(bf16), nt)
        kk = i // 2
        g2 = _row(norm2_g[i])
        if i % 2 == 0:
            h = _ffn(h, mod, g2, *ffn_w, kk, nt)
        else:
            h = _moe(h, mod, g2, _pad_lanes(moe_router[kk]), *moe_w, nt, final_row, i == DEPTH - 1)
    return h.reshape(BATCH, SEQ, D)
```

```python
import functools
import math

import jax
import jax.numpy as jnp
import numpy as np
from jax import lax
from jax.experimental import pallas as pl
from jax.experimental.pallas import tpu as pltpu

f32 = jnp.float32
bf16 = jnp.bfloat16

D = 1024
BATCH = 2
SEQ = 8192
CTX = 256
DEPTH = 4
GRID_W = 64
EPS = 1e-6
N_MOD = 6

HEAD_DIM = 128
N_HEADS = 8
N_KV = 2
Q_GROUP = 4
ROPE_AXIS_DIM = 64
ROPE_THETA = 10000.0

SSD_INNER = 2048
SSD_P = 64
SSD_HEADS = 32
SSD_GROUPS = 4
SSD_HPG = 8
SSD_N = 128
SSD_CHUNK = 128
SSD_BC = SSD_GROUPS * SSD_N
SSD_CONV_CH = SSD_INNER + 2 * SSD_BC

CMLP_D = 2048
CMLP_GROUPS = 8
CMLP_GW = 256
CMLP_CHUNK = 128

D_FF = 3584
N_EXPERTS = 8

N_LAT = BATCH * SEQ
N_CTX = BATCH * CTX
T = N_LAT + N_CTX
TM = 512
NT = T // TM
NT_LAT = N_LAT // TM
TILES_PER_SAMPLE = SEQ // TM
CTX_MOD_ROW = BATCH

TF = 1792
NF = D_FF // TF

TQ = 512
TK = 512
ATT_CHUNKS = SEQ // TK
VW = 2 * HEAD_DIM
LOG2E = math.log2(math.e)
NQ_LAT = SEQ // TQ

VMEM_LIMIT = 56 * 1024 * 1024


def _mod_row(t):
    return jnp.minimum(t // TILES_PER_SAMPLE, CTX_MOD_ROW)


def _tile_spec(width):
    return pl.BlockSpec((TM, width), lambda t: (t, 0))


def _mod_spec():
    return pl.BlockSpec((None, 8, D), lambda t: (_mod_row(t), 0, 0))


def _const_spec(shape):
    n = len(shape)
    return pl.BlockSpec(shape, lambda *_: (0,) * n)


def _params(semantics):
    return pltpu.CompilerParams(dimension_semantics=semantics, vmem_limit_bytes=VMEM_LIMIT)


def _silu(x):
    return x * jax.nn.sigmoid(x)


def _norm_mod(x, g, shift, scale):
    y = x * lax.rsqrt(jnp.mean(x * x, axis=-1, keepdims=True) + EPS) * g
    return y * (1.0 + scale) + shift


def _mod_kernel(c_ref, w_ref, b_ref, o_ref):
    s = _silu(c_ref[...])
    o_ref[...] = jnp.dot(s.astype(bf16), w_ref[...].astype(bf16), preferred_element_type=f32) + b_ref[...]


def _modulation(cond, w_mod, b_mod):
    nblk = N_MOD * D // D
    out = pl.pallas_call(
        _mod_kernel,
        grid=(DEPTH, nblk),
        in_specs=[
            pl.BlockSpec((8, D), lambda i, j: (0, 0)),
            pl.BlockSpec((None, D, D), lambda i, j: (i, 0, j)),
            pl.BlockSpec((None, 1, D), lambda i, j: (i, 0, j)),
        ],
        out_specs=pl.BlockSpec((None, 8, D), lambda i, j: (i, 0, j)),
        out_shape=jax.ShapeDtypeStruct((DEPTH, 8, N_MOD * D), f32),
        compiler_params=_params(("arbitrary", "arbitrary")),
        name="modulation",
    )(cond, w_mod, b_mod.reshape(DEPTH, 1, N_MOD * D))
    mod = out[:, :3].reshape(DEPTH, 3, N_MOD, D)
    return jnp.pad(mod, ((0, 0), (0, 0), (0, 8 - N_MOD), (0, 0)))


def _qkv_kernel(h_ref, mod_ref, g_ref, w_ref, wp_ref, qg_ref, kg_ref, cos_ref, sin_ref, q_ref, k_ref, v_ref):
    a = _norm_mod(h_ref[...], g_ref[...], mod_ref[0:1, :], mod_ref[1:2, :]).astype(bf16)
    qkv = jnp.dot(a, w_ref[...], preferred_element_type=f32)
    qkp = jnp.dot(a, wp_ref[...], preferred_element_type=f32)
    cos = cos_ref[...]
    sin = sin_ref[...]

    def head(x, xp, g, scale):
        r = lax.rsqrt(jnp.mean(x * x, axis=-1, keepdims=True) + EPS) * scale
        return (x * g[0:1, :] * cos + xp * g[1:2, :] * sin) * r

    qg = qg_ref[...]
    kg = kg_ref[...]
    for hh in range(N_HEADS):
        sl = slice(hh * HEAD_DIM, (hh + 1) * HEAD_DIM)
        q_ref[:, sl] = head(qkv[:, sl], qkp[:, sl], qg, HEAD_DIM ** -0.5 * LOG2E).astype(bf16)
    for hh in range(N_KV):
        sl = slice(hh * HEAD_DIM, (hh + 1) * HEAD_DIM)
        src = slice((N_HEADS + hh) * HEAD_DIM, (N_HEADS + hh + 1) * HEAD_DIM)
        k_ref[:, sl] = head(qkv[:, src], qkp[:, src], kg, 1.0).astype(bf16)
        v_ref[:, hh * VW:hh * VW + HEAD_DIM] = qkv[:, (N_HEADS + N_KV + hh) * HEAD_DIM:(N_HEADS + N_KV + hh + 1) * HEAD_DIM].astype(bf16)
        v_ref[:, hh * VW + HEAD_DIM:(hh + 1) * VW] = jnp.ones((TM, HEAD_DIM), bf16)


def _rope_partner(w):
    lead = w.shape[:-1]
    w5 = w.reshape(lead + (-1, 2, 2, ROPE_AXIS_DIM // 2))
    return w5[..., ::-1, :].reshape(w.shape)


def _qkv_proj(h, mod, g, w_qkv, w_partner, q_g, k_g, cos_t, sin_t):
    rope_spec = pl.BlockSpec((TM, HEAD_DIM), lambda t: (jnp.where(t < NT_LAT, t % TILES_PER_SAMPLE, TILES_PER_SAMPLE), 0))
    return pl.pallas_call(
        _qkv_kernel,
        grid=(NT,),
        in_specs=[
            _tile_spec(D), _mod_spec(), _const_spec((1, D)), _const_spec((D, (N_HEADS + 2 * N_KV) * HEAD_DIM)),
            _const_spec((D, (N_HEADS + N_KV) * HEAD_DIM)),
            _const_spec((2, HEAD_DIM)), _const_spec((2, HEAD_DIM)), rope_spec, rope_spec,
        ],
        out_specs=[_tile_spec(N_HEADS * HEAD_DIM), _tile_spec(N_KV * HEAD_DIM), _tile_spec(N_KV * VW)],
        out_shape=[
            jax.ShapeDtypeStruct((T, N_HEADS * HEAD_DIM), bf16),
            jax.ShapeDtypeStruct((T, N_KV * HEAD_DIM), bf16),
            jax.ShapeDtypeStruct((T, N_KV * VW), bf16),
        ],
        compiler_params=_params(("parallel",)),
        name="attn_qkv",
    )(h, mod, g, w_qkv, w_partner, q_g, k_g, cos_t, sin_t)


def _stack_heads(q):
    return jnp.concatenate([q[:, i * HEAD_DIM:(i + 1) * HEAD_DIM] for i in range(Q_GROUP)], axis=0)


def _scores(qs, k):
    return lax.dot_general(qs, k, (((1,), (1,)), ((), ())), preferred_element_type=f32)


def _attn_kernel(q_ref, kl_ref, vl_ref, kc_ref, vc_ref, wg_ref, wu_ref, wd_ref, o_ref, wg_o, wu_o, wd_o,
                 m_sc, acc_sc, s_a, s_b, s_c):
    qs = _stack_heads(q_ref[...])

    wg_o[...] = wg_ref[...].astype(bf16)
    wu_o[...] = wu_ref[...].astype(bf16)
    wd_o[...] = wd_ref[...].astype(bf16)

    def absorb(s, v):
        m_prev = m_sc[...]
        m_next = jnp.maximum(m_prev, jnp.max(s, axis=1, keepdims=True))
        alpha = jnp.exp2(m_prev - m_next)
        p = jnp.exp2(s - jnp.concatenate([m_next] * (s.shape[1] // HEAD_DIM), axis=1))
        acc_sc[...] = (jnp.concatenate([alpha, alpha], axis=1) * acc_sc[...]
                       + jnp.dot(p.astype(bf16), v, preferred_element_type=f32))
        m_sc[...] = m_next

    def chunk(ref, c):
        return ref[pl.ds(pl.multiple_of(c * TK, TK), TK), :]

    m_sc[...] = jnp.full_like(m_sc, -jnp.inf)
    acc_sc[...] = jnp.zeros_like(acc_sc)
    s_c[...] = _scores(qs, kc_ref[...])
    s_a[...] = _scores(qs, chunk(kl_ref, 0))

    def body(c2, carry):
        c = 2 * c2
        s_b[...] = _scores(qs, chunk(kl_ref, c + 1))
        absorb(s_a[...], chunk(vl_ref, c))
        s_a[...] = _scores(qs, chunk(kl_ref, c + 2))
        absorb(s_b[...], chunk(vl_ref, c + 1))
        return carry
    lax.fori_loop(0, ATT_CHUNKS // 2 - 1, body, 0)

    s_b[...] = _scores(qs, chunk(kl_ref, ATT_CHUNKS - 1))
    absorb(s_a[...], chunk(vl_ref, ATT_CHUNKS - 2))
    absorb(s_b[...], chunk(vl_ref, ATT_CHUNKS - 1))
    absorb(s_c[...], vc_ref[...])

    acc = acc_sc[...]
    o = acc[:, :HEAD_DIM] / acc[:, HEAD_DIM:]
    for i in range(Q_GROUP):
        o_ref[:, i * HEAD_DIM:(i + 1) * HEAD_DIM] = o[i * TQ:(i + 1) * TQ].astype(bf16)


def _attn_ctx_kernel(q_ref, kc_ref, vc_ref, o_ref):
    s = _scores(_stack_heads(q_ref[...]), kc_ref[...])
    p = jnp.exp2(s - jnp.max(s, axis=1, keepdims=True))
    acc = jnp.dot(p.astype(bf16), vc_ref[...], preferred_element_type=f32)
    o = acc[:, :HEAD_DIM] / acc[:, HEAD_DIM:]
    for i in range(Q_GROUP):
        o_ref[:, i * HEAD_DIM:(i + 1) * HEAD_DIM] = o[i * CTX:(i + 1) * CTX].astype(bf16)


def _attention(q, k, v, wg, wu, wd, layer):
    ctx_blk = N_LAT // CTX
    steps = BATCH * N_KV * NQ_LAT
    n_layers = wg.shape[0]
    slabs = [w.reshape(n_layers * steps, -1, w.shape[-1]) for w in (wg, wu, wd)]

    def slab_spec(w, first):
        return pl.BlockSpec((None,) + w.shape[1:],
                            lambda b, kh, qi: (first + (b * N_KV + kh) * NQ_LAT + qi, 0, 0))

    def q_map(b, kh, qi):
        return (b * NQ_LAT + qi, kh)

    def lat_spec(width):
        return pl.BlockSpec((SEQ, width), lambda b, kh, qi: (b, kh))

    def ctx_spec(width):
        return pl.BlockSpec((CTX, width), lambda b, kh, qi: (ctx_blk + b, kh))

    rows = Q_GROUP * TQ
    out = pl.pallas_call(
        _attn_kernel,
        grid=(BATCH, N_KV, NQ_LAT),
        in_specs=[pl.BlockSpec((TQ, Q_GROUP * HEAD_DIM), q_map), lat_spec(HEAD_DIM), lat_spec(VW),
                  ctx_spec(HEAD_DIM), ctx_spec(VW)] + [slab_spec(w, layer * steps) for w in slabs],
        out_specs=[pl.BlockSpec((TQ, Q_GROUP * HEAD_DIM), q_map)] + [slab_spec(w, 0) for w in slabs],
        out_shape=[jax.ShapeDtypeStruct((N_LAT, N_HEADS * HEAD_DIM), bf16)]
        + [jax.ShapeDtypeStruct((steps,) + w.shape[1:], bf16) for w in slabs],
        scratch_shapes=[pltpu.VMEM((rows, HEAD_DIM), f32), pltpu.VMEM((rows, VW), f32),
                        pltpu.VMEM((rows, TK), f32), pltpu.VMEM((rows, TK), f32), pltpu.VMEM((rows, CTX), f32)],
        compiler_params=_params(("parallel", "parallel", "arbitrary")),
        name="attn_core",
    )(q, k, v, k, v, *slabs)
    return out[0], [o.reshape(w.shape[1:]) for o, w in zip(out[1:], (wg, wu, wd))]


def _attention_ctx(q, k, v):
    ctx_blk = N_LAT // CTX
    return pl.pallas_call(
        _attn_ctx_kernel,
        grid=(BATCH, N_KV),
        in_specs=[pl.BlockSpec((CTX, Q_GROUP * HEAD_DIM), lambda b, kh: (ctx_blk + b, kh)),
                  pl.BlockSpec((CTX, HEAD_DIM), lambda b, kh: (ctx_blk + b, kh)),
                  pl.BlockSpec((CTX, VW), lambda b, kh: (ctx_blk + b, kh))],
        out_specs=pl.BlockSpec((CTX, Q_GROUP * HEAD_DIM), lambda b, kh: (b, kh)),
        out_shape=jax.ShapeDtypeStruct((N_CTX, N_HEADS * HEAD_DIM), bf16),
        compiler_params=_params(("parallel", "parallel")),
        name="attn_ctx",
    )(q, k, v)


def _proj_res_kernel(y_ref, w_ref, h_ref, mod_ref, o_ref):
    y = jnp.dot(y_ref[...], w_ref[...], preferred_element_type=f32)
    o_ref[...] = h_ref[...] + mod_ref[2:3, :] * y


def _proj_res2_kernel(yl_ref, yc_ref, w_ref, h_ref, mod_ref, o_ref):
    y_in = jnp.where(pl.program_id(0) < NT_LAT, yl_ref[...], yc_ref[...])
    y = jnp.dot(y_in, w_ref[...], preferred_element_type=f32)
    o_ref[...] = h_ref[...] + mod_ref[2:3, :] * y


def _proj_res(y, w, h, mod, nt):
    kdim = y.shape[1]
    return pl.pallas_call(
        _proj_res_kernel,
        grid=(nt,),
        in_specs=[_tile_spec(kdim), _const_spec((kdim, D)), _tile_spec(D), _mod_spec()],
        out_specs=_tile_spec(D),
        out_shape=jax.ShapeDtypeStruct((nt * TM, D), f32),
        compiler_params=_params(("parallel",)),
        name="proj_res",
    )(y, w, h, mod)


def _proj_res2(y_lat, y_ctx, w, h, mod):
    kdim = y_lat.shape[1]
    return pl.pallas_call(
        _proj_res2_kernel,
        grid=(NT,),
        in_specs=[pl.BlockSpec((TM, kdim), lambda t: (jnp.minimum(t, NT_LAT - 1), 0)), _const_spec((TM, kdim)),
                  _const_spec((kdim, D)), _tile_spec(D), _mod_spec()],
        out_specs=_tile_spec(D),
        out_shape=jax.ShapeDtypeStruct((T, D), f32),
        compiler_params=_params(("parallel",)),
        name="proj_res2",
    )(y_lat, y_ctx, w, h, mod)


def _ffn_kernel(h_ref, mod_ref, g_ref, wg_ref, wu_ref, wd_ref, o_ref, xn_sc, acc_sc):
    f = pl.program_id(1)

    @pl.when(f == 0)
    def _():
        xn_sc[...] = _norm_mod(h_ref[...], g_ref[...], mod_ref[3:4, :], mod_ref[4:5, :]).astype(bf16)
        acc_sc[...] = jnp.zeros_like(acc_sc)

    x = xn_sc[...]
    gate = jnp.dot(x, wg_ref[...], preferred_element_type=f32)
    up = jnp.dot(x, wu_ref[...], preferred_element_type=f32)
    hid = (_silu(gate) * up).astype(bf16)
    acc_sc[...] += jnp.dot(hid, wd_ref[...], preferred_element_type=f32)

    @pl.when(f == NF - 1)
    def _():
        o_ref[...] = h_ref[...] + mod_ref[5:6, :] * acc_sc[...]


def _ffn(h, mod, g, wg, wu, wd, layer, nt):
    return pl.pallas_call(
        _ffn_kernel,
        grid=(nt, NF),
        in_specs=[
            pl.BlockSpec((TM, D), lambda t, f: (t, 0)),
            pl.BlockSpec((None, 8, D), lambda t, f: (_mod_row(t), 0, 0)),
            pl.BlockSpec((1, D), lambda t, f: (0, 0)),
            pl.BlockSpec((None, D, TF), lambda t, f: (layer, 0, f)),
            pl.BlockSpec((None, D, TF), lambda t, f: (layer, 0, f)),
            pl.BlockSpec((None, TF, D), lambda t, f: (layer, f, 0)),
        ],
        out_specs=pl.BlockSpec((TM, D), lambda t, f: (t, 0)),
        out_shape=jax.ShapeDtypeStruct((nt * TM, D), f32),
        scratch_shapes=[pltpu.VMEM((TM, D), bf16), pltpu.VMEM((TM, D), f32)],
        compiler_params=_params(("parallel", "arbitrary")),
        name="ffn_dense",
    )(h, mod, g, wg, wu, wd)


def _split_bf16(x):
    hi = x.astype(bf16)
    lo = (x - hi.astype(f32)).astype(bf16)
    return hi, lo


def _split3(x):
    p0 = x.astype(bf16)
    r = x - p0.astype(f32)
    p1 = r.astype(bf16)
    p2 = (r - p1.astype(f32)).astype(bf16)
    return p0, p1, p2


def _dot3(a_parts, b):
    out = jnp.dot(a_parts[0], b, preferred_element_type=f32)
    for a in a_parts[1:]:
        out = out + jnp.dot(a, b, preferred_element_type=f32)
    return out


RT = 256
RT_PER_SAMPLE = SEQ // RT
ALIGN = 16
WIN = RT + ALIGN
FT = 512
HEAD_ROWS = 128
SLAB = 64
STAGE = FT + -(-WIN // SLAB) * SLAB
CAP = 17408
CAP_BLOCKS = CAP // FT
DN_T = (((0,), (0,)), ((), ()))
NO_ROW = -1e9


def _route_kernel(h_ref, mod_ref, g_ref, r_ref, xs_hbm, gs_hbm, rank_ref, seg_ref, cnt_ref,
                  x_stage, g_stage, cnt_sc, pend_sc, dst_sc, sems, *, n_rt):
    i = pl.program_id(0)

    @pl.when(i == 0)
    def _():
        x_stage[...] = jnp.zeros_like(x_stage)
        g_stage[...] = jnp.zeros_like(g_stage)
        for e in range(N_EXPERTS):
            cnt_sc[e] = 0
            pend_sc[e] = 0

    x = _norm_mod(h_ref[...], g_ref[...], mod_ref[3:4, :], mod_ref[4:5, :])
    xh, xl = _split_bf16(x)
    rh, rl = _split_bf16(r_ref[...])
    logits = (jnp.dot(xh, rh, preferred_element_type=f32) + jnp.dot(xl, rh, preferred_element_type=f32)
              + jnp.dot(xh, rl, preferred_element_type=f32))
    lane = lax.broadcasted_iota(jnp.int32, (RT, 128), 1)
    lg = jnp.where(lane < N_EXPERTS, logits, -jnp.inf)
    m1 = jnp.max(lg, axis=1, keepdims=True)
    i1 = jnp.min(jnp.where(lg == m1, lane, 128), axis=1, keepdims=True)
    lg2 = jnp.where(lane == i1, -jnp.inf, lg)
    m2 = jnp.max(lg2, axis=1, keepdims=True)
    i2 = jnp.min(jnp.where(lg2 == m2, lane, 128), axis=1, keepdims=True)
    e2 = jnp.exp(m2 - m1)
    den = 1.0 + e2
    gates = jnp.where(lane == i1, 1.0 / den, 0.0) + jnp.where(lane == i2, e2 / den, 0.0)
    used = jnp.where((lane == i1) | (lane == i2), 1.0, 0.0).astype(bf16)

    tp = lax.broadcasted_iota(jnp.int32, (RT, RT), 0)
    tt = lax.broadcasted_iota(jnp.int32, (RT, RT), 1)
    earlier = jnp.where(tp < tt, 1.0, 0.0).astype(bf16)
    eye = jnp.where(tp == tt, 1.0, 0.0).astype(bf16)
    rank_t = lax.dot_general(used, earlier, DN_T, preferred_element_type=f32)
    used_t = lax.dot_general(used, eye, DN_T, preferred_element_type=f32)
    rank_t = jnp.where(used_t > 0, rank_t, NO_ROW)
    rank_ref[...] = rank_t[0:N_EXPERTS]

    g_hi, g_lo = _split_bf16(gates)
    xg = jnp.concatenate([xh, g_hi, g_lo], axis=1)
    head_row = lax.broadcasted_iota(jnp.int32, (HEAD_ROWS, RT), 0).astype(f32)
    slab_row = lax.broadcasted_iota(jnp.int32, (SLAB, RT), 0).astype(f32)

    def flush_copies(e, done_rows):
        dst = pl.multiple_of(e * CAP + done_rows, FT)
        return (pltpu.make_async_copy(x_stage.at[e, pl.ds(0, FT), :], xs_hbm.at[pl.ds(dst, FT), :], sems.at[e, 0]),
                pltpu.make_async_copy(g_stage.at[e, pl.ds(0, FT), :], gs_hbm.at[pl.ds(dst, FT), :], sems.at[e, 1]))

    def settle(e):
        @pl.when(pend_sc[e] == 1)
        def _():
            for copy in flush_copies(e, dst_sc[e]):
                copy.wait()
            x_stage[e, 0:STAGE - FT, :] = x_stage[e, FT:STAGE, :]
            x_stage[e, STAGE - FT:STAGE, :] = jnp.zeros((FT, D), bf16)
            g_stage[e, 0:STAGE - FT, :] = g_stage[e, FT:STAGE, :]
            g_stage[e, STAGE - FT:STAGE, :] = jnp.zeros((FT, 128), f32)
            pend_sc[e] = 0

    for e in range(N_EXPERTS):
        settle(e)

    state = []
    n_of = []
    for e in range(N_EXPERTS):
        cnt = cnt_sc[e]
        seg_ref[i * N_EXPERTS + e] = cnt
        fill = cnt % FT
        start = pl.multiple_of((fill // ALIGN) * ALIGN, ALIGN)
        n_e = jnp.sum(used_t[e:e + 1, :]).astype(jnp.int32)
        cnt_sc[e] = cnt + n_e
        n_of.append(n_e)
        state.append((start, fill - start, fill + n_e >= FT, cnt - fill))

    def target_rows(e):
        return rank_t[e:e + 1, :] + state[e][1].astype(f32)

    for e in range(N_EXPERTS):
        start = state[e][0]
        p = jnp.where(head_row == target_rows(e), 1.0, 0.0).astype(bf16)
        rows = jnp.dot(p, xg, preferred_element_type=f32)
        grow = rows[:, D:D + 128] + rows[:, D + 128:]
        old = pl.ds(start, ALIGN)
        new = pl.ds(pl.multiple_of(start + ALIGN, ALIGN), HEAD_ROWS - ALIGN)
        x_stage[e, old, :] = (x_stage[e, old, :].astype(f32) + rows[:ALIGN, :D]).astype(bf16)
        x_stage[e, new, :] = rows[ALIGN:, :D].astype(bf16)
        g_stage[e, old, :] = g_stage[e, old, :] + grow[:ALIGN]
        g_stage[e, new, :] = grow[ALIGN:]

    crowded = False
    for e in range(N_EXPERTS):
        crowded = crowded | (state[e][1] + n_of[e] > HEAD_ROWS)

    @pl.when(crowded)
    def _():
        for e in range(N_EXPERTS):
            def place(sl, carry, e=e):
                p = jnp.where(slab_row + (sl * SLAB).astype(f32) == target_rows(e), 1.0, 0.0).astype(bf16)
                rows = jnp.dot(p, xg, preferred_element_type=f32)
                dst = pl.ds(pl.multiple_of(state[e][0] + sl * SLAB, ALIGN), SLAB)
                x_stage[e, dst, :] = rows[:, :D].astype(bf16)
                g_stage[e, dst, :] = rows[:, D:D + 128] + rows[:, D + 128:]
                return carry
            lax.fori_loop(HEAD_ROWS // SLAB, (state[e][1] + n_of[e] + SLAB - 1) // SLAB, place, 0)

    for e in range(N_EXPERTS):
        @pl.when(state[e][2])
        def _():
            for copy in flush_copies(e, state[e][3]):
                copy.start()
            pend_sc[e] = 1
            dst_sc[e] = state[e][3]

    @pl.when(i == n_rt - 1)
    def _():
        for e in range(N_EXPERTS):
            settle(e)
            c = cnt_sc[e]
            cnt_ref[e] = c
            for copy in flush_copies(e, c - c % FT):
                copy.start()
                copy.wait()


def _moe_route(h, mod, g, router, n_rt):
    return pl.pallas_call(
        functools.partial(_route_kernel, n_rt=n_rt),
        grid=(n_rt,),
        in_specs=[
            pl.BlockSpec((RT, D), lambda i: (i, 0)),
            pl.BlockSpec((None, 8, D), lambda i: (jnp.minimum(i // RT_PER_SAMPLE, CTX_MOD_ROW), 0, 0)),
            _const_spec((1, D)), _const_spec((D, 128)),
        ],
        out_specs=[
            pl.BlockSpec(memory_space=pl.ANY), pl.BlockSpec(memory_space=pl.ANY),
            pl.BlockSpec((None, N_EXPERTS, RT), lambda i: (i, 0, 0)),
            pl.BlockSpec(memory_space=pltpu.SMEM), pl.BlockSpec(memory_space=pltpu.SMEM),
        ],
        out_shape=[
            jax.ShapeDtypeStruct((N_EXPERTS * CAP, D), bf16),
            jax.ShapeDtypeStruct((N_EXPERTS * CAP, 128), f32),
            jax.ShapeDtypeStruct((n_rt, N_EXPERTS, RT), f32),
            jax.ShapeDtypeStruct((n_rt * N_EXPERTS,), jnp.int32),
            jax.ShapeDtypeStruct((N_EXPERTS,), jnp.int32),
        ],
        scratch_shapes=[
            pltpu.VMEM((N_EXPERTS, STAGE, D), bf16), pltpu.VMEM((N_EXPERTS, STAGE, 128), f32),
            pltpu.SMEM((N_EXPERTS,), jnp.int32), pltpu.SMEM((N_EXPERTS,), jnp.int32),
            pltpu.SMEM((N_EXPERTS,), jnp.int32), pltpu.SemaphoreType.DMA((N_EXPERTS, 2)),
        ],
        compiler_params=_params(("arbitrary",)),
        name="moe_route",
    )(h, mod, g, router)


def _moe_ffn_kernel(te_ref, tb_ref, nt_ref, x_ref, gs_ref, wg_ref, wu_ref, wd_ref, y_ref, acc_sc):
    j = pl.program_id(0)
    f = pl.program_id(1)

    @pl.when(j < nt_ref[0])
    def _():
        @pl.when(f == 0)
        def _():
            acc_sc[...] = jnp.zeros_like(acc_sc)

        x = x_ref[...]
        gate = jnp.dot(x, wg_ref[...], preferred_element_type=f32)
        up = jnp.dot(x, wu_ref[...], preferred_element_type=f32)
        hid = (_silu(gate) * up).astype(bf16)
        acc_sc[...] += jnp.dot(hid, wd_ref[...], preferred_element_type=f32)

        @pl.when(f == NF - 1)
        def _():
            lane = lax.broadcasted_iota(jnp.int32, (FT, 128), 1)
            ge = jnp.sum(jnp.where(lane == te_ref[j], gs_ref[...], 0.0), axis=1, keepdims=True)
            y_ref[...] = (ge * acc_sc[...]).astype(bf16)


def _moe_experts(xs, gs, wg, wu, wd, tile_expert, tile_block, n_tiles):
    max_tiles = tile_expert.shape[0]

    def f_eff(j, f, nt):
        return jnp.where(j < nt[0], f, NF - 1)

    grid_spec = pltpu.PrefetchScalarGridSpec(
        num_scalar_prefetch=3,
        grid=(max_tiles, NF),
        in_specs=[
            pl.BlockSpec((FT, D), lambda j, f, te, tb, nt: (tb[j], 0)),
            pl.BlockSpec((FT, 128), lambda j, f, te, tb, nt: (tb[j], 0)),
            pl.BlockSpec((None, D, TF), lambda j, f, te, tb, nt: (te[j], 0, f_eff(j, f, nt))),
            pl.BlockSpec((None, D, TF), lambda j, f, te, tb, nt: (te[j], 0, f_eff(j, f, nt))),
            pl.BlockSpec((None, TF, D), lambda j, f, te, tb, nt: (te[j], f_eff(j, f, nt), 0)),
        ],
        out_specs=pl.BlockSpec((FT, D), lambda j, f, te, tb, nt: (tb[j], 0)),
        scratch_shapes=[pltpu.VMEM((FT, D), f32)],
    )
    return pl.pallas_call(
        _moe_ffn_kernel,
        grid_spec=grid_spec,
        out_shape=jax.ShapeDtypeStruct((N_EXPERTS * CAP, D), bf16),
        compiler_params=_params(("arbitrary", "arbitrary")),
        name="moe_experts",
    )(tile_expert, tile_block, n_tiles, xs, gs, wg, wu, wd)


def _moe_combine_kernel(seg_ref, nseg_ref, wmax_ref, rank_ref, h_ref, mod_ref, fg_ref, ys_hbm, o_ref, ybuf, acc_sc,
                        sems, *, n_rt, apply_final):
    i = pl.program_id(0)
    main_row = lax.broadcasted_iota(jnp.int32, (RT, RT), 0).astype(f32)
    last_row = lax.broadcasted_iota(jnp.int32, (ALIGN, RT), 0).astype(f32) + RT

    def window(step, e):
        seg = seg_ref[step * N_EXPERTS + e]
        start = pl.multiple_of(jnp.minimum((seg // ALIGN) * ALIGN, wmax_ref[e]), ALIGN)
        return seg, start

    def window_copy(step, e):
        _, start = window(step, e)
        return pltpu.make_async_copy(ys_hbm.at[pl.ds(e * CAP + start, WIN), :], ybuf.at[step % 2, e],
                                     sems.at[step % 2, e])

    def fetch(step):
        for e in range(N_EXPERTS):
            @pl.when(nseg_ref[step * N_EXPERTS + e] > 0)
            def _():
                window_copy(step, e).start()

    @pl.when(i == 0)
    def _():
        ybuf[...] = jnp.zeros_like(ybuf)
        fetch(0)

    @pl.when(i + 1 < n_rt)
    def _():
        fetch(i + 1)

    acc = None
    spill = False
    for e in range(N_EXPERTS):
        seg, start = window(i, e)
        n = nseg_ref[i * N_EXPERTS + e]

        @pl.when(n > 0)
        def _():
            window_copy(i, e).wait()

        target = rank_ref[e:e + 1, :] + (seg - start).astype(f32)
        p = jnp.where(main_row == target, 1.0, 0.0).astype(bf16)
        part = lax.dot_general(p, ybuf[i % 2, e, 0:RT, :], DN_T, preferred_element_type=f32)
        acc = part if acc is None else acc + part
        spill = spill | (seg - start + n > RT)
    acc_sc[...] = acc

    @pl.when(spill)
    def _():
        for e in range(N_EXPERTS):
            seg, start = window(i, e)
            target = rank_ref[e:e + 1, :] + (seg - start).astype(f32)
            p = jnp.where(last_row == target, 1.0, 0.0).astype(bf16)
            acc_sc[...] += lax.dot_general(p, ybuf[i % 2, e, RT:WIN, :], DN_T, preferred_element_type=f32)

    out = h_ref[...] + mod_ref[5:6, :] * acc_sc[...]
    if apply_final:
        out = out * lax.rsqrt(jnp.mean(out * out, axis=-1, keepdims=True) + EPS) * fg_ref[...]
    o_ref[...] = out


def _moe_combine(ys, rank, seg, nseg, wmax, h, mod, final_g, n_rt, apply_final):
    grid_spec = pltpu.PrefetchScalarGridSpec(
        num_scalar_prefetch=3,
        grid=(n_rt,),
        in_specs=[
            pl.BlockSpec((None, N_EXPERTS, RT), lambda i, *_: (i, 0, 0)),
            pl.BlockSpec((RT, D), lambda i, *_: (i, 0)),
            pl.BlockSpec((None, 8, D), lambda i, *_: (jnp.minimum(i // RT_PER_SAMPLE, CTX_MOD_ROW), 0, 0)),
            pl.BlockSpec((1, D), lambda i, *_: (0, 0)),
            pl.BlockSpec(memory_space=pl.ANY),
        ],
        out_specs=pl.BlockSpec((RT, D), lambda i, *_: (i, 0)),
        scratch_shapes=[pltpu.VMEM((2, N_EXPERTS, WIN, D), bf16), pltpu.VMEM((RT, D), f32),
                        pltpu.SemaphoreType.DMA((2, N_EXPERTS))],
    )
    return pl.pallas_call(
        functools.partial(_moe_combine_kernel, n_rt=n_rt, apply_final=apply_final),
        grid_spec=grid_spec,
        out_shape=jax.ShapeDtypeStruct((n_rt * RT, D), f32),
        compiler_params=_params(("arbitrary",)),
        name="moe_combine",
    )(seg, nseg, wmax, rank, h, mod, final_g, ys)


def _moe(h, mod, g, router, wg, wu, wd, nt, final_g, apply_final):
    n_rt = nt * (TM // RT)
    xs, gs, rank, seg, counts = _moe_route(h, mod, g, router, n_rt)
    tiles = (counts + FT - 1) // FT
    ends = jnp.cumsum(tiles)
    n_tiles = ends[-1]
    max_tiles = (2 * n_rt * RT) // FT + N_EXPERTS
    j = jnp.minimum(jnp.arange(max_tiles, dtype=jnp.int32), n_tiles - 1)
    tile_expert = jnp.sum((j[:, None] >= ends[None, :]).astype(jnp.int32), axis=1)
    tile_block = tile_expert * CAP_BLOCKS + j - (ends - tiles)[tile_expert]
    ys = _moe_experts(xs, gs, wg, wu, wd, tile_expert, tile_block, n_tiles.reshape(1))
    seg2 = seg.reshape(n_rt, N_EXPERTS)
    nseg = (jnp.concatenate([seg2[1:], counts[None, :]], axis=0) - seg2).reshape(-1)
    wmax = jnp.maximum(tiles * FT - WIN, 0)
    return _moe_combine(ys, rank, seg, nseg, wmax, h, mod, final_g, n_rt, apply_final)


def _ssd_in_kernel(h_ref, mod_ref, g_ref, wz_ref, wx_ref, wdt_ref, z_ref, x_ref, dt_ref):
    a = _norm_mod(h_ref[...], g_ref[...], mod_ref[0:1, :], mod_ref[1:2, :]).astype(bf16)
    for j in range(SSD_INNER // D):
        z_ref[:, j * D:(j + 1) * D] = jnp.dot(a, wz_ref[:, j * D:(j + 1) * D], preferred_element_type=f32).astype(bf16)
    for j in range(SSD_CONV_CH // D):
        x_ref[:, j * D:(j + 1) * D] = jnp.dot(a, wx_ref[:, j * D:(j + 1) * D], preferred_element_type=f32).astype(bf16)
    dt_ref[...] = jnp.dot(a, wdt_ref[...], preferred_element_type=f32)


def _ssd_in_proj(h, mod, g, wz, wx, wdt):
    return pl.pallas_call(
        _ssd_in_kernel,
        grid=(NT,),
        in_specs=[_tile_spec(D), _mod_spec(), _const_spec((1, D)), _const_spec((D, SSD_INNER)),
                  _const_spec((D, SSD_CONV_CH)), _const_spec((D, 256))],
        out_specs=[_tile_spec(SSD_INNER), _tile_spec(SSD_CONV_CH), _tile_spec(256)],
        out_shape=[jax.ShapeDtypeStruct((T, SSD_INNER), bf16), jax.ShapeDtypeStruct((T, SSD_CONV_CH), bf16),
                   jax.ShapeDtypeStruct((T, 256), f32)],
        compiler_params=_params(("parallel",)),
        name="ssd_in_proj",
    )(h, mod, g, wz, wx, wdt)


CONV_TM = 256
CONV_HALO = 16
CONV_TILES_PER_SAMPLE = SEQ // CONV_TM


def _conv_kernel(x_ref, prev_ref, next_ref, w_ref, b_ref, o_ref):
    i = pl.program_id(0)
    is_ctx = i >= N_LAT // CONV_TM
    seg_start = is_ctx | (i % CONV_TILES_PER_SAMPLE == 0)
    seg_end = is_ctx | (i % CONV_TILES_PER_SAMPLE == CONV_TILES_PER_SAMPLE - 1)
    x = x_ref[...].astype(f32)
    prev_row = jnp.where(seg_start, 0.0, prev_ref[CONV_HALO - 1:CONV_HALO, :].astype(f32))
    next_row = jnp.where(seg_end, 0.0, next_ref[0:1, :].astype(f32))
    row = lax.broadcasted_iota(jnp.int32, x.shape, 0)
    xm1 = jnp.where(row == 0, prev_row, pltpu.roll(x, 1, 0))
    xp1 = jnp.where(row == CONV_TM - 1, next_row, pltpu.roll(x, CONV_TM - 1, 0))
    y = w_ref[0:1, :] * xm1 + w_ref[1:2, :] * x + w_ref[2:3, :] * xp1 + b_ref[...]
    o_ref[...] = _silu(y).astype(bf16)


def _ssd_conv(xbc, conv_w, conv_b):
    n = T // CONV_TM
    per = CONV_TM // CONV_HALO
    last = T // CONV_HALO - 1
    return pl.pallas_call(
        _conv_kernel,
        grid=(n,),
        in_specs=[
            pl.BlockSpec((CONV_TM, SSD_CONV_CH), lambda i: (i, 0)),
            pl.BlockSpec((CONV_HALO, SSD_CONV_CH), lambda i: (jnp.maximum(i * per - 1, 0), 0)),
            pl.BlockSpec((CONV_HALO, SSD_CONV_CH), lambda i: (jnp.minimum((i + 1) * per, last), 0)),
            _const_spec((8, SSD_CONV_CH)), _const_spec((1, SSD_CONV_CH)),
        ],
        out_specs=pl.BlockSpec((CONV_TM, SSD_CONV_CH), lambda i: (i, 0)),
        out_shape=jax.ShapeDtypeStruct((T, SSD_CONV_CH), bf16),
        compiler_params=_params(("parallel",)),
        name="ssd_conv",
    )(xbc, xbc, xbc, conv_w, conv_b)


def _ssd_scan_kernel(xf_ref, bf_ref, cf_ref, dtf_ref, xb_ref, bb_ref, cb_ref, dtb_ref, bias_ref, alog_ref,
                     yf_ref, yb_ref, state_sc):
    @pl.when(pl.program_id(1) == 0)
    def _():
        state_sc[...] = jnp.zeros_like(state_sc)

    _ssd_chunk(0, xf_ref, bf_ref, cf_ref, dtf_ref, bias_ref, alog_ref, yf_ref, state_sc)
    _ssd_chunk(1, xb_ref, bb_ref, cb_ref, dtb_ref, bias_ref, alog_ref, yb_ref, state_sc)


def _ssd_chunk(d, x_ref, b_ref, c_ref, dt_ref, bias_ref, alog_ref, y_ref, state_sc):
    L = SSD_CHUNK
    li = lax.broadcasted_iota(jnp.int32, (L, L), 0)
    si = lax.broadcasted_iota(jnp.int32, (L, L), 1)
    causal = (si <= li) if d == 0 else (si >= li)
    tri = jnp.where(causal, 1.0, 0.0).astype(bf16)
    hi = lax.broadcasted_iota(jnp.int32, (128, SSD_INNER), 0)
    ci = lax.broadcasted_iota(jnp.int32, (128, SSD_INNER), 1)
    expand = jnp.where(ci // SSD_P == hi, 1.0, 0.0).astype(bf16)

    dt = jax.nn.softplus(dt_ref[...] + bias_ref[d])
    a_neg = -jnp.exp(alog_ref[d])
    da = dt * a_neg
    da_parts = _split3(da)
    cs = (jnp.dot(tri, da_parts[0], preferred_element_type=f32) + jnp.dot(tri, da_parts[1], preferred_element_type=f32)
          + jnp.dot(tri, da_parts[2], preferred_element_type=f32))
    cs_t = cs.T
    total = jnp.sum(da, axis=0, keepdims=True)

    e_out = jnp.exp(cs)
    e_in = jnp.exp(total - cs) * dt
    dt_x = jnp.dot(dt.astype(bf16), expand, preferred_element_type=f32).astype(bf16)
    e_in_x = jnp.dot(e_in.astype(bf16), expand, preferred_element_type=f32).astype(bf16)
    out_scale = jnp.dot(e_out.astype(bf16), expand, preferred_element_type=f32)
    chunk_decay = _dot3(_split_bf16(jnp.broadcast_to(jnp.exp(total), (8, 128))), expand)[0:1]

    x = x_ref[...]
    first_head = lax.broadcasted_iota(jnp.int32, (L, 2 * SSD_P), 1) < SSD_P
    xdt = x * dt_x
    xw = x * e_in_x

    for g in range(SSD_GROUPS):
        bg = b_ref[:, g * SSD_N:(g + 1) * SSD_N]
        cg = c_ref[:, g * SSD_N:(g + 1) * SSD_N]
        cb = lax.dot_general(cg, bg, (((1,), (1,)), ((), ())), preferred_element_type=f32)
        gsl = slice(g * SSD_HPG * SSD_P, (g + 1) * SSD_HPG * SSD_P)
        st = state_sc[d, g]
        y_off = jnp.dot(cg, st.astype(bf16), preferred_element_type=f32) * out_scale[:, gsl]
        def decay_matrix(hd):
            seg = cs[:, hd:hd + 1] - cs_t[hd:hd + 1, :]
            return (cb * jnp.exp(jnp.where(causal, seg, -jnp.inf))).astype(bf16)

        ys = []
        for r in range(0, SSD_HPG, 2):
            hd = g * SSD_HPG + r
            m2 = jnp.concatenate([decay_matrix(hd), decay_matrix(hd + 1)], axis=1)
            x2 = xdt[:, hd * SSD_P:(hd + 2) * SSD_P]
            rhs = jnp.concatenate([jnp.where(first_head, x2, 0), jnp.where(first_head, 0, x2)], axis=0)
            ys.append(jnp.dot(m2, rhs, preferred_element_type=f32))
        y_ref[:, gsl] = (jnp.concatenate(ys, axis=1) + y_off).astype(bf16)
        new = lax.dot_general(bg, xw[:, gsl], (((0,), (0,)), ((), ())), preferred_element_type=f32)
        state_sc[d, g] = st * chunk_decay[:, gsl] + new


SSD_NCHUNK = (SEQ + CTX) // SSD_CHUNK
SSD_CTX_CHUNKS = CTX // SSD_CHUNK
SSD_LAT_CHUNKS = SEQ // SSD_CHUNK


def _ssd_chunk_block(d, b, c):
    ctx_j = c if d == 0 else SSD_CTX_CHUNKS - 1 - c
    lat_j = c - SSD_CTX_CHUNKS if d == 0 else SSD_LAT_CHUNKS - 1 - (c - SSD_CTX_CHUNKS)
    return jnp.where(c < SSD_CTX_CHUNKS, N_LAT // SSD_CHUNK + b * SSD_CTX_CHUNKS + ctx_j, b * SSD_LAT_CHUNKS + lat_j)


def _ssd_scan(xbc, dt, dt_bias, a_log):
    x_blk = SSD_INNER // SSD_BC

    def chunk_specs(d):
        return [
            pl.BlockSpec((SSD_CHUNK, SSD_INNER), lambda b, c: (_ssd_chunk_block(d, b, c), 0)),
            pl.BlockSpec((SSD_CHUNK, SSD_BC), lambda b, c: (_ssd_chunk_block(d, b, c), x_blk)),
            pl.BlockSpec((SSD_CHUNK, SSD_BC), lambda b, c: (_ssd_chunk_block(d, b, c), x_blk + 1)),
            pl.BlockSpec((SSD_CHUNK, 128), lambda b, c: (_ssd_chunk_block(d, b, c), d)),
        ]

    def y_spec(d):
        return pl.BlockSpec((SSD_CHUNK, SSD_INNER), lambda b, c: (_ssd_chunk_block(d, b, c), 0))

    return pl.pallas_call(
        _ssd_scan_kernel,
        grid=(BATCH, SSD_NCHUNK),
        in_specs=chunk_specs(0) + chunk_specs(1) + [_const_spec((2, 1, 128)), _const_spec((2, 1, 128))],
        out_specs=[y_spec(0), y_spec(1)],
        out_shape=[jax.ShapeDtypeStruct((T, SSD_INNER), bf16)] * 2,
        scratch_shapes=[pltpu.VMEM((2, SSD_GROUPS, SSD_N, SSD_HPG * SSD_P), f32)],
        compiler_params=_params(("parallel", "arbitrary")),
        name="ssd_scan",
    )(xbc, xbc, xbc, dt, xbc, xbc, xbc, dt, dt_bias, a_log)


def _ssd_out_kernel(yf_ref, yb_ref, x_ref, z_ref, dskip_ref, ng_ref, w_ref, h_ref, mod_ref, o_ref):
    y = yf_ref[...].astype(f32) + yb_ref[...].astype(f32) + x_ref[...].astype(f32) * dskip_ref[...]
    gated = y * _silu(z_ref[...].astype(f32))
    gw = SSD_INNER // SSD_GROUPS
    parts = []
    for g in range(SSD_GROUPS):
        s = gated[:, g * gw:(g + 1) * gw]
        parts.append(s * lax.rsqrt(jnp.mean(s * s, axis=-1, keepdims=True) + EPS))
    normed = (jnp.concatenate(parts, axis=1) * ng_ref[...]).astype(bf16)
    out = jnp.dot(normed, w_ref[...], preferred_element_type=f32)
    o_ref[...] = h_ref[...] + mod_ref[2:3, :] * out


def _ssd_out(yf, yb, xbc, z, dskip, ng, w_out, h, mod, nt):
    return pl.pallas_call(
        _ssd_out_kernel,
        grid=(nt,),
        in_specs=[
            _tile_spec(SSD_INNER), _tile_spec(SSD_INNER), _tile_spec(SSD_INNER),
            _tile_spec(SSD_INNER), _const_spec((1, SSD_INNER)), _const_spec((1, SSD_INNER)),
            _const_spec((SSD_INNER, D)), _tile_spec(D), _mod_spec(),
        ],
        out_specs=_tile_spec(D),
        out_shape=jax.ShapeDtypeStruct((nt * TM, D), f32),
        compiler_params=_params(("parallel",)),
        name="ssd_out",
    )(yf, yb, xbc, z, dskip, ng, w_out, h, mod)


def _gelu(x):
    return 0.5 * x * (1.0 + lax.erf(x * math.sqrt(0.5)))


def _cmlp_kernel(h_ref, mod_ref, g_ref, wu_ref, wv_ref, bu_ref, bv_ref, vg_ref, ws_ref, bs_ref, wo_ref, o_ref, uv_sc):
    a = _norm_mod(h_ref[...], g_ref[...], mod_ref[0:1, :], mod_ref[1:2, :]).astype(bf16)
    v = _gelu(jnp.dot(a, wv_ref[...], preferred_element_type=f32) + bv_ref[...])
    v = (v * lax.rsqrt(jnp.mean(v * v, axis=-1, keepdims=True) + EPS) * vg_ref[...]).astype(bf16)
    u = _gelu(jnp.dot(a, wu_ref[...], preferred_element_type=f32) + bu_ref[...])
    for ck in range(TM // CMLP_CHUNK):
        rows = slice(ck * CMLP_CHUNK, (ck + 1) * CMLP_CHUNK)
        for g in range(CMLP_GROUPS):
            cols = slice(g * CMLP_GW, (g + 1) * CMLP_GW)
            mixed = jnp.dot(ws_ref[g], v[rows, cols], preferred_element_type=f32) + bs_ref[:, cols]
            uv_sc[rows, cols] = (u[rows, cols] * mixed).astype(bf16)
    out = jnp.dot(uv_sc[...], wo_ref[...], preferred_element_type=f32)
    o_ref[...] = h_ref[...] + mod_ref[2:3, :] * out


def _cmlp(h, mod, g, wu, wv, bu, bv, vg, ws, bs, wo, nt):
    return pl.pallas_call(
        _cmlp_kernel,
        grid=(nt,),
        in_specs=[
            _tile_spec(D), _mod_spec(), _const_spec((1, D)), _const_spec((D, CMLP_D)), _const_spec((D, CMLP_D)),
            _const_spec((1, CMLP_D)), _const_spec((1, CMLP_D)), _const_spec((1, CMLP_D)),
            _const_spec((CMLP_GROUPS, CMLP_CHUNK, CMLP_CHUNK)), _const_spec((CMLP_CHUNK, CMLP_D)),
            _const_spec((CMLP_D, D)),
        ],
        out_specs=_tile_spec(D),
        out_shape=jax.ShapeDtypeStruct((nt * TM, D), f32),
        scratch_shapes=[pltpu.VMEM((TM, CMLP_D), bf16)],
        compiler_params=_params(("parallel",)),
        name="cmlp",
    )(h, mod, g, wu, wv, bu, bv, vg, ws, bs, wo)


def _rope_tables():
    pos = np.arange(SEQ)
    inv_freq = (1.0 / (np.float32(ROPE_THETA) ** (np.arange(0, ROPE_AXIS_DIM, 2, dtype=np.float32) / ROPE_AXIS_DIM)))
    inv_freq = inv_freq.astype(np.float32)
    ang_r = (pos // GRID_W).astype(np.float32)[:, None] * inv_freq
    ang_c = (pos % GRID_W).astype(np.float32)[:, None] * inv_freq
    cos = np.concatenate([np.cos(ang_r)] * 2 + [np.cos(ang_c)] * 2, axis=1)
    sin = np.concatenate([-np.sin(ang_r), np.sin(ang_r), -np.sin(ang_c), np.sin(ang_c)], axis=1)
    cos = np.concatenate([cos, np.ones((TM, HEAD_DIM), np.float32)], axis=0).astype(np.float32)
    sin = np.concatenate([sin, np.zeros((TM, HEAD_DIM), np.float32)], axis=0).astype(np.float32)
    return jnp.asarray(cos), jnp.asarray(sin)


def _row(v):
    return v.reshape(1, -1)


def _pad_lanes(v, width=128):
    return jnp.pad(v, ((0, 0), (0, width - v.shape[1])))


def kernel(x, c, ctx, c_ctx, w_mod, b_mod, norm1_g, norm2_g, attn_w_qkv, attn_q_g, attn_k_g, attn_w_o, ssd_w_in, ssd_conv_w, ssd_conv_b, ssd_dt_bias_f, ssd_dt_bias_b, ssd_a_log_f, ssd_a_log_b, ssd_d_skip, ssd_norm_g, ssd_w_out, cmlp_w_in, cmlp_b_in, cmlp_v_g, cmlp_w_s, cmlp_b_s, cmlp_w_out, ffn_w_gate, ffn_w_up, ffn_w_down, moe_router, moe_w_gate, moe_w_up, moe_w_down, final_g):
    h = jnp.concatenate([x.reshape(N_LAT, D), ctx.reshape(N_CTX, D)], axis=0)
    cond = jnp.concatenate([c, c_ctx[None, :], jnp.zeros((8 - BATCH - 1, D), f32)], axis=0)
    mods = _modulation(cond, w_mod, b_mod)
    cos_t, sin_t = _rope_tables()
    ffn_w = [w.astype(bf16) for w in (ffn_w_gate, ffn_w_up, ffn_w_down)]
    moe_w = None
    final_row = _row(final_g)

    for i in range(DEPTH):
        need_ctx = i < DEPTH - 1
        nt = NT if need_ctx else NT_LAT
        mod = mods[i]
        kind, j = i % 3, i // 3
        g1 = _row(norm1_g[i])
        if kind == 0:
            w_qkv = attn_w_qkv[j]
            w_partner = _rope_partner(w_qkv[:, :(N_HEADS + N_KV) * HEAD_DIM]).astype(bf16)
            q_g = jnp.stack([attn_q_g[j], _rope_partner(attn_q_g[j])])
            k_g = jnp.stack([attn_k_g[j], _rope_partner(attn_k_g[j])])
            q, k, v = _qkv_proj(h, mod, g1, w_qkv.astype(bf16), w_partner, q_g, k_g, cos_t, sin_t)
            moe_layer = i if i % 2 == 1 else i + 1
            nxt = moe_layer // 2
            o, moe_w = _attention(q, k, v, moe_w_gate, moe_w_up, moe_w_down, nxt)
            w_o = attn_w_o[j].astype(bf16)
            h = _proj_res2(o, _attention_ctx(q, k, v), w_o, h, mod) if need_ctx else _proj_res(o, w_o, h, mod, nt)
        elif kind == 1:
            w_in = ssd_w_in[j]
            wz = w_in[:, :SSD_INNER].astype(bf16)
            wx = w_in[:, SSD_INNER:SSD_INNER + SSD_CONV_CH].astype(bf16)
            w_dt = w_in[:, SSD_INNER + SSD_CONV_CH:]
            wdt = jnp.concatenate([_pad_lanes(w_dt[:, :SSD_HEADS]), _pad_lanes(w_dt[:, SSD_HEADS:])], axis=1).astype(bf16)
            z, xbc, dt = _ssd_in_proj(h, mod, g1, wz, wx, wdt)
            conv_w = jnp.pad(ssd_conv_w[j], ((0, 8 - ssd_conv_w.shape[1]), (0, 0)))
            xbc = _ssd_conv(xbc, conv_w, _row(ssd_conv_b[j]))
            dt_bias = jnp.stack([_pad_lanes(_row(ssd_dt_bias_f[j])), _pad_lanes(_row(ssd_dt_bias_b[j]))])
            a_log = jnp.stack([_pad_lanes(_row(ssd_a_log_f[j])), _pad_lanes(_row(ssd_a_log_b[j]))])
            yf, yb = _ssd_scan(xbc, dt, dt_bias, a_log)
            dskip = _row(jnp.repeat(ssd_d_skip[j], SSD_P))
            h = _ssd_out(yf, yb, xbc, z, dskip, _row(ssd_norm_g[j]), ssd_w_out[j].astype(bf16), h, mod, nt)
        else:
            w_in = cmlp_w_in[j]
            b_in = cmlp_b_in[j]
            bs = jnp.repeat(cmlp_b_s[j].T, CMLP_GW, axis=1)
            h = _cmlp(h, mod, g1, w_in[:, :CMLP_D].astype(bf16), w_in[:, CMLP_D:].astype(bf16),
                      _row(b_in[:CMLP_D]), _row(b_in[CMLP_D:]), _row(cmlp_v_g[j]), cmlp_w_s[j].astype(bf16), bs,
                      cmlp_w_out[j].astype(bf16), nt)
        kk = i // 2
        g2 = _row(norm2_g[i])
        if i % 2 == 0:
            h = _ffn(h, mod, g2, *ffn_w, kk, nt)
        else:
            h = _moe(h, mod, g2, _pad_lanes(moe_router[kk]), *moe_w, nt, final_row, i == DEPTH - 1)
    return h.reshape(BATCH, SEQ, D)
```

```python
import functools
import math

import jax
import jax.numpy as jnp
import numpy as np
from jax import lax
from jax.experimental import pallas as pl
from jax.experimental.pallas import tpu as pltpu

f32 = jnp.float32
bf16 = jnp.bfloat16

D = 1024
BATCH = 2
SEQ = 8192
CTX = 256
DEPTH = 4
GRID_W = 64
EPS = 1e-6
N_MOD = 6

HEAD_DIM = 128
N_HEADS = 8
N_KV = 2
Q_GROUP = 4
ROPE_AXIS_DIM = 64
ROPE_THETA = 10000.0

SSD_INNER = 2048
SSD_P = 64
SSD_HEADS = 32
SSD_GROUPS = 4
SSD_HPG = 8
SSD_N = 128
SSD_CHUNK = 128
SSD_BC = SSD_GROUPS * SSD_N
SSD_CONV_CH = SSD_INNER + 2 * SSD_BC

CMLP_D = 2048
CMLP_GROUPS = 8
CMLP_GW = 256
CMLP_CHUNK = 128

D_FF = 3584
N_EXPERTS = 8

N_LAT = BATCH * SEQ
N_CTX = BATCH * CTX
T = N_LAT + N_CTX
TM = 512
NT = T // TM
NT_LAT = N_LAT // TM
TILES_PER_SAMPLE = SEQ // TM
CTX_MOD_ROW = BATCH

TF = 1792
NF = D_FF // TF

TQ = 512
TK = 512
ATT_CHUNKS = SEQ // TK
VW = 2 * HEAD_DIM
LOG2E = math.log2(math.e)
NQ_LAT = SEQ // TQ

VMEM_LIMIT = 56 * 1024 * 1024


def _mod_row(t):
    return jnp.minimum(t // TILES_PER_SAMPLE, CTX_MOD_ROW)


def _tile_spec(width):
    return pl.BlockSpec((TM, width), lambda t: (t, 0))


def _pair_specs(pair):
    lat, ctx = pair
    ctx_block = 0 if ctx.shape[0] == TM else NT_LAT
    return [pl.BlockSpec((TM, lat.shape[1]), lambda t: (jnp.minimum(t, NT_LAT - 1), 0)),
            pl.BlockSpec((TM, ctx.shape[1]), lambda t: (ctx_block, 0))]


def _pick_tile(lat_ref, ctx_ref):
    return jnp.where(pl.program_id(0) < NT_LAT, lat_ref[...], ctx_ref[...])


def _mod_spec():
    return pl.BlockSpec((None, 8, D), lambda t: (_mod_row(t), 0, 0))


def _const_spec(shape):
    n = len(shape)
    return pl.BlockSpec(shape, lambda *_: (0,) * n)


def _params(semantics):
    return pltpu.CompilerParams(dimension_semantics=semantics, vmem_limit_bytes=VMEM_LIMIT)


def _silu(x):
    return x * jax.nn.sigmoid(x)


def _norm_mod(x, g, shift, scale):
    y = x * lax.rsqrt(jnp.mean(x * x, axis=-1, keepdims=True) + EPS) * g
    return y * (1.0 + scale) + shift


def _mod_kernel(c_ref, w_ref, b_ref, o_ref):
    s = _silu(c_ref[...])
    o_ref[...] = jnp.dot(s.astype(bf16), w_ref[...].astype(bf16), preferred_element_type=f32) + b_ref[...]


def _modulation(cond, w_mod, b_mod):
    nblk = N_MOD * D // D
    out = pl.pallas_call(
        _mod_kernel,
        grid=(DEPTH, nblk),
        in_specs=[
            pl.BlockSpec((8, D), lambda i, j: (0, 0)),
            pl.BlockSpec((None, D, D), lambda i, j: (i, 0, j)),
            pl.BlockSpec((None, 1, D), lambda i, j: (i, 0, j)),
        ],
        out_specs=pl.BlockSpec((None, 8, D), lambda i, j: (i, 0, j)),
        out_shape=jax.ShapeDtypeStruct((DEPTH, 8, N_MOD * D), f32),
        compiler_params=_params(("arbitrary", "arbitrary")),
        name="modulation",
    )(cond, w_mod, b_mod.reshape(DEPTH, 1, N_MOD * D))
    mod = out[:, :3].reshape(DEPTH, 3, N_MOD, D)
    return jnp.pad(mod, ((0, 0), (0, 0), (0, 8 - N_MOD), (0, 0)))


def _qkv_kernel(hl_ref, hc_ref, mod_ref, g_ref, w_ref, wp_ref, qg_ref, kg_ref, cos_ref, sin_ref,
                q_ref, k_ref, v_ref):
    h = _pick_tile(hl_ref, hc_ref)
    a = _norm_mod(h, g_ref[...], mod_ref[0:1, :], mod_ref[1:2, :]).astype(bf16)
    qkv = jnp.dot(a, w_ref[...], preferred_element_type=f32)
    qkp = jnp.dot(a, wp_ref[...], preferred_element_type=f32)
    cos = cos_ref[...]
    sin = sin_ref[...]

    def head(x, xp, g, scale):
        r = lax.rsqrt(jnp.mean(x * x, axis=-1, keepdims=True) + EPS) * scale
        return (x * g[0:1, :] * cos + xp * g[1:2, :] * sin) * r

    qg = qg_ref[...]
    kg = kg_ref[...]
    for hh in range(N_HEADS):
        sl = slice(hh * HEAD_DIM, (hh + 1) * HEAD_DIM)
        q_ref[:, sl] = head(qkv[:, sl], qkp[:, sl], qg, HEAD_DIM ** -0.5 * LOG2E).astype(bf16)
    for hh in range(N_KV):
        sl = slice(hh * HEAD_DIM, (hh + 1) * HEAD_DIM)
        src = slice((N_HEADS + hh) * HEAD_DIM, (N_HEADS + hh + 1) * HEAD_DIM)
        k_ref[:, sl] = head(qkv[:, src], qkp[:, src], kg, 1.0).astype(bf16)
        v_ref[:, hh * VW:hh * VW + HEAD_DIM] = qkv[:, (N_HEADS + N_KV + hh) * HEAD_DIM:(N_HEADS + N_KV + hh + 1) * HEAD_DIM].astype(bf16)
        v_ref[:, hh * VW + HEAD_DIM:(hh + 1) * VW] = jnp.ones((TM, HEAD_DIM), bf16)


def _rope_partner(w):
    lead = w.shape[:-1]
    w5 = w.reshape(lead + (-1, 2, 2, ROPE_AXIS_DIM // 2))
    return w5[..., ::-1, :].reshape(w.shape)


def _qkv_proj(h_pair, mod, g, w_qkv, w_partner, q_g, k_g, cos_t, sin_t):
    rope_spec = pl.BlockSpec((TM, HEAD_DIM), lambda t: (jnp.where(t < NT_LAT, t % TILES_PER_SAMPLE, TILES_PER_SAMPLE), 0))
    return pl.pallas_call(
        _qkv_kernel,
        grid=(NT,),
        in_specs=[
            *_pair_specs(h_pair), _mod_spec(), _const_spec((1, D)),
            _const_spec((D, (N_HEADS + 2 * N_KV) * HEAD_DIM)), _const_spec((D, (N_HEADS + N_KV) * HEAD_DIM)),
            _const_spec((2, HEAD_DIM)), _const_spec((2, HEAD_DIM)), rope_spec, rope_spec,
        ],
        out_specs=[_tile_spec(N_HEADS * HEAD_DIM), _tile_spec(N_KV * HEAD_DIM), _tile_spec(N_KV * VW)],
        out_shape=[
            jax.ShapeDtypeStruct((T, N_HEADS * HEAD_DIM), bf16),
            jax.ShapeDtypeStruct((T, N_KV * HEAD_DIM), bf16),
            jax.ShapeDtypeStruct((T, N_KV * VW), bf16),
        ],
        compiler_params=_params(("parallel",)),
        name="attn_qkv",
    )(*h_pair, mod, g, w_qkv, w_partner, q_g, k_g, cos_t, sin_t)


def _stack_heads(q):
    return jnp.concatenate([q[:, i * HEAD_DIM:(i + 1) * HEAD_DIM] for i in range(Q_GROUP)], axis=0)


def _scores(qs, k):
    return lax.dot_general(qs, k, (((1,), (1,)), ((), ())), preferred_element_type=f32)


def _attn_kernel(q_ref, kl_ref, vl_ref, kc_ref, vc_ref, wg_ref, wu_ref, wd_ref, o_ref, wg_o, wu_o, wd_o,
                 m_sc, acc_sc, s_a, s_b, s_c):
    qs = _stack_heads(q_ref[...])

    wg_o[...] = wg_ref[...].astype(bf16)
    wu_o[...] = wu_ref[...].astype(bf16)
    wd_o[...] = wd_ref[...].astype(bf16)

    def absorb(s, v):
        m_prev = m_sc[...]
        m_next = jnp.maximum(m_prev, jnp.max(s, axis=1, keepdims=True))
        alpha = jnp.exp2(m_prev - m_next)
        p = jnp.exp2(s - jnp.concatenate([m_next] * (s.shape[1] // HEAD_DIM), axis=1))
        acc_sc[...] = (jnp.concatenate([alpha, alpha], axis=1) * acc_sc[...]
                       + jnp.dot(p.astype(bf16), v, preferred_element_type=f32))
        m_sc[...] = m_next

    def chunk(ref, c):
        return ref[pl.ds(pl.multiple_of(c * TK, TK), TK), :]

    m_sc[...] = jnp.full_like(m_sc, -jnp.inf)
    acc_sc[...] = jnp.zeros_like(acc_sc)
    s_c[...] = _scores(qs, kc_ref[...])
    s_a[...] = _scores(qs, chunk(kl_ref, 0))

    def body(c2, carry):
        c = 2 * c2
        s_b[...] = _scores(qs, chunk(kl_ref, c + 1))
        absorb(s_a[...], chunk(vl_ref, c))
        s_a[...] = _scores(qs, chunk(kl_ref, c + 2))
        absorb(s_b[...], chunk(vl_ref, c + 1))
        return carry
    lax.fori_loop(0, ATT_CHUNKS // 2 - 1, body, 0)

    s_b[...] = _scores(qs, chunk(kl_ref, ATT_CHUNKS - 1))
    absorb(s_a[...], chunk(vl_ref, ATT_CHUNKS - 2))
    absorb(s_b[...], chunk(vl_ref, ATT_CHUNKS - 1))
    absorb(s_c[...], vc_ref[...])

    acc = acc_sc[...]
    o = acc[:, :HEAD_DIM] / acc[:, HEAD_DIM:]
    for i in range(Q_GROUP):
        o_ref[:, i * HEAD_DIM:(i + 1) * HEAD_DIM] = o[i * TQ:(i + 1) * TQ].astype(bf16)


def _attn_ctx_kernel(q_ref, kc_ref, vc_ref, o_ref):
    s = _scores(_stack_heads(q_ref[...]), kc_ref[...])
    p = jnp.exp2(s - jnp.max(s, axis=1, keepdims=True))
    acc = jnp.dot(p.astype(bf16), vc_ref[...], preferred_element_type=f32)
    o = acc[:, :HEAD_DIM] / acc[:, HEAD_DIM:]
    for i in range(Q_GROUP):
        o_ref[:, i * HEAD_DIM:(i + 1) * HEAD_DIM] = o[i * CTX:(i + 1) * CTX].astype(bf16)


def _attention(q, k, v, wg, wu, wd, layer):
    ctx_blk = N_LAT // CTX
    steps = BATCH * N_KV * NQ_LAT
    n_layers = wg.shape[0]
    slabs = [w.reshape(n_layers * steps, -1, w.shape[-1]) for w in (wg, wu, wd)]

    def slab_spec(w, first):
        return pl.BlockSpec((None,) + w.shape[1:],
                            lambda b, kh, qi: (first + (b * N_KV + kh) * NQ_LAT + qi, 0, 0))

    def q_map(b, kh, qi):
        return (b * NQ_LAT + qi, kh)

    def lat_spec(width):
        return pl.BlockSpec((SEQ, width), lambda b, kh, qi: (b, kh))

    def ctx_spec(width):
        return pl.BlockSpec((CTX, width), lambda b, kh, qi: (ctx_blk + b, kh))

    rows = Q_GROUP * TQ
    out = pl.pallas_call(
        _attn_kernel,
        grid=(BATCH, N_KV, NQ_LAT),
        in_specs=[pl.BlockSpec((TQ, Q_GROUP * HEAD_DIM), q_map), lat_spec(HEAD_DIM), lat_spec(VW),
                  ctx_spec(HEAD_DIM), ctx_spec(VW)] + [slab_spec(w, layer * steps) for w in slabs],
        out_specs=[pl.BlockSpec((TQ, Q_GROUP * HEAD_DIM), q_map)] + [slab_spec(w, 0) for w in slabs],
        out_shape=[jax.ShapeDtypeStruct((N_LAT, N_HEADS * HEAD_DIM), bf16)]
        + [jax.ShapeDtypeStruct((steps,) + w.shape[1:], bf16) for w in slabs],
        scratch_shapes=[pltpu.VMEM((rows, HEAD_DIM), f32), pltpu.VMEM((rows, VW), f32),
                        pltpu.VMEM((rows, TK), f32), pltpu.VMEM((rows, TK), f32), pltpu.VMEM((rows, CTX), f32)],
        compiler_params=_params(("parallel", "parallel", "arbitrary")),
        name="attn_core",
    )(q, k, v, k, v, *slabs)
    return out[0], [o.reshape(w.shape[1:]) for o, w in zip(out[1:], (wg, wu, wd))]


def _attention_ctx(q, k, v):
    ctx_blk = N_LAT // CTX
    return pl.pallas_call(
        _attn_ctx_kernel,
        grid=(BATCH, N_KV),
        in_specs=[pl.BlockSpec((CTX, Q_GROUP * HEAD_DIM), lambda b, kh: (ctx_blk + b, kh)),
                  pl.BlockSpec((CTX, HEAD_DIM), lambda b, kh: (ctx_blk + b, kh)),
                  pl.BlockSpec((CTX, VW), lambda b, kh: (ctx_blk + b, kh))],
        out_specs=pl.BlockSpec((CTX, Q_GROUP * HEAD_DIM), lambda b, kh: (b, kh)),
        out_shape=jax.ShapeDtypeStruct((N_CTX, N_HEADS * HEAD_DIM), bf16),
        compiler_params=_params(("parallel", "parallel")),
        name="attn_ctx",
    )(q, k, v)


def _proj_res_kernel(y_ref, w_ref, h_ref, mod_ref, o_ref):
    y = jnp.dot(y_ref[...], w_ref[...], preferred_element_type=f32)
    o_ref[...] = h_ref[...] + mod_ref[2:3, :] * y


def _proj_res2_kernel(yl_ref, yc_ref, w_ref, hl_ref, hc_ref, mod_ref, o_ref):
    y = jnp.dot(_pick_tile(yl_ref, yc_ref), w_ref[...], preferred_element_type=f32)
    o_ref[...] = _pick_tile(hl_ref, hc_ref) + mod_ref[2:3, :] * y


def _proj_res(y, w, h, mod, nt):
    kdim = y.shape[1]
    return pl.pallas_call(
        _proj_res_kernel,
        grid=(nt,),
        in_specs=[_tile_spec(kdim), _const_spec((kdim, D)), _tile_spec(D), _mod_spec()],
        out_specs=_tile_spec(D),
        out_shape=jax.ShapeDtypeStruct((nt * TM, D), f32),
        compiler_params=_params(("parallel",)),
        name="proj_res",
    )(y, w, h, mod)


def _proj_res2(y_pair, w, h_pair, mod):
    kdim = y_pair[0].shape[1]
    return pl.pallas_call(
        _proj_res2_kernel,
        grid=(NT,),
        in_specs=[*_pair_specs(y_pair), _const_spec((kdim, D)), *_pair_specs(h_pair), _mod_spec()],
        out_specs=_tile_spec(D),
        out_shape=jax.ShapeDtypeStruct((T, D), f32),
        compiler_params=_params(("parallel",)),
        name="proj_res2",
    )(*y_pair, w, *h_pair, mod)


def _ffn_kernel(h_ref, mod_ref, g_ref, wg_ref, wu_ref, wd_ref, o_ref, xn_sc, acc_sc):
    f = pl.program_id(1)

    @pl.when(f == 0)
    def _():
        xn_sc[...] = _norm_mod(h_ref[...], g_ref[...], mod_ref[3:4, :], mod_ref[4:5, :]).astype(bf16)
        acc_sc[...] = jnp.zeros_like(acc_sc)

    x = xn_sc[...]
    gate = jnp.dot(x, wg_ref[...], preferred_element_type=f32)
    up = jnp.dot(x, wu_ref[...], preferred_element_type=f32)
    hid = (_silu(gate) * up).astype(bf16)
    acc_sc[...] += jnp.dot(hid, wd_ref[...], preferred_element_type=f32)

    @pl.when(f == NF - 1)
    def _():
        o_ref[...] = h_ref[...] + mod_ref[5:6, :] * acc_sc[...]


def _ffn(h, mod, g, wg, wu, wd, layer, nt):
    return pl.pallas_call(
        _ffn_kernel,
        grid=(nt, NF),
        in_specs=[
            pl.BlockSpec((TM, D), lambda t, f: (t, 0)),
            pl.BlockSpec((None, 8, D), lambda t, f: (_mod_row(t), 0, 0)),
            pl.BlockSpec((1, D), lambda t, f: (0, 0)),
            pl.BlockSpec((None, D, TF), lambda t, f: (layer, 0, f)),
            pl.BlockSpec((None, D, TF), lambda t, f: (layer, 0, f)),
            pl.BlockSpec((None, TF, D), lambda t, f: (layer, f, 0)),
        ],
        out_specs=pl.BlockSpec((TM, D), lambda t, f: (t, 0)),
        out_shape=jax.ShapeDtypeStruct((nt * TM, D), f32),
        scratch_shapes=[pltpu.VMEM((TM, D), bf16), pltpu.VMEM((TM, D), f32)],
        compiler_params=_params(("parallel", "arbitrary")),
        name="ffn_dense",
    )(h, mod, g, wg, wu, wd)


def _split_bf16(x):
    hi = x.astype(bf16)
    lo = (x - hi.astype(f32)).astype(bf16)
    return hi, lo


def _split3(x):
    p0 = x.astype(bf16)
    r = x - p0.astype(f32)
    p1 = r.astype(bf16)
    p2 = (r - p1.astype(f32)).astype(bf16)
    return p0, p1, p2


def _dot3(a_parts, b):
    out = jnp.dot(a_parts[0], b, preferred_element_type=f32)
    for a in a_parts[1:]:
        out = out + jnp.dot(a, b, preferred_element_type=f32)
    return out


RT = 256
RT_PER_SAMPLE = SEQ // RT
ALIGN = 16
WIN = RT + ALIGN
FT = 512
HEAD_ROWS = 128
SLAB = 64
STAGE = FT + -(-WIN // SLAB) * SLAB
CAP = 17408
CAP_BLOCKS = CAP // FT
DN_T = (((0,), (0,)), ((), ()))
NO_ROW = -1e9


def _route_kernel(h_ref, mod_ref, g_ref, r_ref, xs_hbm, gs_hbm, rank_ref, seg_ref, cnt_ref,
                  x_stage, g_stage, cnt_sc, pend_sc, dst_sc, sems, *, n_rt):
    i = pl.program_id(0)

    @pl.when(i == 0)
    def _():
        x_stage[...] = jnp.zeros_like(x_stage)
        g_stage[...] = jnp.zeros_like(g_stage)
        for e in range(N_EXPERTS):
            cnt_sc[e] = 0
            pend_sc[e] = 0

    x = _norm_mod(h_ref[...], g_ref[...], mod_ref[3:4, :], mod_ref[4:5, :])
    xh, xl = _split_bf16(x)
    rh, rl = _split_bf16(r_ref[...])
    logits = (jnp.dot(xh, rh, preferred_element_type=f32) + jnp.dot(xl, rh, preferred_element_type=f32)
              + jnp.dot(xh, rl, preferred_element_type=f32))
    lane = lax.broadcasted_iota(jnp.int32, (RT, 128), 1)
    lg = jnp.where(lane < N_EXPERTS, logits, -jnp.inf)
    m1 = jnp.max(lg, axis=1, keepdims=True)
    i1 = jnp.min(jnp.where(lg == m1, lane, 128), axis=1, keepdims=True)
    lg2 = jnp.where(lane == i1, -jnp.inf, lg)
    m2 = jnp.max(lg2, axis=1, keepdims=True)
    i2 = jnp.min(jnp.where(lg2 == m2, lane, 128), axis=1, keepdims=True)
    e2 = jnp.exp(m2 - m1)
    den = 1.0 + e2
    gates = jnp.where(lane == i1, 1.0 / den, 0.0) + jnp.where(lane == i2, e2 / den, 0.0)
    used = jnp.where((lane == i1) | (lane == i2), 1.0, 0.0).astype(bf16)

    tp = lax.broadcasted_iota(jnp.int32, (RT, RT), 0)
    tt = lax.broadcasted_iota(jnp.int32, (RT, RT), 1)
    earlier = jnp.where(tp < tt, 1.0, 0.0).astype(bf16)
    eye = jnp.where(tp == tt, 1.0, 0.0).astype(bf16)
    rank_t = lax.dot_general(used, earlier, DN_T, preferred_element_type=f32)
    used_t = lax.dot_general(used, eye, DN_T, preferred_element_type=f32)
    rank_t = jnp.where(used_t > 0, rank_t, NO_ROW)
    rank_ref[...] = rank_t[0:N_EXPERTS]

    g_hi, g_lo = _split_bf16(gates)
    xg = jnp.concatenate([xh, g_hi, g_lo], axis=1)
    head_row = lax.broadcasted_iota(jnp.int32, (HEAD_ROWS, RT), 0).astype(f32)
    slab_row = lax.broadcasted_iota(jnp.int32, (SLAB, RT), 0).astype(f32)

    def flush_copies(e, done_rows):
        dst = pl.multiple_of(e * CAP + done_rows, FT)
        return (pltpu.make_async_copy(x_stage.at[e, pl.ds(0, FT), :], xs_hbm.at[pl.ds(dst, FT), :], sems.at[e, 0]),
                pltpu.make_async_copy(g_stage.at[e, pl.ds(0, FT), :], gs_hbm.at[pl.ds(dst, FT), :], sems.at[e, 1]))

    def settle(e):
        @pl.when(pend_sc[e] == 1)
        def _():
            for copy in flush_copies(e, dst_sc[e]):
                copy.wait()
            x_stage[e, 0:STAGE - FT, :] = x_stage[e, FT:STAGE, :]
            x_stage[e, STAGE - FT:STAGE, :] = jnp.zeros((FT, D), bf16)
            g_stage[e, 0:STAGE - FT, :] = g_stage[e, FT:STAGE, :]
            g_stage[e, STAGE - FT:STAGE, :] = jnp.zeros((FT, 128), f32)
            pend_sc[e] = 0

    for e in range(N_EXPERTS):
        settle(e)

    state = []
    n_of = []
    for e in range(N_EXPERTS):
        cnt = cnt_sc[e]
        seg_ref[i * N_EXPERTS + e] = cnt
        fill = cnt % FT
        start = pl.multiple_of((fill // ALIGN) * ALIGN, ALIGN)
        n_e = jnp.sum(used_t[e:e + 1, :]).astype(jnp.int32)
        cnt_sc[e] = cnt + n_e
        n_of.append(n_e)
        state.append((start, fill - start, fill + n_e >= FT, cnt - fill))

    def target_rows(e):
        return rank_t[e:e + 1, :] + state[e][1].astype(f32)

    for e in range(N_EXPERTS):
        start = state[e][0]
        p = jnp.where(head_row == target_rows(e), 1.0, 0.0).astype(bf16)
        rows = jnp.dot(p, xg, preferred_element_type=f32)
        grow = rows[:, D:D + 128] + rows[:, D + 128:]
        old = pl.ds(start, ALIGN)
        new = pl.ds(pl.multiple_of(start + ALIGN, ALIGN), HEAD_ROWS - ALIGN)
        x_stage[e, old, :] = (x_stage[e, old, :].astype(f32) + rows[:ALIGN, :D]).astype(bf16)
        x_stage[e, new, :] = rows[ALIGN:, :D].astype(bf16)
        g_stage[e, old, :] = g_stage[e, old, :] + grow[:ALIGN]
        g_stage[e, new, :] = grow[ALIGN:]

    crowded = False
    for e in range(N_EXPERTS):
        crowded = crowded | (state[e][1] + n_of[e] > HEAD_ROWS)

    @pl.when(crowded)
    def _():
        for e in range(N_EXPERTS):
            def place(sl, carry, e=e):
                p = jnp.where(slab_row + (sl * SLAB).astype(f32) == target_rows(e), 1.0, 0.0).astype(bf16)
                rows = jnp.dot(p, xg, preferred_element_type=f32)
                dst = pl.ds(pl.multiple_of(state[e][0] + sl * SLAB, ALIGN), SLAB)
                x_stage[e, dst, :] = rows[:, :D].astype(bf16)
                g_stage[e, dst, :] = rows[:, D:D + 128] + rows[:, D + 128:]
                return carry
            lax.fori_loop(HEAD_ROWS // SLAB, (state[e][1] + n_of[e] + SLAB - 1) // SLAB, place, 0)

    for e in range(N_EXPERTS):
        @pl.when(state[e][2])
        def _():
            for copy in flush_copies(e, state[e][3]):
                copy.start()
            pend_sc[e] = 1
            dst_sc[e] = state[e][3]

    @pl.when(i == n_rt - 1)
    def _():
        for e in range(N_EXPERTS):
            settle(e)
            c = cnt_sc[e]
            cnt_ref[e] = c
            for copy in flush_copies(e, c - c % FT):
                copy.start()
                copy.wait()


def _moe_route(h, mod, g, router, n_rt):
    return pl.pallas_call(
        functools.partial(_route_kernel, n_rt=n_rt),
        grid=(n_rt,),
        in_specs=[
            pl.BlockSpec((RT, D), lambda i: (i, 0)),
            pl.BlockSpec((None, 8, D), lambda i: (jnp.minimum(i // RT_PER_SAMPLE, CTX_MOD_ROW), 0, 0)),
            _const_spec((1, D)), _const_spec((D, 128)),
        ],
        out_specs=[
            pl.BlockSpec(memory_space=pl.ANY), pl.BlockSpec(memory_space=pl.ANY),
            pl.BlockSpec((None, N_EXPERTS, RT), lambda i: (i, 0, 0)),
            pl.BlockSpec(memory_space=pltpu.SMEM), pl.BlockSpec(memory_space=pltpu.SMEM),
        ],
        out_shape=[
            jax.ShapeDtypeStruct((N_EXPERTS * CAP, D), bf16),
            jax.ShapeDtypeStruct((N_EXPERTS * CAP, 128), f32),
            jax.ShapeDtypeStruct((n_rt, N_EXPERTS, RT), f32),
            jax.ShapeDtypeStruct((n_rt * N_EXPERTS,), jnp.int32),
            jax.ShapeDtypeStruct((N_EXPERTS,), jnp.int32),
        ],
        scratch_shapes=[
            pltpu.VMEM((N_EXPERTS, STAGE, D), bf16), pltpu.VMEM((N_EXPERTS, STAGE, 128), f32),
            pltpu.SMEM((N_EXPERTS,), jnp.int32), pltpu.SMEM((N_EXPERTS,), jnp.int32),
            pltpu.SMEM((N_EXPERTS,), jnp.int32), pltpu.SemaphoreType.DMA((N_EXPERTS, 2)),
        ],
        compiler_params=_params(("arbitrary",)),
        name="moe_route",
    )(h, mod, g, router)


def _moe_ffn_kernel(te_ref, tb_ref, nt_ref, x_ref, gs_ref, wg_ref, wu_ref, wd_ref, y_ref, acc_sc):
    j = pl.program_id(0)
    f = pl.program_id(1)

    @pl.when(j < nt_ref[0])
    def _():
        @pl.when(f == 0)
        def _():
            acc_sc[...] = jnp.zeros_like(acc_sc)

        x = x_ref[...]
        gate = jnp.dot(x, wg_ref[...], preferred_element_type=f32)
        up = jnp.dot(x, wu_ref[...], preferred_element_type=f32)
        hid = (_silu(gate) * up).astype(bf16)
        acc_sc[...] += jnp.dot(hid, wd_ref[...], preferred_element_type=f32)

        @pl.when(f == NF - 1)
        def _():
            lane = lax.broadcasted_iota(jnp.int32, (FT, 128), 1)
            ge = jnp.sum(jnp.where(lane == te_ref[j], gs_ref[...], 0.0), axis=1, keepdims=True)
            y_ref[...] = (ge * acc_sc[...]).astype(bf16)


def _moe_experts(xs, gs, wg, wu, wd, tile_expert, tile_block, n_tiles):
    max_tiles = tile_expert.shape[0]

    def f_eff(j, f, nt):
        return jnp.where(j < nt[0], f, NF - 1)

    grid_spec = pltpu.PrefetchScalarGridSpec(
        num_scalar_prefetch=3,
        grid=(max_tiles, NF),
        in_specs=[
            pl.BlockSpec((FT, D), lambda j, f, te, tb, nt: (tb[j], 0)),
            pl.BlockSpec((FT, 128), lambda j, f, te, tb, nt: (tb[j], 0)),
            pl.BlockSpec((None, D, TF), lambda j, f, te, tb, nt: (te[j], 0, f_eff(j, f, nt))),
            pl.BlockSpec((None, D, TF), lambda j, f, te, tb, nt: (te[j], 0, f_eff(j, f, nt))),
            pl.BlockSpec((None, TF, D), lambda j, f, te, tb, nt: (te[j], f_eff(j, f, nt), 0)),
        ],
        out_specs=pl.BlockSpec((FT, D), lambda j, f, te, tb, nt: (tb[j], 0)),
        scratch_shapes=[pltpu.VMEM((FT, D), f32)],
    )
    return pl.pallas_call(
        _moe_ffn_kernel,
        grid_spec=grid_spec,
        out_shape=jax.ShapeDtypeStruct((N_EXPERTS * CAP, D), bf16),
        compiler_params=_params(("arbitrary", "arbitrary")),
        name="moe_experts",
    )(tile_expert, tile_block, n_tiles, xs, gs, wg, wu, wd)


def _moe_combine_kernel(seg_ref, nseg_ref, wmax_ref, rank_ref, h_ref, mod_ref, fg_ref, ys_hbm, o_ref, ybuf, acc_sc,
                        sems, *, n_rt, apply_final):
    i = pl.program_id(0)
    main_row = lax.broadcasted_iota(jnp.int32, (RT, RT), 0).astype(f32)
    last_row = lax.broadcasted_iota(jnp.int32, (ALIGN, RT), 0).astype(f32) + RT

    def window(step, e):
        seg = seg_ref[step * N_EXPERTS + e]
        start = pl.multiple_of(jnp.minimum((seg // ALIGN) * ALIGN, wmax_ref[e]), ALIGN)
        return seg, start

    def window_copy(step, e):
        _, start = window(step, e)
        return pltpu.make_async_copy(ys_hbm.at[pl.ds(e * CAP + start, WIN), :], ybuf.at[step % 2, e],
                                     sems.at[step % 2, e])

    def fetch(step):
        for e in range(N_EXPERTS):
            @pl.when(nseg_ref[step * N_EXPERTS + e] > 0)
            def _():
                window_copy(step, e).start()

    @pl.when(i == 0)
    def _():
        ybuf[...] = jnp.zeros_like(ybuf)
        fetch(0)

    @pl.when(i + 1 < n_rt)
    def _():
        fetch(i + 1)

    acc = None
    spill = False
    for e in range(N_EXPERTS):
        seg, start = window(i, e)
        n = nseg_ref[i * N_EXPERTS + e]

        @pl.when(n > 0)
        def _():
            window_copy(i, e).wait()

        target = rank_ref[e:e + 1, :] + (seg - start).astype(f32)
        p = jnp.where(main_row == target, 1.0, 0.0).astype(bf16)
        part = lax.dot_general(p, ybuf[i % 2, e, 0:RT, :], DN_T, preferred_element_type=f32)
        acc = part if acc is None else acc + part
        spill = spill | (seg - start + n > RT)
    acc_sc[...] = acc

    @pl.when(spill)
    def _():
        for e in range(N_EXPERTS):
            seg, start = window(i, e)
            target = rank_ref[e:e + 1, :] + (seg - start).astype(f32)
            p = jnp.where(last_row == target, 1.0, 0.0).astype(bf16)
            acc_sc[...] += lax.dot_general(p, ybuf[i % 2, e, RT:WIN, :], DN_T, preferred_element_type=f32)

    out = h_ref[...] + mod_ref[5:6, :] * acc_sc[...]
    if apply_final:
        out = out * lax.rsqrt(jnp.mean(out * out, axis=-1, keepdims=True) + EPS) * fg_ref[...]
    o_ref[...] = out


def _moe_combine(ys, rank, seg, nseg, wmax, h, mod, final_g, n_rt, apply_final):
    grid_spec = pltpu.PrefetchScalarGridSpec(
        num_scalar_prefetch=3,
        grid=(n_rt,),
        in_specs=[
            pl.BlockSpec((None, N_EXPERTS, RT), lambda i, *_: (i, 0, 0)),
            pl.BlockSpec((RT, D), lambda i, *_: (i, 0)),
            pl.BlockSpec((None, 8, D), lambda i, *_: (jnp.minimum(i // RT_PER_SAMPLE, CTX_MOD_ROW), 0, 0)),
            pl.BlockSpec((1, D), lambda i, *_: (0, 0)),
            pl.BlockSpec(memory_space=pl.ANY),
        ],
        out_specs=pl.BlockSpec((RT, D), lambda i, *_: (i, 0)),
        scratch_shapes=[pltpu.VMEM((2, N_EXPERTS, WIN, D), bf16), pltpu.VMEM((RT, D), f32),
                        pltpu.SemaphoreType.DMA((2, N_EXPERTS))],
    )
    return pl.pallas_call(
        functools.partial(_moe_combine_kernel, n_rt=n_rt, apply_final=apply_final),
        grid_spec=grid_spec,
        out_shape=jax.ShapeDtypeStruct((n_rt * RT, D), f32),
        compiler_params=_params(("arbitrary",)),
        name="moe_combine",
    )(seg, nseg, wmax, rank, h, mod, final_g, ys)


def _moe(h, mod, g, router, wg, wu, wd, nt, final_g, apply_final):
    n_rt = nt * (TM // RT)
    xs, gs, rank, seg, counts = _moe_route(h, mod, g, router, n_rt)
    tiles = (counts + FT - 1) // FT
    ends = jnp.cumsum(tiles)
    n_tiles = ends[-1]
    max_tiles = (2 * n_rt * RT) // FT + N_EXPERTS
    j = jnp.minimum(jnp.arange(max_tiles, dtype=jnp.int32), n_tiles - 1)
    tile_expert = jnp.sum((j[:, None] >= ends[None, :]).astype(jnp.int32), axis=1)
    tile_block = tile_expert * CAP_BLOCKS + j - (ends - tiles)[tile_expert]
    ys = _moe_experts(xs, gs, wg, wu, wd, tile_expert, tile_block, n_tiles.reshape(1))
    seg2 = seg.reshape(n_rt, N_EXPERTS)
    nseg = (jnp.concatenate([seg2[1:], counts[None, :]], axis=0) - seg2).reshape(-1)
    wmax = jnp.maximum(tiles * FT - WIN, 0)
    return _moe_combine(ys, rank, seg, nseg, wmax, h, mod, final_g, n_rt, apply_final)


HALO = 8
CONV_COLS = 256


def _ssd_in_kernel(h_ref, hp_ref, hn_ref, mod_ref, g_ref, wz_ref, wx_ref, wdt_ref, cw_ref, cb_ref,
                   z_ref, x_ref, dt_ref):
    t = pl.program_id(0)
    is_ctx = t >= NT_LAT
    seg_start = is_ctx | (t % TILES_PER_SAMPLE == 0)
    seg_end = is_ctx | (t % TILES_PER_SAMPLE == TILES_PER_SAMPLE - 1)
    shift, scale, g = mod_ref[0:1, :], mod_ref[1:2, :], g_ref[...]
    a = _norm_mod(h_ref[...], g, shift, scale).astype(bf16)
    halo = jnp.concatenate([hp_ref[...], hn_ref[...]], axis=0)
    a_halo = _norm_mod(halo, g, shift, scale).astype(bf16)

    for j in range(SSD_INNER // D):
        z_ref[:, j * D:(j + 1) * D] = jnp.dot(a, wz_ref[:, j * D:(j + 1) * D], preferred_element_type=f32).astype(bf16)
    dt_ref[...] = jnp.dot(a, wdt_ref[...], preferred_element_type=f32)

    cw = CONV_COLS
    row = lax.broadcasted_iota(jnp.int32, (TM, cw), 0)
    first = row == 0
    last = row == TM - 1
    no_prev = is_ctx & (row == CTX)
    no_next = is_ctx & (row == CTX - 1)
    for j in range(SSD_CONV_CH // cw):
        cols = slice(j * cw, (j + 1) * cw)
        x = jnp.dot(a, wx_ref[:, cols], preferred_element_type=f32)
        xh = jnp.dot(a_halo, wx_ref[:, cols], preferred_element_type=f32)
        prev_row = jnp.where(seg_start, 0.0, xh[HALO - 1:HALO, :])
        next_row = jnp.where(seg_end, 0.0, xh[HALO:HALO + 1, :])
        xm1 = jnp.where(first, prev_row, pltpu.roll(x, 1, 0))
        xp1 = jnp.where(last, next_row, pltpu.roll(x, TM - 1, 0))
        xm1 = jnp.where(no_prev, 0.0, xm1)
        xp1 = jnp.where(no_next, 0.0, xp1)
        y = cw_ref[0:1, cols] * xm1 + cw_ref[1:2, cols] * x + cw_ref[2:3, cols] * xp1 + cb_ref[:, cols]
        x_ref[:, cols] = _silu(y).astype(bf16)


def _ssd_in_proj(h, mod, g, wz, wx, wdt, conv_w, conv_b):
    per = TM // HALO
    last_blk = T // HALO - 1
    return pl.pallas_call(
        _ssd_in_kernel,
        grid=(NT,),
        in_specs=[_tile_spec(D),
                  pl.BlockSpec((HALO, D), lambda t: (jnp.maximum(t * per - 1, 0), 0)),
                  pl.BlockSpec((HALO, D), lambda t: (jnp.minimum((t + 1) * per, last_blk), 0)),
                  _mod_spec(), _const_spec((1, D)), _const_spec((D, SSD_INNER)),
                  _const_spec((D, SSD_CONV_CH)), _const_spec((D, 256)),
                  _const_spec((8, SSD_CONV_CH)), _const_spec((1, SSD_CONV_CH))],
        out_specs=[_tile_spec(SSD_INNER), _tile_spec(SSD_CONV_CH), _tile_spec(256)],
        out_shape=[jax.ShapeDtypeStruct((T, SSD_INNER), bf16), jax.ShapeDtypeStruct((T, SSD_CONV_CH), bf16),
                   jax.ShapeDtypeStruct((T, 256), f32)],
        compiler_params=_params(("parallel",)),
        name="ssd_in_proj",
    )(h, h, h, mod, g, wz, wx, wdt, conv_w, conv_b)


def _ssd_scan_kernel(xf_ref, bf_ref, cf_ref, dtf_ref, xb_ref, bb_ref, cb_ref, dtb_ref, bias_ref, alog_ref,
                     yf_ref, yb_ref, state_sc):
    @pl.when(pl.program_id(1) == 0)
    def _():
        state_sc[...] = jnp.zeros_like(state_sc)

    _ssd_chunk(0, xf_ref, bf_ref, cf_ref, dtf_ref, bias_ref, alog_ref, yf_ref, state_sc)
    _ssd_chunk(1, xb_ref, bb_ref, cb_ref, dtb_ref, bias_ref, alog_ref, yb_ref, state_sc)


def _ssd_chunk(d, x_ref, b_ref, c_ref, dt_ref, bias_ref, alog_ref, y_ref, state_sc):
    L = SSD_CHUNK
    li = lax.broadcasted_iota(jnp.int32, (L, L), 0)
    si = lax.broadcasted_iota(jnp.int32, (L, L), 1)
    causal = (si <= li) if d == 0 else (si >= li)
    tri = jnp.where(causal, 1.0, 0.0).astype(bf16)
    hi = lax.broadcasted_iota(jnp.int32, (128, SSD_INNER), 0)
    ci = lax.broadcasted_iota(jnp.int32, (128, SSD_INNER), 1)
    expand = jnp.where(ci // SSD_P == hi, 1.0, 0.0).astype(bf16)

    dt = jax.nn.softplus(dt_ref[...] + bias_ref[d])
    a_neg = -jnp.exp(alog_ref[d])
    da = dt * a_neg
    da_parts = _split3(da)
    cs = (jnp.dot(tri, da_parts[0], preferred_element_type=f32) + jnp.dot(tri, da_parts[1], preferred_element_type=f32)
          + jnp.dot(tri, da_parts[2], preferred_element_type=f32))
    cs_t = cs.T
    total = jnp.sum(da, axis=0, keepdims=True)

    e_out = jnp.exp(cs)
    e_in = jnp.exp(total - cs) * dt
    dt_x = jnp.dot(dt.astype(bf16), expand, preferred_element_type=f32).astype(bf16)
    e_in_x = jnp.dot(e_in.astype(bf16), expand, preferred_element_type=f32).astype(bf16)
    out_scale = jnp.dot(e_out.astype(bf16), expand, preferred_element_type=f32)
    chunk_decay = _dot3(_split_bf16(jnp.broadcast_to(jnp.exp(total), (8, 128))), expand)[0:1]

    x = x_ref[...]
    first_head = lax.broadcasted_iota(jnp.int32, (L, 2 * SSD_P), 1) < SSD_P
    xdt = x * dt_x
    xw = x * e_in_x

    for g in range(SSD_GROUPS):
        bg = b_ref[:, g * SSD_N:(g + 1) * SSD_N]
        cg = c_ref[:, g * SSD_N:(g + 1) * SSD_N]
        cb = lax.dot_general(cg, bg, (((1,), (1,)), ((), ())), preferred_element_type=f32)
        gsl = slice(g * SSD_HPG * SSD_P, (g + 1) * SSD_HPG * SSD_P)
        st = state_sc[d, g]
        y_off = jnp.dot(cg, st.astype(bf16), preferred_element_type=f32) * out_scale[:, gsl]
        def decay_matrix(hd):
            seg = cs[:, hd:hd + 1] - cs_t[hd:hd + 1, :]
            return (cb * jnp.exp(jnp.where(causal, seg, -jnp.inf))).astype(bf16)

        ys = []
        for r in range(0, SSD_HPG, 2):
            hd = g * SSD_HPG + r
            m2 = jnp.concatenate([decay_matrix(hd), decay_matrix(hd + 1)], axis=1)
            x2 = xdt[:, hd * SSD_P:(hd + 2) * SSD_P]
            rhs = jnp.concatenate([jnp.where(first_head, x2, 0), jnp.where(first_head, 0, x2)], axis=0)
            ys.append(jnp.dot(m2, rhs, preferred_element_type=f32))
        y_ref[:, gsl] = (jnp.concatenate(ys, axis=1) + y_off).astype(bf16)
        new = lax.dot_general(bg, xw[:, gsl], (((0,), (0,)), ((), ())), preferred_element_type=f32)
        state_sc[d, g] = st * chunk_decay[:, gsl] + new


SSD_NCHUNK = (SEQ + CTX) // SSD_CHUNK
SSD_CTX_CHUNKS = CTX // SSD_CHUNK
SSD_LAT_CHUNKS = SEQ // SSD_CHUNK


def _ssd_chunk_block(d, b, c):
    ctx_j = c if d == 0 else SSD_CTX_CHUNKS - 1 - c
    lat_j = c - SSD_CTX_CHUNKS if d == 0 else SSD_LAT_CHUNKS - 1 - (c - SSD_CTX_CHUNKS)
    return jnp.where(c < SSD_CTX_CHUNKS, N_LAT // SSD_CHUNK + b * SSD_CTX_CHUNKS + ctx_j, b * SSD_LAT_CHUNKS + lat_j)


def _ssd_scan(xbc, dt, dt_bias, a_log):
    x_blk = SSD_INNER // SSD_BC

    def chunk_specs(d):
        return [
            pl.BlockSpec((SSD_CHUNK, SSD_INNER), lambda b, c: (_ssd_chunk_block(d, b, c), 0)),
            pl.BlockSpec((SSD_CHUNK, SSD_BC), lambda b, c: (_ssd_chunk_block(d, b, c), x_blk)),
            pl.BlockSpec((SSD_CHUNK, SSD_BC), lambda b, c: (_ssd_chunk_block(d, b, c), x_blk + 1)),
            pl.BlockSpec((SSD_CHUNK, 128), lambda b, c: (_ssd_chunk_block(d, b, c), d)),
        ]

    def y_spec(d):
        return pl.BlockSpec((SSD_CHUNK, SSD_INNER), lambda b, c: (_ssd_chunk_block(d, b, c), 0))

    return pl.pallas_call(
        _ssd_scan_kernel,
        grid=(BATCH, SSD_NCHUNK),
        in_specs=chunk_specs(0) + chunk_specs(1) + [_const_spec((2, 1, 128)), _const_spec((2, 1, 128))],
        out_specs=[y_spec(0), y_spec(1)],
        out_shape=[jax.ShapeDtypeStruct((T, SSD_INNER), bf16)] * 2,
        scratch_shapes=[pltpu.VMEM((2, SSD_GROUPS, SSD_N, SSD_HPG * SSD_P), f32)],
        compiler_params=_params(("parallel", "arbitrary")),
        name="ssd_scan",
    )(xbc, xbc, xbc, dt, xbc, xbc, xbc, dt, dt_bias, a_log)


def _ssd_out_kernel(yf_ref, yb_ref, x_ref, z_ref, dskip_ref, ng_ref, w_ref, h_ref, mod_ref, o_ref):
    y = yf_ref[...].astype(f32) + yb_ref[...].astype(f32) + x_ref[...].astype(f32) * dskip_ref[...]
    gated = y * _silu(z_ref[...].astype(f32))
    gw = SSD_INNER // SSD_GROUPS
    parts = []
    for g in range(SSD_GROUPS):
        s = gated[:, g * gw:(g + 1) * gw]
        parts.append(s * lax.rsqrt(jnp.mean(s * s, axis=-1, keepdims=True) + EPS))
    normed = (jnp.concatenate(parts, axis=1) * ng_ref[...]).astype(bf16)
    out = jnp.dot(normed, w_ref[...], preferred_element_type=f32)
    o_ref[...] = h_ref[...] + mod_ref[2:3, :] * out


def _ssd_out(yf, yb, xbc, z, dskip, ng, w_out, h, mod, nt):
    return pl.pallas_call(
        _ssd_out_kernel,
        grid=(nt,),
        in_specs=[
            _tile_spec(SSD_INNER), _tile_spec(SSD_INNER), _tile_spec(SSD_INNER),
            _tile_spec(SSD_INNER), _const_spec((1, SSD_INNER)), _const_spec((1, SSD_INNER)),
            _const_spec((SSD_INNER, D)), _tile_spec(D), _mod_spec(),
        ],
        out_specs=_tile_spec(D),
        out_shape=jax.ShapeDtypeStruct((nt * TM, D), f32),
        compiler_params=_params(("parallel",)),
        name="ssd_out",
    )(yf, yb, xbc, z, dskip, ng, w_out, h, mod)


def _gelu(x):
    return 0.5 * x * (1.0 + lax.erf(x * math.sqrt(0.5)))


def _cmlp_kernel(h_ref, mod_ref, g_ref, wu_ref, wv_ref, bu_ref, bv_ref, vg_ref, ws_ref, bs_ref, wo_ref, o_ref, uv_sc):
    a = _norm_mod(h_ref[...], g_ref[...], mod_ref[0:1, :], mod_ref[1:2, :]).astype(bf16)
    v = _gelu(jnp.dot(a, wv_ref[...], preferred_element_type=f32) + bv_ref[...])
    v = (v * lax.rsqrt(jnp.mean(v * v, axis=-1, keepdims=True) + EPS) * vg_ref[...]).astype(bf16)
    u = _gelu(jnp.dot(a, wu_ref[...], preferred_element_type=f32) + bu_ref[...])
    for ck in range(TM // CMLP_CHUNK):
        rows = slice(ck * CMLP_CHUNK, (ck + 1) * CMLP_CHUNK)
        for g in range(CMLP_GROUPS):
            cols = slice(g * CMLP_GW, (g + 1) * CMLP_GW)
            mixed = jnp.dot(ws_ref[g], v[rows, cols], preferred_element_type=f32) + bs_ref[:, cols]
            uv_sc[rows, cols] = (u[rows, cols] * mixed).astype(bf16)
    out = jnp.dot(uv_sc[...], wo_ref[...], preferred_element_type=f32)
    o_ref[...] = h_ref[...] + mod_ref[2:3, :] * out


def _cmlp(h, mod, g, wu, wv, bu, bv, vg, ws, bs, wo, nt):
    return pl.pallas_call(
        _cmlp_kernel,
        grid=(nt,),
        in_specs=[
            _tile_spec(D), _mod_spec(), _const_spec((1, D)), _const_spec((D, CMLP_D)), _const_spec((D, CMLP_D)),
            _const_spec((1, CMLP_D)), _const_spec((1, CMLP_D)), _const_spec((1, CMLP_D)),
            _const_spec((CMLP_GROUPS, CMLP_CHUNK, CMLP_CHUNK)), _const_spec((CMLP_CHUNK, CMLP_D)),
            _const_spec((CMLP_D, D)),
        ],
        out_specs=_tile_spec(D),
        out_shape=jax.ShapeDtypeStruct((nt * TM, D), f32),
        scratch_shapes=[pltpu.VMEM((TM, CMLP_D), bf16)],
        compiler_params=_params(("parallel",)),
        name="cmlp",
    )(h, mod, g, wu, wv, bu, bv, vg, ws, bs, wo)


def _rope_tables():
    pos = np.arange(SEQ)
    inv_freq = (1.0 / (np.float32(ROPE_THETA) ** (np.arange(0, ROPE_AXIS_DIM, 2, dtype=np.float32) / ROPE_AXIS_DIM)))
    inv_freq = inv_freq.astype(np.float32)
    ang_r = (pos // GRID_W).astype(np.float32)[:, None] * inv_freq
    ang_c = (pos % GRID_W).astype(np.float32)[:, None] * inv_freq
    cos = np.concatenate([np.cos(ang_r)] * 2 + [np.cos(ang_c)] * 2, axis=1)
    sin = np.concatenate([-np.sin(ang_r), np.sin(ang_r), -np.sin(ang_c), np.sin(ang_c)], axis=1)
    cos = np.concatenate([cos, np.ones((TM, HEAD_DIM), np.float32)], axis=0).astype(np.float32)
    sin = np.concatenate([sin, np.zeros((TM, HEAD_DIM), np.float32)], axis=0).astype(np.float32)
    return jnp.asarray(cos), jnp.asarray(sin)


def _row(v):
    return v.reshape(1, -1)


def _pad_lanes(v, width=128):
    return jnp.pad(v, ((0, 0), (0, width - v.shape[1])))


def kernel(x, c, ctx, c_ctx, w_mod, b_mod, norm1_g, norm2_g, attn_w_qkv, attn_q_g, attn_k_g, attn_w_o, ssd_w_in, ssd_conv_w, ssd_conv_b, ssd_dt_bias_f, ssd_dt_bias_b, ssd_a_log_f, ssd_a_log_b, ssd_d_skip, ssd_norm_g, ssd_w_out, cmlp_w_in, cmlp_b_in, cmlp_v_g, cmlp_w_s, cmlp_b_s, cmlp_w_out, ffn_w_gate, ffn_w_up, ffn_w_down, moe_router, moe_w_gate, moe_w_up, moe_w_down, final_g):
    h = None
    h_pair = (x.reshape(N_LAT, D), ctx.reshape(N_CTX, D))
    cond = jnp.concatenate([c, c_ctx[None, :], jnp.zeros((8 - BATCH - 1, D), f32)], axis=0)
    mods = _modulation(cond, w_mod, b_mod)
    cos_t, sin_t = _rope_tables()
    ffn_w = [w.astype(bf16) for w in (ffn_w_gate, ffn_w_up, ffn_w_down)]
    moe_w = None
    final_row = _row(final_g)

    for i in range(DEPTH):
        need_ctx = i < DEPTH - 1
        nt = NT if need_ctx else NT_LAT
        mod = mods[i]
        kind, j = i % 3, i // 3
        g1 = _row(norm1_g[i])
        if kind == 0:
            w_qkv = attn_w_qkv[j]
            w_partner = _rope_partner(w_qkv[:, :(N_HEADS + N_KV) * HEAD_DIM]).astype(bf16)
            q_g = jnp.stack([attn_q_g[j], _rope_partner(attn_q_g[j])])
            k_g = jnp.stack([attn_k_g[j], _rope_partner(attn_k_g[j])])
            q, k, v = _qkv_proj(h_pair, mod, g1, w_qkv.astype(bf16), w_partner, q_g, k_g, cos_t, sin_t)
            moe_layer = i if i % 2 == 1 else i + 1
            nxt = moe_layer // 2
            o, moe_w = _attention(q, k, v, moe_w_gate, moe_w_up, moe_w_down, nxt)
            w_o = attn_w_o[j].astype(bf16)
            if need_ctx:
                h = _proj_res2((o, _attention_ctx(q, k, v)), w_o, h_pair, mod)
            else:
                h = _proj_res(o, w_o, h, mod, nt)
        elif kind == 1:
            w_in = ssd_w_in[j]
            wz = w_in[:, :SSD_INNER].astype(bf16)
            wx = w_in[:, SSD_INNER:SSD_INNER + SSD_CONV_CH].astype(bf16)
            w_dt = w_in[:, SSD_INNER + SSD_CONV_CH:]
            wdt = jnp.concatenate([_pad_lanes(w_dt[:, :SSD_HEADS]), _pad_lanes(w_dt[:, SSD_HEADS:])], axis=1).astype(bf16)
            conv_w = jnp.pad(ssd_conv_w[j], ((0, 8 - ssd_conv_w.shape[1]), (0, 0)))
            z, xbc, dt = _ssd_in_proj(h, mod, g1, wz, wx, wdt, conv_w, _row(ssd_conv_b[j]))
            dt_bias = jnp.stack([_pad_lanes(_row(ssd_dt_bias_f[j])), _pad_lanes(_row(ssd_dt_bias_b[j]))])
            a_log = jnp.stack([_pad_lanes(_row(ssd_a_log_f[j])), _pad_lanes(_row(ssd_a_log_b[j]))])
            yf, yb = _ssd_scan(xbc, dt, dt_bias, a_log)
            dskip = _row(jnp.repeat(ssd_d_skip[j], SSD_P))
            h = _ssd_out(yf, yb, xbc, z, dskip, _row(ssd_norm_g[j]), ssd_w_out[j].astype(bf16), h, mod, nt)
        else:
            w_in = cmlp_w_in[j]
            b_in = cmlp_b_in[j]
            bs = jnp.repeat(cmlp_b_s[j].T, CMLP_GW, axis=1)
            h = _cmlp(h, mod, g1, w_in[:, :CMLP_D].astype(bf16), w_in[:, CMLP_D:].astype(bf16),
                      _row(b_in[:CMLP_D]), _row(b_in[CMLP_D:]), _row(cmlp_v_g[j]), cmlp_w_s[j].astype(bf16), bs,
                      cmlp_w_out[j].astype(bf16), nt)
        kk = i // 2
        g2 = _row(norm2_g[i])
        if i % 2 == 0:
            h = _ffn(h, mod, g2, *ffn_w, kk, nt)
        else:
            h = _moe(h, mod, g2, _pad_lanes(moe_router[kk]), *moe_w, nt, final_row, i == DEPTH - 1)
        h_pair = (h, h)
    return h.reshape(BATCH, SEQ, D)
```

```python
import functools
import math

import jax
import jax.numpy as jnp
import numpy as np
from jax import lax
from jax.experimental import pallas as pl
from jax.experimental.pallas import tpu as pltpu

f32 = jnp.float32
bf16 = jnp.bfloat16

D = 1024
BATCH = 2
SEQ = 8192
CTX = 256
DEPTH = 4
GRID_W = 64
EPS = 1e-6
N_MOD = 6

HEAD_DIM = 128
N_HEADS = 8
N_KV = 2
Q_GROUP = 4
ROPE_AXIS_DIM = 64
ROPE_THETA = 10000.0

SSD_INNER = 2048
SSD_P = 64
SSD_HEADS = 32
SSD_GROUPS = 4
SSD_HPG = 8
SSD_N = 128
SSD_CHUNK = 128
SSD_BC = SSD_GROUPS * SSD_N
SSD_CONV_CH = SSD_INNER + 2 * SSD_BC

CMLP_D = 2048
CMLP_GROUPS = 8
CMLP_GW = 256
CMLP_CHUNK = 128

D_FF = 3584
N_EXPERTS = 8

N_LAT = BATCH * SEQ
N_CTX = BATCH * CTX
T = N_LAT + N_CTX
TM = 512
NT = T // TM
NT_LAT = N_LAT // TM
TILES_PER_SAMPLE = SEQ // TM
CTX_MOD_ROW = BATCH

TF = 1792
NF = D_FF // TF

TQ = 512
TK = 512
ATT_CHUNKS = SEQ // TK
VW = 2 * HEAD_DIM
LOG2E = math.log2(math.e)
NQ_LAT = SEQ // TQ

VMEM_LIMIT = 56 * 1024 * 1024


def _mod_row(t):
    return jnp.minimum(t // TILES_PER_SAMPLE, CTX_MOD_ROW)


def _tile_spec(width):
    return pl.BlockSpec((TM, width), lambda t: (t, 0))


def _pair_specs(pair):
    lat, ctx = pair
    ctx_block = 0 if ctx.shape[0] == TM else NT_LAT
    return [pl.BlockSpec((TM, lat.shape[1]), lambda t: (jnp.minimum(t, NT_LAT - 1), 0)),
            pl.BlockSpec((TM, ctx.shape[1]), lambda t: (ctx_block, 0))]


def _pick_tile(lat_ref, ctx_ref):
    return jnp.where(pl.program_id(0) < NT_LAT, lat_ref[...], ctx_ref[...])


def _mod_spec():
    return pl.BlockSpec((None, 8, D), lambda t: (_mod_row(t), 0, 0))


def _const_spec(shape):
    n = len(shape)
    return pl.BlockSpec(shape, lambda *_: (0,) * n)


def _params(semantics):
    return pltpu.CompilerParams(dimension_semantics=semantics, vmem_limit_bytes=VMEM_LIMIT)


def _silu(x):
    return x * jax.nn.sigmoid(x)


def _norm_mod(x, g, shift, scale):
    y = x * lax.rsqrt(jnp.mean(x * x, axis=-1, keepdims=True) + EPS) * g
    return y * (1.0 + scale) + shift


def _mod_kernel(c_ref, w_ref, b_ref, o_ref):
    s = _silu(c_ref[...])
    o_ref[...] = jnp.dot(s.astype(bf16), w_ref[...].astype(bf16), preferred_element_type=f32) + b_ref[...]


def _modulation(cond, w_mod, b_mod):
    nblk = N_MOD * D // D
    out = pl.pallas_call(
        _mod_kernel,
        grid=(DEPTH, nblk),
        in_specs=[
            pl.BlockSpec((8, D), lambda i, j: (0, 0)),
            pl.BlockSpec((None, D, D), lambda i, j: (i, 0, j)),
            pl.BlockSpec((None, 1, D), lambda i, j: (i, 0, j)),
        ],
        out_specs=pl.BlockSpec((None, 8, D), lambda i, j: (i, 0, j)),
        out_shape=jax.ShapeDtypeStruct((DEPTH, 8, N_MOD * D), f32),
        compiler_params=_params(("arbitrary", "arbitrary")),
        name="modulation",
    )(cond, w_mod, b_mod.reshape(DEPTH, 1, N_MOD * D))
    mod = out[:, :3].reshape(DEPTH, 3, N_MOD, D)
    return jnp.pad(mod, ((0, 0), (0, 0), (0, 8 - N_MOD), (0, 0)))


def _qkv_kernel(hl_ref, hc_ref, mod_ref, g_ref, w_ref, wp_ref, qg_ref, kg_ref, cos_ref, sin_ref,
                q_ref, k_ref, v_ref):
    h = _pick_tile(hl_ref, hc_ref)
    a = _norm_mod(h, g_ref[...], mod_ref[0:1, :], mod_ref[1:2, :]).astype(bf16)
    qkv = jnp.dot(a, w_ref[...], preferred_element_type=f32)
    qkp = jnp.dot(a, wp_ref[...], preferred_element_type=f32)
    cos = cos_ref[...]
    sin = sin_ref[...]

    def head(x, xp, g, scale):
        r = lax.rsqrt(jnp.mean(x * x, axis=-1, keepdims=True) + EPS) * scale
        return (x * g[0:1, :] * cos + xp * g[1:2, :] * sin) * r

    qg = qg_ref[...]
    kg = kg_ref[...]
    for hh in range(N_HEADS):
        sl = slice(hh * HEAD_DIM, (hh + 1) * HEAD_DIM)
        q_ref[:, sl] = head(qkv[:, sl], qkp[:, sl], qg, HEAD_DIM ** -0.5 * LOG2E).astype(bf16)
    for hh in range(N_KV):
        sl = slice(hh * HEAD_DIM, (hh + 1) * HEAD_DIM)
        src = slice((N_HEADS + hh) * HEAD_DIM, (N_HEADS + hh + 1) * HEAD_DIM)
        k_ref[:, sl] = head(qkv[:, src], qkp[:, src], kg, 1.0).astype(bf16)
        v_ref[:, hh * VW:hh * VW + HEAD_DIM] = qkv[:, (N_HEADS + N_KV + hh) * HEAD_DIM:(N_HEADS + N_KV + hh + 1) * HEAD_DIM].astype(bf16)
        v_ref[:, hh * VW + HEAD_DIM:(hh + 1) * VW] = jnp.ones((TM, HEAD_DIM), bf16)


def _rope_partner(w):
    lead = w.shape[:-1]
    w5 = w.reshape(lead + (-1, 2, 2, ROPE_AXIS_DIM // 2))
    return w5[..., ::-1, :].reshape(w.shape)


def _qkv_proj(h_pair, mod, g, w_qkv, w_partner, q_g, k_g, cos_t, sin_t):
    rope_spec = pl.BlockSpec((TM, HEAD_DIM), lambda t: (jnp.where(t < NT_LAT, t % TILES_PER_SAMPLE, TILES_PER_SAMPLE), 0))
    return pl.pallas_call(
        _qkv_kernel,
        grid=(NT,),
        in_specs=[
            *_pair_specs(h_pair), _mod_spec(), _const_spec((1, D)),
            _const_spec((D, (N_HEADS + 2 * N_KV) * HEAD_DIM)), _const_spec((D, (N_HEADS + N_KV) * HEAD_DIM)),
            _const_spec((2, HEAD_DIM)), _const_spec((2, HEAD_DIM)), rope_spec, rope_spec,
        ],
        out_specs=[_tile_spec(N_HEADS * HEAD_DIM), _tile_spec(N_KV * HEAD_DIM), _tile_spec(N_KV * VW)],
        out_shape=[
            jax.ShapeDtypeStruct((T, N_HEADS * HEAD_DIM), bf16),
            jax.ShapeDtypeStruct((T, N_KV * HEAD_DIM), bf16),
            jax.ShapeDtypeStruct((T, N_KV * VW), bf16),
        ],
        compiler_params=_params(("parallel",)),
        name="attn_qkv",
    )(*h_pair, mod, g, w_qkv, w_partner, q_g, k_g, cos_t, sin_t)


def _stack_heads(q):
    return jnp.concatenate([q[:, i * HEAD_DIM:(i + 1) * HEAD_DIM] for i in range(Q_GROUP)], axis=0)


def _scores(qs, k):
    return lax.dot_general(qs, k, (((1,), (1,)), ((), ())), preferred_element_type=f32)


def _attn_kernel(q_ref, kl_ref, vl_ref, kc_ref, vc_ref, *rest, n_cast):
    qs = _stack_heads(q_ref[...])

    w_in, o_ref, w_out = rest[:n_cast], rest[n_cast], rest[n_cast + 1:2 * n_cast + 1]
    m_sc, acc_sc, s_a, s_b, s_c = rest[2 * n_cast + 1:]

    for src, dst in zip(w_in, w_out):
        dst[...] = src[...].astype(bf16)

    def absorb(s, v):
        m_prev = m_sc[...]
        m_next = jnp.maximum(m_prev, jnp.max(s, axis=1, keepdims=True))
        alpha = jnp.exp2(m_prev - m_next)
        p = jnp.exp2(s - jnp.concatenate([m_next] * (s.shape[1] // HEAD_DIM), axis=1))
        acc_sc[...] = (jnp.concatenate([alpha, alpha], axis=1) * acc_sc[...]
                       + jnp.dot(p.astype(bf16), v, preferred_element_type=f32))
        m_sc[...] = m_next

    def chunk(ref, c):
        return ref[pl.ds(pl.multiple_of(c * TK, TK), TK), :]

    m_sc[...] = jnp.full_like(m_sc, -jnp.inf)
    acc_sc[...] = jnp.zeros_like(acc_sc)
    s_c[...] = _scores(qs, kc_ref[...])
    s_a[...] = _scores(qs, chunk(kl_ref, 0))

    def body(c2, carry):
        c = 2 * c2
        s_b[...] = _scores(qs, chunk(kl_ref, c + 1))
        absorb(s_a[...], chunk(vl_ref, c))
        s_a[...] = _scores(qs, chunk(kl_ref, c + 2))
        absorb(s_b[...], chunk(vl_ref, c + 1))
        return carry
    lax.fori_loop(0, ATT_CHUNKS // 2 - 1, body, 0)

    s_b[...] = _scores(qs, chunk(kl_ref, ATT_CHUNKS - 1))
    absorb(s_a[...], chunk(vl_ref, ATT_CHUNKS - 2))
    absorb(s_b[...], chunk(vl_ref, ATT_CHUNKS - 1))
    absorb(s_c[...], vc_ref[...])

    acc = acc_sc[...]
    o = acc[:, :HEAD_DIM] / acc[:, HEAD_DIM:]
    for i in range(Q_GROUP):
        o_ref[:, i * HEAD_DIM:(i + 1) * HEAD_DIM] = o[i * TQ:(i + 1) * TQ].astype(bf16)


def _attn_ctx_kernel(q_ref, kc_ref, vc_ref, o_ref):
    s = _scores(_stack_heads(q_ref[...]), kc_ref[...])
    p = jnp.exp2(s - jnp.max(s, axis=1, keepdims=True))
    acc = jnp.dot(p.astype(bf16), vc_ref[...], preferred_element_type=f32)
    o = acc[:, :HEAD_DIM] / acc[:, HEAD_DIM:]
    for i in range(Q_GROUP):
        o_ref[:, i * HEAD_DIM:(i + 1) * HEAD_DIM] = o[i * CTX:(i + 1) * CTX].astype(bf16)


def _attention(q, k, v, casts):
    ctx_blk = N_LAT // CTX
    steps = BATCH * N_KV * NQ_LAT
    slabs, firsts, out_shapes = [], [], []
    for w, layer in casts:
        n_parts = steps if layer is None else w.shape[0] * steps
        slabs.append(w.reshape(n_parts, -1, w.shape[-1]))
        firsts.append(0 if layer is None else layer * steps)
        out_shapes.append(w.shape if layer is None else w.shape[1:])

    def slab_spec(w, first):
        return pl.BlockSpec((None,) + w.shape[1:],
                            lambda b, kh, qi: (first + (b * N_KV + kh) * NQ_LAT + qi, 0, 0))

    def q_map(b, kh, qi):
        return (b * NQ_LAT + qi, kh)

    def lat_spec(width):
        return pl.BlockSpec((SEQ, width), lambda b, kh, qi: (b, kh), pipeline_mode=pl.Buffered(1))

    def ctx_spec(width):
        return pl.BlockSpec((CTX, width), lambda b, kh, qi: (ctx_blk + b, kh))

    rows = Q_GROUP * TQ
    out = pl.pallas_call(
        functools.partial(_attn_kernel, n_cast=len(slabs)),
        grid=(BATCH, N_KV, NQ_LAT),
        in_specs=[pl.BlockSpec((TQ, Q_GROUP * HEAD_DIM), q_map), lat_spec(HEAD_DIM), lat_spec(VW),
                  ctx_spec(HEAD_DIM), ctx_spec(VW)] + [slab_spec(w, f) for w, f in zip(slabs, firsts)],
        out_specs=[pl.BlockSpec((TQ, Q_GROUP * HEAD_DIM), q_map)] + [slab_spec(w, 0) for w in slabs],
        out_shape=[jax.ShapeDtypeStruct((N_LAT, N_HEADS * HEAD_DIM), bf16)]
        + [jax.ShapeDtypeStruct((steps,) + w.shape[1:], bf16) for w in slabs],
        scratch_shapes=[pltpu.VMEM((rows, HEAD_DIM), f32), pltpu.VMEM((rows, VW), f32),
                        pltpu.VMEM((rows, TK), f32), pltpu.VMEM((rows, TK), f32), pltpu.VMEM((rows, CTX), f32)],
        compiler_params=_params(("parallel", "parallel", "arbitrary")),
        name="attn_core",
    )(q, k, v, k, v, *slabs)
    return out[0], [o.reshape(shape) for o, shape in zip(out[1:], out_shapes)]


def _attention_ctx(q, k, v):
    ctx_blk = N_LAT // CTX
    return pl.pallas_call(
        _attn_ctx_kernel,
        grid=(BATCH, N_KV),
        in_specs=[pl.BlockSpec((CTX, Q_GROUP * HEAD_DIM), lambda b, kh: (ctx_blk + b, kh)),
                  pl.BlockSpec((CTX, HEAD_DIM), lambda b, kh: (ctx_blk + b, kh)),
                  pl.BlockSpec((CTX, VW), lambda b, kh: (ctx_blk + b, kh))],
        out_specs=pl.BlockSpec((CTX, Q_GROUP * HEAD_DIM), lambda b, kh: (b, kh)),
        out_shape=jax.ShapeDtypeStruct((N_CTX, N_HEADS * HEAD_DIM), bf16),
        compiler_params=_params(("parallel", "parallel")),
        name="attn_ctx",
    )(q, k, v)


def _proj_res_kernel(y_ref, w_ref, h_ref, mod_ref, o_ref):
    y = jnp.dot(y_ref[...], w_ref[...], preferred_element_type=f32)
    o_ref[...] = h_ref[...] + mod_ref[2:3, :] * y


def _proj_res2_kernel(yl_ref, yc_ref, w_ref, hl_ref, hc_ref, mod_ref, o_ref):
    y = jnp.dot(_pick_tile(yl_ref, yc_ref), w_ref[...], preferred_element_type=f32)
    o_ref[...] = _pick_tile(hl_ref, hc_ref) + mod_ref[2:3, :] * y


def _proj_res(y, w, h, mod, nt):
    kdim = y.shape[1]
    return pl.pallas_call(
        _proj_res_kernel,
        grid=(nt,),
        in_specs=[_tile_spec(kdim), _const_spec((kdim, D)), _tile_spec(D), _mod_spec()],
        out_specs=_tile_spec(D),
        out_shape=jax.ShapeDtypeStruct((nt * TM, D), f32),
        compiler_params=_params(("parallel",)),
        name="proj_res",
    )(y, w, h, mod)


def _proj_res2(y_pair, w, h_pair, mod):
    kdim = y_pair[0].shape[1]
    return pl.pallas_call(
        _proj_res2_kernel,
        grid=(NT,),
        in_specs=[*_pair_specs(y_pair), _const_spec((kdim, D)), *_pair_specs(h_pair), _mod_spec()],
        out_specs=_tile_spec(D),
        out_shape=jax.ShapeDtypeStruct((T, D), f32),
        compiler_params=_params(("parallel",)),
        name="proj_res2",
    )(*y_pair, w, *h_pair, mod)


def _ffn_kernel(h_ref, mod_ref, g_ref, wg_ref, wu_ref, wd_ref, o_ref, xn_sc, acc_sc):
    f = pl.program_id(1)

    @pl.when(f == 0)
    def _():
        xn_sc[...] = _norm_mod(h_ref[...], g_ref[...], mod_ref[3:4, :], mod_ref[4:5, :]).astype(bf16)
        acc_sc[...] = jnp.zeros_like(acc_sc)

    x = xn_sc[...]
    gate = jnp.dot(x, wg_ref[...], preferred_element_type=f32)
    up = jnp.dot(x, wu_ref[...], preferred_element_type=f32)
    hid = (_silu(gate) * up).astype(bf16)
    acc_sc[...] += jnp.dot(hid, wd_ref[...], preferred_element_type=f32)

    @pl.when(f == NF - 1)
    def _():
        o_ref[...] = h_ref[...] + mod_ref[5:6, :] * acc_sc[...]


def _ffn(h, mod, g, wg, wu, wd, layer, nt):
    return pl.pallas_call(
        _ffn_kernel,
        grid=(nt, NF),
        in_specs=[
            pl.BlockSpec((TM, D), lambda t, f: (t, 0)),
            pl.BlockSpec((None, 8, D), lambda t, f: (_mod_row(t), 0, 0)),
            pl.BlockSpec((1, D), lambda t, f: (0, 0)),
            pl.BlockSpec((None, D, TF), lambda t, f: (layer, 0, f)),
            pl.BlockSpec((None, D, TF), lambda t, f: (layer, 0, f)),
            pl.BlockSpec((None, TF, D), lambda t, f: (layer, f, 0)),
        ],
        out_specs=pl.BlockSpec((TM, D), lambda t, f: (t, 0)),
        out_shape=jax.ShapeDtypeStruct((nt * TM, D), f32),
        scratch_shapes=[pltpu.VMEM((TM, D), bf16), pltpu.VMEM((TM, D), f32)],
        compiler_params=_params(("parallel", "arbitrary")),
        name="ffn_dense",
    )(h, mod, g, wg, wu, wd)


def _split_bf16(x):
    hi = x.astype(bf16)
    lo = (x - hi.astype(f32)).astype(bf16)
    return hi, lo


def _split3(x):
    p0 = x.astype(bf16)
    r = x - p0.astype(f32)
    p1 = r.astype(bf16)
    p2 = (r - p1.astype(f32)).astype(bf16)
    return p0, p1, p2


def _dot3(a_parts, b):
    out = jnp.dot(a_parts[0], b, preferred_element_type=f32)
    for a in a_parts[1:]:
        out = out + jnp.dot(a, b, preferred_element_type=f32)
    return out


RT = 256
RT_PER_SAMPLE = SEQ // RT
ALIGN = 16
WIN = RT + ALIGN
FT = 512
HEAD_ROWS = 128
SLAB = 64
STAGE = FT + -(-WIN // SLAB) * SLAB
CAP = 17408
CAP_BLOCKS = CAP // FT
DN_T = (((0,), (0,)), ((), ()))
NO_ROW = -1e9


def _route_kernel(h_ref, mod_ref, g_ref, r_ref, xs_hbm, gs_hbm, rank_ref, seg_ref, cnt_ref,
                  x_stage, g_stage, cnt_sc, pend_sc, dst_sc, sems, *, n_rt):
    i = pl.program_id(0)

    @pl.when(i == 0)
    def _():
        x_stage[...] = jnp.zeros_like(x_stage)
        g_stage[...] = jnp.zeros_like(g_stage)
        for e in range(N_EXPERTS):
            cnt_sc[e] = 0
            pend_sc[e] = 0

    x = _norm_mod(h_ref[...], g_ref[...], mod_ref[3:4, :], mod_ref[4:5, :])
    xh, xl = _split_bf16(x)
    rh, rl = _split_bf16(r_ref[...])
    logits = (jnp.dot(xh, rh, preferred_element_type=f32) + jnp.dot(xl, rh, preferred_element_type=f32)
              + jnp.dot(xh, rl, preferred_element_type=f32))
    lane = lax.broadcasted_iota(jnp.int32, (RT, 128), 1)
    lg = jnp.where(lane < N_EXPERTS, logits, -jnp.inf)
    m1 = jnp.max(lg, axis=1, keepdims=True)
    i1 = jnp.min(jnp.where(lg == m1, lane, 128), axis=1, keepdims=True)
    lg2 = jnp.where(lane == i1, -jnp.inf, lg)
    m2 = jnp.max(lg2, axis=1, keepdims=True)
    i2 = jnp.min(jnp.where(lg2 == m2, lane, 128), axis=1, keepdims=True)
    e2 = jnp.exp(m2 - m1)
    den = 1.0 + e2
    gates = jnp.where(lane == i1, 1.0 / den, 0.0) + jnp.where(lane == i2, e2 / den, 0.0)
    used = jnp.where((lane == i1) | (lane == i2), 1.0, 0.0).astype(bf16)

    tp = lax.broadcasted_iota(jnp.int32, (RT, RT), 0)
    tt = lax.broadcasted_iota(jnp.int32, (RT, RT), 1)
    earlier = jnp.where(tp < tt, 1.0, 0.0).astype(bf16)
    eye = jnp.where(tp == tt, 1.0, 0.0).astype(bf16)
    rank_t = lax.dot_general(used, earlier, DN_T, preferred_element_type=f32)
    used_t = lax.dot_general(used, eye, DN_T, preferred_element_type=f32)
    rank_t = jnp.where(used_t > 0, rank_t, NO_ROW)
    rank_ref[...] = rank_t[0:N_EXPERTS]

    g_hi, g_lo = _split_bf16(gates)
    xg = jnp.concatenate([xh, g_hi, g_lo], axis=1)
    head_row = lax.broadcasted_iota(jnp.int32, (HEAD_ROWS, RT), 0).astype(f32)
    slab_row = lax.broadcasted_iota(jnp.int32, (SLAB, RT), 0).astype(f32)

    def flush_copies(e, done_rows):
        dst = pl.multiple_of(e * CAP + done_rows, FT)
        return (pltpu.make_async_copy(x_stage.at[e, pl.ds(0, FT), :], xs_hbm.at[pl.ds(dst, FT), :], sems.at[e, 0]),
                pltpu.make_async_copy(g_stage.at[e, pl.ds(0, FT), :], gs_hbm.at[pl.ds(dst, FT), :], sems.at[e, 1]))

    def settle(e):
        @pl.when(pend_sc[e] == 1)
        def _():
            for copy in flush_copies(e, dst_sc[e]):
                copy.wait()
            x_stage[e, 0:STAGE - FT, :] = x_stage[e, FT:STAGE, :]
            x_stage[e, STAGE - FT:STAGE, :] = jnp.zeros((FT, D), bf16)
            g_stage[e, 0:STAGE - FT, :] = g_stage[e, FT:STAGE, :]
            g_stage[e, STAGE - FT:STAGE, :] = jnp.zeros((FT, 128), f32)
            pend_sc[e] = 0

    for e in range(N_EXPERTS):
        settle(e)

    state = []
    n_of = []
    for e in range(N_EXPERTS):
        cnt = cnt_sc[e]
        seg_ref[i * N_EXPERTS + e] = cnt
        fill = cnt % FT
        start = pl.multiple_of((fill // ALIGN) * ALIGN, ALIGN)
        n_e = jnp.sum(used_t[e:e + 1, :]).astype(jnp.int32)
        cnt_sc[e] = cnt + n_e
        n_of.append(n_e)
        state.append((start, fill - start, fill + n_e >= FT, cnt - fill))

    def target_rows(e):
        return rank_t[e:e + 1, :] + state[e][1].astype(f32)

    for e in range(N_EXPERTS):
        start = state[e][0]
        p = jnp.where(head_row == target_rows(e), 1.0, 0.0).astype(bf16)
        rows = jnp.dot(p, xg, preferred_element_type=f32)
        grow = rows[:, D:D + 128] + rows[:, D + 128:]
        old = pl.ds(start, ALIGN)
        new = pl.ds(pl.multiple_of(start + ALIGN, ALIGN), HEAD_ROWS - ALIGN)
        x_stage[e, old, :] = (x_stage[e, old, :].astype(f32) + rows[:ALIGN, :D]).astype(bf16)
        x_stage[e, new, :] = rows[ALIGN:, :D].astype(bf16)
        g_stage[e, old, :] = g_stage[e, old, :] + grow[:ALIGN]
        g_stage[e, new, :] = grow[ALIGN:]

    crowded = False
    for e in range(N_EXPERTS):
        crowded = crowded | (state[e][1] + n_of[e] > HEAD_ROWS)

    @pl.when(crowded)
    def _():
        for e in range(N_EXPERTS):
            def place(sl, carry, e=e):
                p = jnp.where(slab_row + (sl * SLAB).astype(f32) == target_rows(e), 1.0, 0.0).astype(bf16)
                rows = jnp.dot(p, xg, preferred_element_type=f32)
                dst = pl.ds(pl.multiple_of(state[e][0] + sl * SLAB, ALIGN), SLAB)
                x_stage[e, dst, :] = rows[:, :D].astype(bf16)
                g_stage[e, dst, :] = rows[:, D:D + 128] + rows[:, D + 128:]
                return carry
            lax.fori_loop(HEAD_ROWS // SLAB, (state[e][1] + n_of[e] + SLAB - 1) // SLAB, place, 0)

    for e in range(N_EXPERTS):
        @pl.when(state[e][2])
        def _():
            for copy in flush_copies(e, state[e][3]):
                copy.start()
            pend_sc[e] = 1
            dst_sc[e] = state[e][3]

    @pl.when(i == n_rt - 1)
    def _():
        for e in range(N_EXPERTS):
            settle(e)
            c = cnt_sc[e]
            cnt_ref[e] = c
            for copy in flush_copies(e, c - c % FT):
                copy.start()
                copy.wait()


def _moe_route(h, mod, g, router, n_rt):
    return pl.pallas_call(
        functools.partial(_route_kernel, n_rt=n_rt),
        grid=(n_rt,),
        in_specs=[
            pl.BlockSpec((RT, D), lambda i: (i, 0)),
            pl.BlockSpec((None, 8, D), lambda i: (jnp.minimum(i // RT_PER_SAMPLE, CTX_MOD_ROW), 0, 0)),
            _const_spec((1, D)), _const_spec((D, 128)),
        ],
        out_specs=[
            pl.BlockSpec(memory_space=pl.ANY), pl.BlockSpec(memory_space=pl.ANY),
            pl.BlockSpec((None, N_EXPERTS, RT), lambda i: (i, 0, 0)),
            pl.BlockSpec(memory_space=pltpu.SMEM), pl.BlockSpec(memory_space=pltpu.SMEM),
        ],
        out_shape=[
            jax.ShapeDtypeStruct((N_EXPERTS * CAP, D), bf16),
            jax.ShapeDtypeStruct((N_EXPERTS * CAP, 128), f32),
            jax.ShapeDtypeStruct((n_rt, N_EXPERTS, RT), f32),
            jax.ShapeDtypeStruct((n_rt * N_EXPERTS,), jnp.int32),
            jax.ShapeDtypeStruct((N_EXPERTS,), jnp.int32),
        ],
        scratch_shapes=[
            pltpu.VMEM((N_EXPERTS, STAGE, D), bf16), pltpu.VMEM((N_EXPERTS, STAGE, 128), f32),
            pltpu.SMEM((N_EXPERTS,), jnp.int32), pltpu.SMEM((N_EXPERTS,), jnp.int32),
            pltpu.SMEM((N_EXPERTS,), jnp.int32), pltpu.SemaphoreType.DMA((N_EXPERTS, 2)),
        ],
        compiler_params=_params(("arbitrary",)),
        name="moe_route",
    )(h, mod, g, router)


def _moe_ffn_kernel(te_ref, tb_ref, nt_ref, x_ref, gs_ref, wg_ref, wu_ref, wd_ref, y_ref, acc_sc):
    j = pl.program_id(0)
    f = pl.program_id(1)

    @pl.when(j < nt_ref[0])
    def _():
        @pl.when(f == 0)
        def _():
            acc_sc[...] = jnp.zeros_like(acc_sc)

        x = x_ref[...]
        gate = jnp.dot(x, wg_ref[...], preferred_element_type=f32)
        up = jnp.dot(x, wu_ref[...], preferred_element_type=f32)
        hid = (_silu(gate) * up).astype(bf16)
        acc_sc[...] += jnp.dot(hid, wd_ref[...], preferred_element_type=f32)

        @pl.when(f == NF - 1)
        def _():
            lane = lax.broadcasted_iota(jnp.int32, (FT, 128), 1)
            ge = jnp.sum(jnp.where(lane == te_ref[j], gs_ref[...], 0.0), axis=1, keepdims=True)
            y_ref[...] = (ge * acc_sc[...]).astype(bf16)


def _moe_experts(xs, gs, wg, wu, wd, tile_expert, tile_block, n_tiles):
    max_tiles = tile_expert.shape[0]

    def f_eff(j, f, nt):
        return jnp.where(j < nt[0], f, NF - 1)

    grid_spec = pltpu.PrefetchScalarGridSpec(
        num_scalar_prefetch=3,
        grid=(max_tiles, NF),
        in_specs=[
            pl.BlockSpec((FT, D), lambda j, f, te, tb, nt: (tb[j], 0)),
            pl.BlockSpec((FT, 128), lambda j, f, te, tb, nt: (tb[j], 0)),
            pl.BlockSpec((None, D, TF), lambda j, f, te, tb, nt: (te[j], 0, f_eff(j, f, nt))),
            pl.BlockSpec((None, D, TF), lambda j, f, te, tb, nt: (te[j], 0, f_eff(j, f, nt))),
            pl.BlockSpec((None, TF, D), lambda j, f, te, tb, nt: (te[j], f_eff(j, f, nt), 0)),
        ],
        out_specs=pl.BlockSpec((FT, D), lambda j, f, te, tb, nt: (tb[j], 0)),
        scratch_shapes=[pltpu.VMEM((FT, D), f32)],
    )
    return pl.pallas_call(
        _moe_ffn_kernel,
        grid_spec=grid_spec,
        out_shape=jax.ShapeDtypeStruct((N_EXPERTS * CAP, D), bf16),
        compiler_params=_params(("arbitrary", "arbitrary")),
        name="moe_experts",
    )(tile_expert, tile_block, n_tiles, xs, gs, wg, wu, wd)


def _moe_combine_kernel(seg_ref, nseg_ref, wmax_ref, rank_ref, h_ref, mod_ref, fg_ref, ys_hbm, o_ref, ybuf, acc_sc,
                        sems, *, n_rt, apply_final):
    i = pl.program_id(0)
    main_row = lax.broadcasted_iota(jnp.int32, (RT, RT), 0).astype(f32)
    last_row = lax.broadcasted_iota(jnp.int32, (ALIGN, RT), 0).astype(f32) + RT

    def window(step, e):
        seg = seg_ref[step * N_EXPERTS + e]
        start = pl.multiple_of(jnp.minimum((seg // ALIGN) * ALIGN, wmax_ref[e]), ALIGN)
        return seg, start

    def window_copy(step, e):
        _, start = window(step, e)
        return pltpu.make_async_copy(ys_hbm.at[pl.ds(e * CAP + start, WIN), :], ybuf.at[step % 2, e],
                                     sems.at[step % 2, e])

    def fetch(step):
        for e in range(N_EXPERTS):
            @pl.when(nseg_ref[step * N_EXPERTS + e] > 0)
            def _():
                window_copy(step, e).start()

    @pl.when(i == 0)
    def _():
        ybuf[...] = jnp.zeros_like(ybuf)
        fetch(0)

    @pl.when(i + 1 < n_rt)
    def _():
        fetch(i + 1)

    acc = None
    spill = False
    for e in range(N_EXPERTS):
        seg, start = window(i, e)
        n = nseg_ref[i * N_EXPERTS + e]

        @pl.when(n > 0)
        def _():
            window_copy(i, e).wait()

        target = rank_ref[e:e + 1, :] + (seg - start).astype(f32)
        p = jnp.where(main_row == target, 1.0, 0.0).astype(bf16)
        part = lax.dot_general(p, ybuf[i % 2, e, 0:RT, :], DN_T, preferred_element_type=f32)
        acc = part if acc is None else acc + part
        spill = spill | (seg - start + n > RT)
    acc_sc[...] = acc

    @pl.when(spill)
    def _():
        for e in range(N_EXPERTS):
            seg, start = window(i, e)
            target = rank_ref[e:e + 1, :] + (seg - start).astype(f32)
            p = jnp.where(last_row == target, 1.0, 0.0).astype(bf16)
            acc_sc[...] += lax.dot_general(p, ybuf[i % 2, e, RT:WIN, :], DN_T, preferred_element_type=f32)

    out = h_ref[...] + mod_ref[5:6, :] * acc_sc[...]
    if apply_final:
        out = out * lax.rsqrt(jnp.mean(out * out, axis=-1, keepdims=True) + EPS) * fg_ref[...]
    o_ref[...] = out


def _moe_combine(ys, rank, seg, nseg, wmax, h, mod, final_g, n_rt, apply_final):
    grid_spec = pltpu.PrefetchScalarGridSpec(
        num_scalar_prefetch=3,
        grid=(n_rt,),
        in_specs=[
            pl.BlockSpec((None, N_EXPERTS, RT), lambda i, *_: (i, 0, 0)),
            pl.BlockSpec((RT, D), lambda i, *_: (i, 0)),
            pl.BlockSpec((None, 8, D), lambda i, *_: (jnp.minimum(i // RT_PER_SAMPLE, CTX_MOD_ROW), 0, 0)),
            pl.BlockSpec((1, D), lambda i, *_: (0, 0)),
            pl.BlockSpec(memory_space=pl.ANY),
        ],
        out_specs=pl.BlockSpec((RT, D), lambda i, *_: (i, 0)),
        scratch_shapes=[pltpu.VMEM((2, N_EXPERTS, WIN, D), bf16), pltpu.VMEM((RT, D), f32),
                        pltpu.SemaphoreType.DMA((2, N_EXPERTS))],
    )
    return pl.pallas_call(
        functools.partial(_moe_combine_kernel, n_rt=n_rt, apply_final=apply_final),
        grid_spec=grid_spec,
        out_shape=jax.ShapeDtypeStruct((n_rt * RT, D), f32),
        compiler_params=_params(("arbitrary",)),
        name="moe_combine",
    )(seg, nseg, wmax, rank, h, mod, final_g, ys)


def _moe(h, mod, g, router, wg, wu, wd, nt, final_g, apply_final):
    n_rt = nt * (TM // RT)
    xs, gs, rank, seg, counts = _moe_route(h, mod, g, router, n_rt)
    tiles = (counts + FT - 1) // FT
    ends = jnp.cumsum(tiles)
    n_tiles = ends[-1]
    max_tiles = (2 * n_rt * RT) // FT + N_EXPERTS
    j = jnp.minimum(jnp.arange(max_tiles, dtype=jnp.int32), n_tiles - 1)
    tile_expert = jnp.sum((j[:, None] >= ends[None, :]).astype(jnp.int32), axis=1)
    tile_block = tile_expert * CAP_BLOCKS + j - (ends - tiles)[tile_expert]
    ys = _moe_experts(xs, gs, wg, wu, wd, tile_expert, tile_block, n_tiles.reshape(1))
    seg2 = seg.reshape(n_rt, N_EXPERTS)
    nseg = (jnp.concatenate([seg2[1:], counts[None, :]], axis=0) - seg2).reshape(-1)
    wmax = jnp.maximum(tiles * FT - WIN, 0)
    return _moe_combine(ys, rank, seg, nseg, wmax, h, mod, final_g, n_rt, apply_final)


HALO = 8
CONV_COLS = 256


def _ssd_in_kernel(h_ref, hp_ref, hn_ref, mod_ref, g_ref, wz_ref, wx_ref, wdt_ref, cw_ref, cb_ref,
                   z_ref, x_ref, dt_ref):
    t = pl.program_id(0)
    is_ctx = t >= NT_LAT
    seg_start = is_ctx | (t % TILES_PER_SAMPLE == 0)
    seg_end = is_ctx | (t % TILES_PER_SAMPLE == TILES_PER_SAMPLE - 1)
    shift, scale, g = mod_ref[0:1, :], mod_ref[1:2, :], g_ref[...]
    a = _norm_mod(h_ref[...], g, shift, scale).astype(bf16)
    halo = jnp.concatenate([hp_ref[...], hn_ref[...]], axis=0)
    a_halo = _norm_mod(halo, g, shift, scale).astype(bf16)

    for j in range(SSD_INNER // D):
        z_ref[:, j * D:(j + 1) * D] = jnp.dot(a, wz_ref[:, j * D:(j + 1) * D], preferred_element_type=f32).astype(bf16)
    dt_ref[...] = jnp.dot(a, wdt_ref[...], preferred_element_type=f32)

    cw = CONV_COLS
    row = lax.broadcasted_iota(jnp.int32, (TM, cw), 0)
    first = row == 0
    last = row == TM - 1
    no_prev = is_ctx & (row == CTX)
    no_next = is_ctx & (row == CTX - 1)
    for j in range(SSD_CONV_CH // cw):
        cols = slice(j * cw, (j + 1) * cw)
        x = jnp.dot(a, wx_ref[:, cols], preferred_element_type=f32)
        xh = jnp.dot(a_halo, wx_ref[:, cols], preferred_element_type=f32)
        prev_row = jnp.where(seg_start, 0.0, xh[HALO - 1:HALO, :])
        next_row = jnp.where(seg_end, 0.0, xh[HALO:HALO + 1, :])
        xm1 = jnp.where(first, prev_row, pltpu.roll(x, 1, 0))
        xp1 = jnp.where(last, next_row, pltpu.roll(x, TM - 1, 0))
        xm1 = jnp.where(no_prev, 0.0, xm1)
        xp1 = jnp.where(no_next, 0.0, xp1)
        y = cw_ref[0:1, cols] * xm1 + cw_ref[1:2, cols] * x + cw_ref[2:3, cols] * xp1 + cb_ref[:, cols]
        x_ref[:, cols] = _silu(y).astype(bf16)


def _ssd_in_proj(h, mod, g, wz, wx, wdt, conv_w, conv_b):
    per = TM // HALO
    last_blk = T // HALO - 1
    return pl.pallas_call(
        _ssd_in_kernel,
        grid=(NT,),
        in_specs=[_tile_spec(D),
                  pl.BlockSpec((HALO, D), lambda t: (jnp.maximum(t * per - 1, 0), 0)),
                  pl.BlockSpec((HALO, D), lambda t: (jnp.minimum((t + 1) * per, last_blk), 0)),
                  _mod_spec(), _const_spec((1, D)), _const_spec((D, SSD_INNER)),
                  _const_spec((D, SSD_CONV_CH)), _const_spec((D, 256)),
                  _const_spec((8, SSD_CONV_CH)), _const_spec((1, SSD_CONV_CH))],
        out_specs=[_tile_spec(SSD_INNER), _tile_spec(SSD_CONV_CH), _tile_spec(256)],
        out_shape=[jax.ShapeDtypeStruct((T, SSD_INNER), bf16), jax.ShapeDtypeStruct((T, SSD_CONV_CH), bf16),
                   jax.ShapeDtypeStruct((T, 256), f32)],
        compiler_params=_params(("parallel",)),
        name="ssd_in_proj",
    )(h, h, h, mod, g, wz, wx, wdt, conv_w, conv_b)


def _ssd_scan_kernel(xf_ref, bf_ref, cf_ref, dtf_ref, xb_ref, bb_ref, cb_ref, dtb_ref, bias_ref, alog_ref,
                     yf_ref, yb_ref, state_sc):
    @pl.when(pl.program_id(1) == 0)
    def _():
        state_sc[...] = jnp.zeros_like(state_sc)

    _ssd_chunk(0, xf_ref, bf_ref, cf_ref, dtf_ref, bias_ref, alog_ref, yf_ref, state_sc)
    _ssd_chunk(1, xb_ref, bb_ref, cb_ref, dtb_ref, bias_ref, alog_ref, yb_ref, state_sc)


def _ssd_chunk(d, x_ref, b_ref, c_ref, dt_ref, bias_ref, alog_ref, y_ref, state_sc):
    L = SSD_CHUNK
    li = lax.broadcasted_iota(jnp.int32, (L, L), 0)
    si = lax.broadcasted_iota(jnp.int32, (L, L), 1)
    causal = (si <= li) if d == 0 else (si >= li)
    tri = jnp.where(causal, 1.0, 0.0).astype(bf16)
    hi = lax.broadcasted_iota(jnp.int32, (128, SSD_INNER), 0)
    ci = lax.broadcasted_iota(jnp.int32, (128, SSD_INNER), 1)
    expand = jnp.where(ci // SSD_P == hi, 1.0, 0.0).astype(bf16)

    dt = jax.nn.softplus(dt_ref[...] + bias_ref[d])
    a_neg = -jnp.exp(alog_ref[d])
    da = dt * a_neg
    da_parts = _split3(da)
    cs = (jnp.dot(tri, da_parts[0], preferred_element_type=f32) + jnp.dot(tri, da_parts[1], preferred_element_type=f32)
          + jnp.dot(tri, da_parts[2], preferred_element_type=f32))
    cs_t = cs.T
    total = jnp.sum(da, axis=0, keepdims=True)

    e_out = jnp.exp(cs)
    e_in = jnp.exp(total - cs) * dt
    dt_x = jnp.dot(dt.astype(bf16), expand, preferred_element_type=f32).astype(bf16)
    e_in_x = jnp.dot(e_in.astype(bf16), expand, preferred_element_type=f32).astype(bf16)
    out_scale = jnp.dot(e_out.astype(bf16), expand, preferred_element_type=f32)
    chunk_decay = _dot3(_split_bf16(jnp.broadcast_to(jnp.exp(total), (8, 128))), expand)[0:1]

    x = x_ref[...]
    first_head = lax.broadcasted_iota(jnp.int32, (L, 2 * SSD_P), 1) < SSD_P
    xdt = x * dt_x
    xw = x * e_in_x

    for g in range(SSD_GROUPS):
        bg = b_ref[:, g * SSD_N:(g + 1) * SSD_N]
        cg = c_ref[:, g * SSD_N:(g + 1) * SSD_N]
        cb = lax.dot_general(cg, bg, (((1,), (1,)), ((), ())), preferred_element_type=f32)
        gsl = slice(g * SSD_HPG * SSD_P, (g + 1) * SSD_HPG * SSD_P)
        st = state_sc[d, g]
        y_off = jnp.dot(cg, st.astype(bf16), preferred_element_type=f32) * out_scale[:, gsl]
        def decay_matrix(hd):
            seg = cs[:, hd:hd + 1] - cs_t[hd:hd + 1, :]
            return (cb * jnp.exp(jnp.where(causal, seg, -jnp.inf))).astype(bf16)

        ys = []
        for r in range(0, SSD_HPG, 2):
            hd = g * SSD_HPG + r
            m2 = jnp.concatenate([decay_matrix(hd), decay_matrix(hd + 1)], axis=1)
            x2 = xdt[:, hd * SSD_P:(hd + 2) * SSD_P]
            rhs = jnp.concatenate([jnp.where(first_head, x2, 0), jnp.where(first_head, 0, x2)], axis=0)
            ys.append(jnp.dot(m2, rhs, preferred_element_type=f32))
        y_ref[:, gsl] = (jnp.concatenate(ys, axis=1) + y_off).astype(bf16)
        new = lax.dot_general(bg, xw[:, gsl], (((0,), (0,)), ((), ())), preferred_element_type=f32)
        state_sc[d, g] = st * chunk_decay[:, gsl] + new


SSD_NCHUNK = (SEQ + CTX) // SSD_CHUNK
SSD_CTX_CHUNKS = CTX // SSD_CHUNK
SSD_LAT_CHUNKS = SEQ // SSD_CHUNK


def _ssd_chunk_block(d, b, c):
    ctx_j = c if d == 0 else SSD_CTX_CHUNKS - 1 - c
    lat_j = c - SSD_CTX_CHUNKS if d == 0 else SSD_LAT_CHUNKS - 1 - (c - SSD_CTX_CHUNKS)
    return jnp.where(c < SSD_CTX_CHUNKS, N_LAT // SSD_CHUNK + b * SSD_CTX_CHUNKS + ctx_j, b * SSD_LAT_CHUNKS + lat_j)


def _ssd_scan(xbc, dt, dt_bias, a_log):
    x_blk = SSD_INNER // SSD_BC

    def chunk_specs(d):
        return [
            pl.BlockSpec((SSD_CHUNK, SSD_INNER), lambda b, c: (_ssd_chunk_block(d, b, c), 0)),
            pl.BlockSpec((SSD_CHUNK, SSD_BC), lambda b, c: (_ssd_chunk_block(d, b, c), x_blk)),
            pl.BlockSpec((SSD_CHUNK, SSD_BC), lambda b, c: (_ssd_chunk_block(d, b, c), x_blk + 1)),
            pl.BlockSpec((SSD_CHUNK, 128), lambda b, c: (_ssd_chunk_block(d, b, c), d)),
        ]

    def y_spec(d):
        return pl.BlockSpec((SSD_CHUNK, SSD_INNER), lambda b, c: (_ssd_chunk_block(d, b, c), 0))

    return pl.pallas_call(
        _ssd_scan_kernel,
        grid=(BATCH, SSD_NCHUNK),
        in_specs=chunk_specs(0) + chunk_specs(1) + [_const_spec((2, 1, 128)), _const_spec((2, 1, 128))],
        out_specs=[y_spec(0), y_spec(1)],
        out_shape=[jax.ShapeDtypeStruct((T, SSD_INNER), bf16)] * 2,
        scratch_shapes=[pltpu.VMEM((2, SSD_GROUPS, SSD_N, SSD_HPG * SSD_P), f32)],
        compiler_params=_params(("parallel", "arbitrary")),
        name="ssd_scan",
    )(xbc, xbc, xbc, dt, xbc, xbc, xbc, dt, dt_bias, a_log)


def _ssd_out_kernel(yf_ref, yb_ref, x_ref, z_ref, dskip_ref, ng_ref, w_ref, h_ref, mod_ref, o_ref):
    y = yf_ref[...].astype(f32) + yb_ref[...].astype(f32) + x_ref[...].astype(f32) * dskip_ref[...]
    gated = y * _silu(z_ref[...].astype(f32))
    gw = SSD_INNER // SSD_GROUPS
    parts = []
    for g in range(SSD_GROUPS):
        s = gated[:, g * gw:(g + 1) * gw]
        parts.append(s * lax.rsqrt(jnp.mean(s * s, axis=-1, keepdims=True) + EPS))
    normed = (jnp.concatenate(parts, axis=1) * ng_ref[...]).astype(bf16)
    out = jnp.dot(normed, w_ref[...], preferred_element_type=f32)
    o_ref[...] = h_ref[...] + mod_ref[2:3, :] * out


def _ssd_out(yf, yb, xbc, z, dskip, ng, w_out, h, mod, nt):
    return pl.pallas_call(
        _ssd_out_kernel,
        grid=(nt,),
        in_specs=[
            _tile_spec(SSD_INNER), _tile_spec(SSD_INNER), _tile_spec(SSD_INNER),
            _tile_spec(SSD_INNER), _const_spec((1, SSD_INNER)), _const_spec((1, SSD_INNER)),
            _const_spec((SSD_INNER, D)), _tile_spec(D), _mod_spec(),
        ],
        out_specs=_tile_spec(D),
        out_shape=jax.ShapeDtypeStruct((nt * TM, D), f32),
        compiler_params=_params(("parallel",)),
        name="ssd_out",
    )(yf, yb, xbc, z, dskip, ng, w_out, h, mod)


def _gelu(x):
    return 0.5 * x * (1.0 + lax.erf(x * math.sqrt(0.5)))


def _cmlp_kernel(h_ref, mod_ref, g_ref, wu_ref, wv_ref, bu_ref, bv_ref, vg_ref, ws_ref, bs_ref, wo_ref, o_ref, uv_sc):
    a = _norm_mod(h_ref[...], g_ref[...], mod_ref[0:1, :], mod_ref[1:2, :]).astype(bf16)
    v = _gelu(jnp.dot(a, wv_ref[...], preferred_element_type=f32) + bv_ref[...])
    v = (v * lax.rsqrt(jnp.mean(v * v, axis=-1, keepdims=True) + EPS) * vg_ref[...]).astype(bf16)
    u = _gelu(jnp.dot(a, wu_ref[...], preferred_element_type=f32) + bu_ref[...])
    for ck in range(TM // CMLP_CHUNK):
        rows = slice(ck * CMLP_CHUNK, (ck + 1) * CMLP_CHUNK)
        for g in range(CMLP_GROUPS):
            cols = slice(g * CMLP_GW, (g + 1) * CMLP_GW)
            mixed = jnp.dot(ws_ref[g], v[rows, cols], preferred_element_type=f32) + bs_ref[:, cols]
            uv_sc[rows, cols] = (u[rows, cols] * mixed).astype(bf16)
    out = jnp.dot(uv_sc[...], wo_ref[...], preferred_element_type=f32)
    o_ref[...] = h_ref[...] + mod_ref[2:3, :] * out


def _cmlp(h, mod, g, wu, wv, bu, bv, vg, ws, bs, wo, nt):
    return pl.pallas_call(
        _cmlp_kernel,
        grid=(nt,),
        in_specs=[
            _tile_spec(D), _mod_spec(), _const_spec((1, D)), _const_spec((D, CMLP_D)), _const_spec((D, CMLP_D)),
            _const_spec((1, CMLP_D)), _const_spec((1, CMLP_D)), _const_spec((1, CMLP_D)),
            _const_spec((CMLP_GROUPS, CMLP_CHUNK, CMLP_CHUNK)), _const_spec((CMLP_CHUNK, CMLP_D)),
            _const_spec((CMLP_D, D)),
        ],
        out_specs=_tile_spec(D),
        out_shape=jax.ShapeDtypeStruct((nt * TM, D), f32),
        scratch_shapes=[pltpu.VMEM((TM, CMLP_D), bf16)],
        compiler_params=_params(("parallel",)),
        name="cmlp",
    )(h, mod, g, wu, wv, bu, bv, vg, ws, bs, wo)


def _rope_tables():
    pos = np.arange(SEQ)
    inv_freq = (1.0 / (np.float32(ROPE_THETA) ** (np.arange(0, ROPE_AXIS_DIM, 2, dtype=np.float32) / ROPE_AXIS_DIM)))
    inv_freq = inv_freq.astype(np.float32)
    ang_r = (pos // GRID_W).astype(np.float32)[:, None] * inv_freq
    ang_c = (pos % GRID_W).astype(np.float32)[:, None] * inv_freq
    cos = np.concatenate([np.cos(ang_r)] * 2 + [np.cos(ang_c)] * 2, axis=1)
    sin = np.concatenate([-np.sin(ang_r), np.sin(ang_r), -np.sin(ang_c), np.sin(ang_c)], axis=1)
    cos = np.concatenate([cos, np.ones((TM, HEAD_DIM), np.float32)], axis=0).astype(np.float32)
    sin = np.concatenate([sin, np.zeros((TM, HEAD_DIM), np.float32)], axis=0).astype(np.float32)
    return jnp.asarray(cos), jnp.asarray(sin)


def _row(v):
    return v.reshape(1, -1)


def _pad_lanes(v, width=128):
    return jnp.pad(v, ((0, 0), (0, width - v.shape[1])))


def kernel(x, c, ctx, c_ctx, w_mod, b_mod, norm1_g, norm2_g, attn_w_qkv, attn_q_g, attn_k_g, attn_w_o, ssd_w_in, ssd_conv_w, ssd_conv_b, ssd_dt_bias_f, ssd_dt_bias_b, ssd_a_log_f, ssd_a_log_b, ssd_d_skip, ssd_norm_g, ssd_w_out, cmlp_w_in, cmlp_b_in, cmlp_v_g, cmlp_w_s, cmlp_b_s, cmlp_w_out, ffn_w_gate, ffn_w_up, ffn_w_down, moe_router, moe_w_gate, moe_w_up, moe_w_down, final_g):
    h = None
    h_pair = (x.reshape(N_LAT, D), ctx.reshape(N_CTX, D))
    cond = jnp.concatenate([c, c_ctx[None, :], jnp.zeros((8 - BATCH - 1, D), f32)], axis=0)
    mods = _modulation(cond, w_mod, b_mod)
    cos_t, sin_t = _rope_tables()
    ffn_w = None
    moe_w = None
    final_row = _row(final_g)

    for i in range(DEPTH):
        need_ctx = i < DEPTH - 1
        nt = NT if need_ctx else NT_LAT
        mod = mods[i]
        kind, j = i % 3, i // 3
        g1 = _row(norm1_g[i])
        if kind == 0:
            w_qkv = attn_w_qkv[j]
            w_partner = _rope_partner(w_qkv[:, :(N_HEADS + N_KV) * HEAD_DIM]).astype(bf16)
            q_g = jnp.stack([attn_q_g[j], _rope_partner(attn_q_g[j])])
            k_g = jnp.stack([attn_k_g[j], _rope_partner(attn_k_g[j])])
            q, k, v = _qkv_proj(h_pair, mod, g1, w_qkv.astype(bf16), w_partner, q_g, k_g, cos_t, sin_t)
            moe_layer = i if i % 2 == 1 else i + 1
            nxt = moe_layer // 2
            casts = [(w, nxt) for w in (moe_w_gate, moe_w_up, moe_w_down)]
            if ffn_w is None:
                casts += [(w, None) for w in (ffn_w_gate, ffn_w_up, ffn_w_down)]
            o, cast_w = _attention(q, k, v, casts)
            moe_w = cast_w[:3]
            if ffn_w is None:
                ffn_w = cast_w[3:]
            w_o = attn_w_o[j].astype(bf16)
            if need_ctx:
                h = _proj_res2((o, _attention_ctx(q, k, v)), w_o, h_pair, mod)
            else:
                h = _proj_res(o, w_o, h, mod, nt)
        elif kind == 1:
            w_in = ssd_w_in[j]
            wz = w_in[:, :SSD_INNER].astype(bf16)
            wx = w_in[:, SSD_INNER:SSD_INNER + SSD_CONV_CH].astype(bf16)
            w_dt = w_in[:, SSD_INNER + SSD_CONV_CH:]
            wdt = jnp.concatenate([_pad_lanes(w_dt[:, :SSD_HEADS]), _pad_lanes(w_dt[:, SSD_HEADS:])], axis=1).astype(bf16)
            conv_w = jnp.pad(ssd_conv_w[j], ((0, 8 - ssd_conv_w.shape[1]), (0, 0)))
            z, xbc, dt = _ssd_in_proj(h, mod, g1, wz, wx, wdt, conv_w, _row(ssd_conv_b[j]))
            dt_bias = jnp.stack([_pad_lanes(_row(ssd_dt_bias_f[j])), _pad_lanes(_row(ssd_dt_bias_b[j]))])
            a_log = jnp.stack([_pad_lanes(_row(ssd_a_log_f[j])), _pad_lanes(_row(ssd_a_log_b[j]))])
            yf, yb = _ssd_scan(xbc, dt, dt_bias, a_log)
            dskip = _row(jnp.repeat(ssd_d_skip[j], SSD_P))
            h = _ssd_out(yf, yb, xbc, z, dskip, _row(ssd_norm_g[j]), ssd_w_out[j].astype(bf16), h, mod, nt)
        else:
            w_in = cmlp_w_in[j]
            b_in = cmlp_b_in[j]
            bs = jnp.repeat(cmlp_b_s[j].T, CMLP_GW, axis=1)
            h = _cmlp(h, mod, g1, w_in[:, :CMLP_D].astype(bf16), w_in[:, CMLP_D:].astype(bf16),
                      _row(b_in[:CMLP_D]), _row(b_in[CMLP_D:]), _row(cmlp_v_g[j]), cmlp_w_s[j].astype(bf16), bs,
                      cmlp_w_out[j].astype(bf16), nt)
        kk = i // 2
        g2 = _row(norm2_g[i])
        if i % 2 == 0:
            h = _ffn(h, mod, g2, *ffn_w, kk, nt)
        else:
            h = _moe(h, mod, g2, _pad_lanes(moe_router[kk]), *moe_w, nt, final_row, i == DEPTH - 1)
        h_pair = (h, h)
    return h.reshape(BATCH, SEQ, D)
```

```python
import functools
import math

import jax
import jax.numpy as jnp
import numpy as np
from jax import lax
from jax.experimental import pallas as pl
from jax.experimental.pallas import tpu as pltpu

f32 = jnp.float32
bf16 = jnp.bfloat16

D = 1024
BATCH = 2
SEQ = 8192
CTX = 256
DEPTH = 4
GRID_W = 64
EPS = 1e-6
N_MOD = 6

HEAD_DIM = 128
N_HEADS = 8
N_KV = 2
Q_GROUP = 4
ROPE_AXIS_DIM = 64
ROPE_THETA = 10000.0

SSD_INNER = 2048
SSD_P = 64
SSD_HEADS = 32
SSD_GROUPS = 4
SSD_HPG = 8
SSD_N = 128
SSD_CHUNK = 128
SSD_BC = SSD_GROUPS * SSD_N
SSD_CONV_CH = SSD_INNER + 2 * SSD_BC

CMLP_D = 2048
CMLP_GROUPS = 8
CMLP_GW = 256
CMLP_CHUNK = 128

D_FF = 3584
N_EXPERTS = 8

N_LAT = BATCH * SEQ
N_CTX = BATCH * CTX
T = N_LAT + N_CTX
TM = 512
NT = T // TM
NT_LAT = N_LAT // TM
TILES_PER_SAMPLE = SEQ // TM
CTX_MOD_ROW = BATCH

TF = 1792
NF = D_FF // TF

TQ = 512
TK = 512
ATT_CHUNKS = SEQ // TK
VW = 2 * HEAD_DIM
LOG2E = math.log2(math.e)
NQ_LAT = SEQ // TQ

VMEM_LIMIT = 56 * 1024 * 1024


def _mod_row(t):
    return jnp.minimum(t // TILES_PER_SAMPLE, CTX_MOD_ROW)


def _tile_spec(width):
    return pl.BlockSpec((TM, width), lambda t: (t, 0))


def _pair_specs(pair):
    lat, ctx = pair
    ctx_block = 0 if ctx.shape[0] == TM else NT_LAT
    return [pl.BlockSpec((TM, lat.shape[1]), lambda t: (jnp.minimum(t, NT_LAT - 1), 0)),
            pl.BlockSpec((TM, ctx.shape[1]), lambda t: (ctx_block, 0))]


def _pick_tile(lat_ref, ctx_ref):
    return jnp.where(pl.program_id(0) < NT_LAT, lat_ref[...], ctx_ref[...])


def _mod_spec():
    return pl.BlockSpec((None, 8, D), lambda t: (_mod_row(t), 0, 0))


def _const_spec(shape):
    n = len(shape)
    return pl.BlockSpec(shape, lambda *_: (0,) * n)


def _params(semantics):
    return pltpu.CompilerParams(dimension_semantics=semantics, vmem_limit_bytes=VMEM_LIMIT)


def _silu(x):
    return x * jax.nn.sigmoid(x)


def _norm_mod(x, g, shift, scale):
    y = x * lax.rsqrt(jnp.mean(x * x, axis=-1, keepdims=True) + EPS) * g
    return y * (1.0 + scale) + shift


def _mod_kernel(c_ref, w_ref, b_ref, o_ref):
    s = _silu(c_ref[...])
    o_ref[...] = jnp.dot(s.astype(bf16), w_ref[...].astype(bf16), preferred_element_type=f32) + b_ref[...]


def _modulation(cond, w_mod, b_mod):
    nblk = N_MOD * D // D
    out = pl.pallas_call(
        _mod_kernel,
        grid=(DEPTH, nblk),
        in_specs=[
            pl.BlockSpec((8, D), lambda i, j: (0, 0)),
            pl.BlockSpec((None, D, D), lambda i, j: (i, 0, j)),
            pl.BlockSpec((None, 1, D), lambda i, j: (i, 0, j)),
        ],
        out_specs=pl.BlockSpec((None, 8, D), lambda i, j: (i, 0, j)),
        out_shape=jax.ShapeDtypeStruct((DEPTH, 8, N_MOD * D), f32),
        compiler_params=_params(("arbitrary", "arbitrary")),
        name="modulation",
    )(cond, w_mod, b_mod.reshape(DEPTH, 1, N_MOD * D))
    mod = out[:, :3].reshape(DEPTH, 3, N_MOD, D)
    return jnp.pad(mod, ((0, 0), (0, 0), (0, 8 - N_MOD), (0, 0)))


def _qkv_kernel(hl_ref, hc_ref, mod_ref, g_ref, w_ref, wp_ref, qg_ref, kg_ref, cos_ref, sin_ref,
                q_ref, k_ref, v_ref):
    h = _pick_tile(hl_ref, hc_ref)
    a = _norm_mod(h, g_ref[...], mod_ref[0:1, :], mod_ref[1:2, :]).astype(bf16)
    qkv = jnp.dot(a, w_ref[...], preferred_element_type=f32)
    qkp = jnp.dot(a, wp_ref[...], preferred_element_type=f32)
    cos = cos_ref[...]
    sin = sin_ref[...]

    def head(x, xp, g, scale):
        r = lax.rsqrt(jnp.mean(x * x, axis=-1, keepdims=True) + EPS) * scale
        return (x * g[0:1, :] * cos + xp * g[1:2, :] * sin) * r

    qg = qg_ref[...]
    kg = kg_ref[...]
    for hh in range(N_HEADS):
        sl = slice(hh * HEAD_DIM, (hh + 1) * HEAD_DIM)
        q_ref[:, sl] = head(qkv[:, sl], qkp[:, sl], qg, HEAD_DIM ** -0.5 * LOG2E).astype(bf16)
    for hh in range(N_KV):
        sl = slice(hh * HEAD_DIM, (hh + 1) * HEAD_DIM)
        src = slice((N_HEADS + hh) * HEAD_DIM, (N_HEADS + hh + 1) * HEAD_DIM)
        k_ref[:, sl] = head(qkv[:, src], qkp[:, src], kg, 1.0).astype(bf16)
        v_ref[:, hh * VW:hh * VW + HEAD_DIM] = qkv[:, (N_HEADS + N_KV + hh) * HEAD_DIM:(N_HEADS + N_KV + hh + 1) * HEAD_DIM].astype(bf16)
        v_ref[:, hh * VW + HEAD_DIM:(hh + 1) * VW] = jnp.ones((TM, HEAD_DIM), bf16)


def _rope_partner(w):
    lead = w.shape[:-1]
    w5 = w.reshape(lead + (-1, 2, 2, ROPE_AXIS_DIM // 2))
    return w5[..., ::-1, :].reshape(w.shape)


def _qkv_proj(h_pair, mod, g, w_qkv, w_partner, q_g, k_g, cos_t, sin_t):
    rope_spec = pl.BlockSpec((TM, HEAD_DIM), lambda t: (jnp.where(t < NT_LAT, t % TILES_PER_SAMPLE, TILES_PER_SAMPLE), 0))
    return pl.pallas_call(
        _qkv_kernel,
        grid=(NT,),
        in_specs=[
            *_pair_specs(h_pair), _mod_spec(), _const_spec((1, D)),
            _const_spec((D, (N_HEADS + 2 * N_KV) * HEAD_DIM)), _const_spec((D, (N_HEADS + N_KV) * HEAD_DIM)),
            _const_spec((2, HEAD_DIM)), _const_spec((2, HEAD_DIM)), rope_spec, rope_spec,
        ],
        out_specs=[_tile_spec(N_HEADS * HEAD_DIM), _tile_spec(N_KV * HEAD_DIM), _tile_spec(N_KV * VW)],
        out_shape=[
            jax.ShapeDtypeStruct((T, N_HEADS * HEAD_DIM), bf16),
            jax.ShapeDtypeStruct((T, N_KV * HEAD_DIM), bf16),
            jax.ShapeDtypeStruct((T, N_KV * VW), bf16),
        ],
        compiler_params=_params(("parallel",)),
        name="attn_qkv",
    )(*h_pair, mod, g, w_qkv, w_partner, q_g, k_g, cos_t, sin_t)


def _stack_heads(q):
    return jnp.concatenate([q[:, i * HEAD_DIM:(i + 1) * HEAD_DIM] for i in range(Q_GROUP)], axis=0)


def _scores(qs, k):
    return lax.dot_general(qs, k, (((1,), (1,)), ((), ())), preferred_element_type=f32)


def _attn_kernel(q_ref, kl_ref, vl_ref, kc_ref, vc_ref, *rest, n_cast):
    qs = _stack_heads(q_ref[...])

    w_in, o_ref, w_out = rest[:n_cast], rest[n_cast], rest[n_cast + 1:2 * n_cast + 1]
    m_sc, acc_sc, s_a, s_b, s_c = rest[2 * n_cast + 1:]

    for src, dst in zip(w_in, w_out):
        dst[...] = src[...].astype(bf16)

    def absorb(s, v):
        m_prev = m_sc[...]
        m_next = jnp.maximum(m_prev, jnp.max(s, axis=1, keepdims=True))
        alpha = jnp.exp2(m_prev - m_next)
        p = jnp.exp2(s - jnp.concatenate([m_next] * (s.shape[1] // HEAD_DIM), axis=1))
        acc_sc[...] = (jnp.concatenate([alpha, alpha], axis=1) * acc_sc[...]
                       + jnp.dot(p.astype(bf16), v, preferred_element_type=f32))
        m_sc[...] = m_next

    def chunk(ref, c):
        return ref[pl.ds(pl.multiple_of(c * TK, TK), TK), :]

    m_sc[...] = jnp.full_like(m_sc, -jnp.inf)
    acc_sc[...] = jnp.zeros_like(acc_sc)
    s_c[...] = _scores(qs, kc_ref[...])
    s_a[...] = _scores(qs, chunk(kl_ref, 0))

    def body(c2, carry):
        c = 2 * c2
        s_b[...] = _scores(qs, chunk(kl_ref, c + 1))
        absorb(s_a[...], chunk(vl_ref, c))
        s_a[...] = _scores(qs, chunk(kl_ref, c + 2))
        absorb(s_b[...], chunk(vl_ref, c + 1))
        return carry
    lax.fori_loop(0, ATT_CHUNKS // 2 - 1, body, 0)

    s_b[...] = _scores(qs, chunk(kl_ref, ATT_CHUNKS - 1))
    absorb(s_a[...], chunk(vl_ref, ATT_CHUNKS - 2))
    absorb(s_b[...], chunk(vl_ref, ATT_CHUNKS - 1))
    absorb(s_c[...], vc_ref[...])

    acc = acc_sc[...]
    o = acc[:, :HEAD_DIM] / acc[:, HEAD_DIM:]
    for i in range(Q_GROUP):
        o_ref[:, i * HEAD_DIM:(i + 1) * HEAD_DIM] = o[i * TQ:(i + 1) * TQ].astype(bf16)


def _attn_ctx_kernel(q_ref, kc_ref, vc_ref, o_ref):
    s = _scores(_stack_heads(q_ref[...]), kc_ref[...])
    p = jnp.exp2(s - jnp.max(s, axis=1, keepdims=True))
    acc = jnp.dot(p.astype(bf16), vc_ref[...], preferred_element_type=f32)
    o = acc[:, :HEAD_DIM] / acc[:, HEAD_DIM:]
    for i in range(Q_GROUP):
        o_ref[:, i * HEAD_DIM:(i + 1) * HEAD_DIM] = o[i * CTX:(i + 1) * CTX].astype(bf16)


def _attention(q, k, v, casts):
    ctx_blk = N_LAT // CTX
    steps = BATCH * N_KV * NQ_LAT
    slabs, firsts, out_shapes = [], [], []
    for w, layer in casts:
        n_parts = steps if layer is None else w.shape[0] * steps
        slabs.append(w.reshape(n_parts, -1, w.shape[-1]))
        firsts.append(0 if layer is None else layer * steps)
        out_shapes.append(w.shape if layer is None else w.shape[1:])

    def slab_spec(w, first):
        return pl.BlockSpec((None,) + w.shape[1:],
                            lambda b, kh, qi: (first + (b * N_KV + kh) * NQ_LAT + qi, 0, 0))

    def q_map(b, kh, qi):
        return (b * NQ_LAT + qi, kh)

    def lat_spec(width):
        return pl.BlockSpec((SEQ, width), lambda b, kh, qi: (b, kh), pipeline_mode=pl.Buffered(1))

    def ctx_spec(width):
        return pl.BlockSpec((CTX, width), lambda b, kh, qi: (ctx_blk + b, kh))

    rows = Q_GROUP * TQ
    out = pl.pallas_call(
        functools.partial(_attn_kernel, n_cast=len(slabs)),
        grid=(BATCH, N_KV, NQ_LAT),
        in_specs=[pl.BlockSpec((TQ, Q_GROUP * HEAD_DIM), q_map), lat_spec(HEAD_DIM), lat_spec(VW),
                  ctx_spec(HEAD_DIM), ctx_spec(VW)] + [slab_spec(w, f) for w, f in zip(slabs, firsts)],
        out_specs=[pl.BlockSpec((TQ, Q_GROUP * HEAD_DIM), q_map)] + [slab_spec(w, 0) for w in slabs],
        out_shape=[jax.ShapeDtypeStruct((N_LAT, N_HEADS * HEAD_DIM), bf16)]
        + [jax.ShapeDtypeStruct((steps,) + w.shape[1:], bf16) for w in slabs],
        scratch_shapes=[pltpu.VMEM((rows, HEAD_DIM), f32), pltpu.VMEM((rows, VW), f32),
                        pltpu.VMEM((rows, TK), f32), pltpu.VMEM((rows, TK), f32), pltpu.VMEM((rows, CTX), f32)],
        compiler_params=_params(("parallel", "parallel", "arbitrary")),
        name="attn_core",
    )(q, k, v, k, v, *slabs)
    return out[0], [o.reshape(shape) for o, shape in zip(out[1:], out_shapes)]


def _attention_ctx(q, k, v):
    ctx_blk = N_LAT // CTX
    return pl.pallas_call(
        _attn_ctx_kernel,
        grid=(BATCH, N_KV),
        in_specs=[pl.BlockSpec((CTX, Q_GROUP * HEAD_DIM), lambda b, kh: (ctx_blk + b, kh)),
                  pl.BlockSpec((CTX, HEAD_DIM), lambda b, kh: (ctx_blk + b, kh)),
                  pl.BlockSpec((CTX, VW), lambda b, kh: (ctx_blk + b, kh))],
        out_specs=pl.BlockSpec((CTX, Q_GROUP * HEAD_DIM), lambda b, kh: (b, kh)),
        out_shape=jax.ShapeDtypeStruct((N_CTX, N_HEADS * HEAD_DIM), bf16),
        compiler_params=_params(("parallel", "parallel")),
        name="attn_ctx",
    )(q, k, v)


def _proj_res_kernel(y_ref, w_ref, h_ref, mod_ref, o_ref):
    y = jnp.dot(y_ref[...], w_ref[...], preferred_element_type=f32)
    o_ref[...] = h_ref[...] + mod_ref[2:3, :] * y


def _proj_res2_kernel(yl_ref, yc_ref, w_ref, hl_ref, hc_ref, mod_ref, o_ref):
    y = jnp.dot(_pick_tile(yl_ref, yc_ref), w_ref[...], preferred_element_type=f32)
    o_ref[...] = _pick_tile(hl_ref, hc_ref) + mod_ref[2:3, :] * y


def _proj_res(y, w, h, mod, nt):
    kdim = y.shape[1]
    return pl.pallas_call(
        _proj_res_kernel,
        grid=(nt,),
        in_specs=[_tile_spec(kdim), _const_spec((kdim, D)), _tile_spec(D), _mod_spec()],
        out_specs=_tile_spec(D),
        out_shape=jax.ShapeDtypeStruct((nt * TM, D), f32),
        compiler_params=_params(("parallel",)),
        name="proj_res",
    )(y, w, h, mod)


def _proj_res2(y_pair, w, h_pair, mod):
    kdim = y_pair[0].shape[1]
    return pl.pallas_call(
        _proj_res2_kernel,
        grid=(NT,),
        in_specs=[*_pair_specs(y_pair), _const_spec((kdim, D)), *_pair_specs(h_pair), _mod_spec()],
        out_specs=_tile_spec(D),
        out_shape=jax.ShapeDtypeStruct((T, D), f32),
        compiler_params=_params(("parallel",)),
        name="proj_res2",
    )(*y_pair, w, *h_pair, mod)


def _ffn_kernel(h_ref, mod_ref, g_ref, wg_ref, wu_ref, wd_ref, o_ref):
    x = _norm_mod(h_ref[...], g_ref[...], mod_ref[3:4, :], mod_ref[4:5, :]).astype(bf16)
    acc = None
    for f in range(NF):
        cols = slice(f * TF, (f + 1) * TF)
        gate = jnp.dot(x, wg_ref[:, cols], preferred_element_type=f32)
        up = jnp.dot(x, wu_ref[:, cols], preferred_element_type=f32)
        hid = (_silu(gate) * up).astype(bf16)
        part = jnp.dot(hid, wd_ref[cols, :], preferred_element_type=f32)
        acc = part if acc is None else acc + part
    o_ref[...] = h_ref[...] + mod_ref[5:6, :] * acc


def _ffn(h, mod, g, wg, wu, wd, layer, nt):
    def resident(shape):
        return pl.BlockSpec((None,) + shape, lambda t: (layer, 0, 0), pipeline_mode=pl.Buffered(1))

    return pl.pallas_call(
        _ffn_kernel,
        grid=(nt,),
        in_specs=[_tile_spec(D), _mod_spec(), _const_spec((1, D)),
                  resident((D, D_FF)), resident((D, D_FF)), resident((D_FF, D))],
        out_specs=_tile_spec(D),
        out_shape=jax.ShapeDtypeStruct((nt * TM, D), f32),
        compiler_params=_params(("parallel",)),
        name="ffn_dense",
    )(h, mod, g, wg, wu, wd)


def _split_bf16(x):
    hi = x.astype(bf16)
    lo = (x - hi.astype(f32)).astype(bf16)
    return hi, lo


def _split3(x):
    p0 = x.astype(bf16)
    r = x - p0.astype(f32)
    p1 = r.astype(bf16)
    p2 = (r - p1.astype(f32)).astype(bf16)
    return p0, p1, p2


def _dot3(a_parts, b):
    out = jnp.dot(a_parts[0], b, preferred_element_type=f32)
    for a in a_parts[1:]:
        out = out + jnp.dot(a, b, preferred_element_type=f32)
    return out


RT = 256
RT_PER_SAMPLE = SEQ // RT
ALIGN = 16
WIN = RT + ALIGN
FT = 512
HEAD_ROWS = 128
SLAB = 64
STAGE = FT + -(-WIN // SLAB) * SLAB
CAP = 17408
CAP_BLOCKS = CAP // FT
DN_T = (((0,), (0,)), ((), ()))
NO_ROW = -1e9


def _route_kernel(h_ref, mod_ref, g_ref, r_ref, xs_hbm, gs_hbm, rank_ref, seg_ref, cnt_ref,
                  x_stage, g_stage, cnt_sc, pend_sc, dst_sc, sems, *, n_rt):
    i = pl.program_id(0)

    @pl.when(i == 0)
    def _():
        x_stage[...] = jnp.zeros_like(x_stage)
        g_stage[...] = jnp.zeros_like(g_stage)
        for e in range(N_EXPERTS):
            cnt_sc[e] = 0
            pend_sc[e] = 0

    x = _norm_mod(h_ref[...], g_ref[...], mod_ref[3:4, :], mod_ref[4:5, :])
    xh, xl = _split_bf16(x)
    rh, rl = _split_bf16(r_ref[...])
    logits = (jnp.dot(xh, rh, preferred_element_type=f32) + jnp.dot(xl, rh, preferred_element_type=f32)
              + jnp.dot(xh, rl, preferred_element_type=f32))
    lane = lax.broadcasted_iota(jnp.int32, (RT, 128), 1)
    lg = jnp.where(lane < N_EXPERTS, logits, -jnp.inf)
    m1 = jnp.max(lg, axis=1, keepdims=True)
    i1 = jnp.min(jnp.where(lg == m1, lane, 128), axis=1, keepdims=True)
    lg2 = jnp.where(lane == i1, -jnp.inf, lg)
    m2 = jnp.max(lg2, axis=1, keepdims=True)
    i2 = jnp.min(jnp.where(lg2 == m2, lane, 128), axis=1, keepdims=True)
    e2 = jnp.exp(m2 - m1)
    den = 1.0 + e2
    gates = jnp.where(lane == i1, 1.0 / den, 0.0) + jnp.where(lane == i2, e2 / den, 0.0)
    used = jnp.where((lane == i1) | (lane == i2), 1.0, 0.0).astype(bf16)

    tp = lax.broadcasted_iota(jnp.int32, (RT, RT), 0)
    tt = lax.broadcasted_iota(jnp.int32, (RT, RT), 1)
    earlier = jnp.where(tp < tt, 1.0, 0.0).astype(bf16)
    eye = jnp.where(tp == tt, 1.0, 0.0).astype(bf16)
    rank_t = lax.dot_general(used, earlier, DN_T, preferred_element_type=f32)
    used_t = lax.dot_general(used, eye, DN_T, preferred_element_type=f32)
    rank_t = jnp.where(used_t > 0, rank_t, NO_ROW)
    rank_ref[...] = rank_t[0:N_EXPERTS]

    g_hi, g_lo = _split_bf16(gates)
    xg = jnp.concatenate([xh, g_hi, g_lo], axis=1)
    head_row = lax.broadcasted_iota(jnp.int32, (HEAD_ROWS, RT), 0).astype(f32)
    slab_row = lax.broadcasted_iota(jnp.int32, (SLAB, RT), 0).astype(f32)

    def flush_copies(e, done_rows):
        dst = pl.multiple_of(e * CAP + done_rows, FT)
        return (pltpu.make_async_copy(x_stage.at[e, pl.ds(0, FT), :], xs_hbm.at[pl.ds(dst, FT), :], sems.at[e, 0]),
                pltpu.make_async_copy(g_stage.at[e, pl.ds(0, FT), :], gs_hbm.at[pl.ds(dst, FT), :], sems.at[e, 1]))

    def settle(e):
        @pl.when(pend_sc[e] == 1)
        def _():
            for copy in flush_copies(e, dst_sc[e]):
                copy.wait()
            x_stage[e, 0:STAGE - FT, :] = x_stage[e, FT:STAGE, :]
            x_stage[e, STAGE - FT:STAGE, :] = jnp.zeros((FT, D), bf16)
            g_stage[e, 0:STAGE - FT, :] = g_stage[e, FT:STAGE, :]
            g_stage[e, STAGE - FT:STAGE, :] = jnp.zeros((FT, 128), f32)
            pend_sc[e] = 0

    for e in range(N_EXPERTS):
        settle(e)

    state = []
    n_of = []
    for e in range(N_EXPERTS):
        cnt = cnt_sc[e]
        seg_ref[i * N_EXPERTS + e] = cnt
        fill = cnt % FT
        start = pl.multiple_of((fill // ALIGN) * ALIGN, ALIGN)
        n_e = jnp.sum(used_t[e:e + 1, :]).astype(jnp.int32)
        cnt_sc[e] = cnt + n_e
        n_of.append(n_e)
        state.append((start, fill - start, fill + n_e >= FT, cnt - fill))

    def target_rows(e):
        return rank_t[e:e + 1, :] + state[e][1].astype(f32)

    for e in range(N_EXPERTS):
        start = state[e][0]
        p = jnp.where(head_row == target_rows(e), 1.0, 0.0).astype(bf16)
        rows = jnp.dot(p, xg, preferred_element_type=f32)
        grow = rows[:, D:D + 128] + rows[:, D + 128:]
        old = pl.ds(start, ALIGN)
        new = pl.ds(pl.multiple_of(start + ALIGN, ALIGN), HEAD_ROWS - ALIGN)
        x_stage[e, old, :] = (x_stage[e, old, :].astype(f32) + rows[:ALIGN, :D]).astype(bf16)
        x_stage[e, new, :] = rows[ALIGN:, :D].astype(bf16)
        g_stage[e, old, :] = g_stage[e, old, :] + grow[:ALIGN]
        g_stage[e, new, :] = grow[ALIGN:]

    crowded = False
    for e in range(N_EXPERTS):
        crowded = crowded | (state[e][1] + n_of[e] > HEAD_ROWS)

    @pl.when(crowded)
    def _():
        for e in range(N_EXPERTS):
            def place(sl, carry, e=e):
                p = jnp.where(slab_row + (sl * SLAB).astype(f32) == target_rows(e), 1.0, 0.0).astype(bf16)
                rows = jnp.dot(p, xg, preferred_element_type=f32)
                dst = pl.ds(pl.multiple_of(state[e][0] + sl * SLAB, ALIGN), SLAB)
                x_stage[e, dst, :] = rows[:, :D].astype(bf16)
                g_stage[e, dst, :] = rows[:, D:D + 128] + rows[:, D + 128:]
                return carry
            lax.fori_loop(HEAD_ROWS // SLAB, (state[e][1] + n_of[e] + SLAB - 1) // SLAB, place, 0)

    for e in range(N_EXPERTS):
        @pl.when(state[e][2])
        def _():
            for copy in flush_copies(e, state[e][3]):
                copy.start()
            pend_sc[e] = 1
            dst_sc[e] = state[e][3]

    @pl.when(i == n_rt - 1)
    def _():
        for e in range(N_EXPERTS):
            settle(e)
            c = cnt_sc[e]
            cnt_ref[e] = c
            for copy in flush_copies(e, c - c % FT):
                copy.start()
                copy.wait()


def _moe_route(h, mod, g, router, n_rt):
    return pl.pallas_call(
        functools.partial(_route_kernel, n_rt=n_rt),
        grid=(n_rt,),
        in_specs=[
            pl.BlockSpec((RT, D), lambda i: (i, 0)),
            pl.BlockSpec((None, 8, D), lambda i: (jnp.minimum(i // RT_PER_SAMPLE, CTX_MOD_ROW), 0, 0)),
            _const_spec((1, D)), _const_spec((D, 128)),
        ],
        out_specs=[
            pl.BlockSpec(memory_space=pl.ANY), pl.BlockSpec(memory_space=pl.ANY),
            pl.BlockSpec((None, N_EXPERTS, RT), lambda i: (i, 0, 0)),
            pl.BlockSpec(memory_space=pltpu.SMEM), pl.BlockSpec(memory_space=pltpu.SMEM),
        ],
        out_shape=[
            jax.ShapeDtypeStruct((N_EXPERTS * CAP, D), bf16),
            jax.ShapeDtypeStruct((N_EXPERTS * CAP, 128), f32),
            jax.ShapeDtypeStruct((n_rt, N_EXPERTS, RT), f32),
            jax.ShapeDtypeStruct((n_rt * N_EXPERTS,), jnp.int32),
            jax.ShapeDtypeStruct((N_EXPERTS,), jnp.int32),
        ],
        scratch_shapes=[
            pltpu.VMEM((N_EXPERTS, STAGE, D), bf16), pltpu.VMEM((N_EXPERTS, STAGE, 128), f32),
            pltpu.SMEM((N_EXPERTS,), jnp.int32), pltpu.SMEM((N_EXPERTS,), jnp.int32),
            pltpu.SMEM((N_EXPERTS,), jnp.int32), pltpu.SemaphoreType.DMA((N_EXPERTS, 2)),
        ],
        compiler_params=_params(("arbitrary",)),
        name="moe_route",
    )(h, mod, g, router)


def _moe_ffn_kernel(te_ref, tb_ref, nt_ref, x_ref, gs_ref, wg_ref, wu_ref, wd_ref, y_ref, acc_sc):
    j = pl.program_id(0)
    f = pl.program_id(1)

    @pl.when(j < nt_ref[0])
    def _():
        @pl.when(f == 0)
        def _():
            acc_sc[...] = jnp.zeros_like(acc_sc)

        x = x_ref[...]
        gate = jnp.dot(x, wg_ref[...], preferred_element_type=f32)
        up = jnp.dot(x, wu_ref[...], preferred_element_type=f32)
        hid = (_silu(gate) * up).astype(bf16)
        acc_sc[...] += jnp.dot(hid, wd_ref[...], preferred_element_type=f32)

        @pl.when(f == NF - 1)
        def _():
            lane = lax.broadcasted_iota(jnp.int32, (FT, 128), 1)
            ge = jnp.sum(jnp.where(lane == te_ref[j], gs_ref[...], 0.0), axis=1, keepdims=True)
            y_ref[...] = (ge * acc_sc[...]).astype(bf16)


def _moe_experts(xs, gs, wg, wu, wd, tile_expert, tile_block, n_tiles):
    max_tiles = tile_expert.shape[0]

    def f_eff(j, f, nt):
        return jnp.where(j < nt[0], f, NF - 1)

    grid_spec = pltpu.PrefetchScalarGridSpec(
        num_scalar_prefetch=3,
        grid=(max_tiles, NF),
        in_specs=[
            pl.BlockSpec((FT, D), lambda j, f, te, tb, nt: (tb[j], 0)),
            pl.BlockSpec((FT, 128), lambda j, f, te, tb, nt: (tb[j], 0)),
            pl.BlockSpec((None, D, TF), lambda j, f, te, tb, nt: (te[j], 0, f_eff(j, f, nt))),
            pl.BlockSpec((None, D, TF), lambda j, f, te, tb, nt: (te[j], 0, f_eff(j, f, nt))),
            pl.BlockSpec((None, TF, D), lambda j, f, te, tb, nt: (te[j], f_eff(j, f, nt), 0)),
        ],
        out_specs=pl.BlockSpec((FT, D), lambda j, f, te, tb, nt: (tb[j], 0)),
        scratch_shapes=[pltpu.VMEM((FT, D), f32)],
    )
    return pl.pallas_call(
        _moe_ffn_kernel,
        grid_spec=grid_spec,
        out_shape=jax.ShapeDtypeStruct((N_EXPERTS * CAP, D), bf16),
        compiler_params=_params(("arbitrary", "arbitrary")),
        name="moe_experts",
    )(tile_expert, tile_block, n_tiles, xs, gs, wg, wu, wd)


def _moe_combine_kernel(seg_ref, nseg_ref, wmax_ref, rank_ref, h_ref, mod_ref, fg_ref, ys_hbm, o_ref, ybuf, acc_sc,
                        sems, *, n_rt, apply_final):
    i = pl.program_id(0)
    main_row = lax.broadcasted_iota(jnp.int32, (RT, RT), 0).astype(f32)
    last_row = lax.broadcasted_iota(jnp.int32, (ALIGN, RT), 0).astype(f32) + RT

    def window(step, e):
        seg = seg_ref[step * N_EXPERTS + e]
        start = pl.multiple_of(jnp.minimum((seg // ALIGN) * ALIGN, wmax_ref[e]), ALIGN)
        return seg, start

    def window_copy(step, e):
        _, start = window(step, e)
        return pltpu.make_async_copy(ys_hbm.at[pl.ds(e * CAP + start, WIN), :], ybuf.at[step % 2, e],
                                     sems.at[step % 2, e])

    def fetch(step):
        for e in range(N_EXPERTS):
            @pl.when(nseg_ref[step * N_EXPERTS + e] > 0)
            def _():
                window_copy(step, e).start()

    @pl.when(i == 0)
    def _():
        ybuf[...] = jnp.zeros_like(ybuf)
        fetch(0)

    @pl.when(i + 1 < n_rt)
    def _():
        fetch(i + 1)

    acc = None
    spill = False
    for e in range(N_EXPERTS):
        seg, start = window(i, e)
        n = nseg_ref[i * N_EXPERTS + e]

        @pl.when(n > 0)
        def _():
            window_copy(i, e).wait()

        target = rank_ref[e:e + 1, :] + (seg - start).astype(f32)
        p = jnp.where(main_row == target, 1.0, 0.0).astype(bf16)
        part = lax.dot_general(p, ybuf[i % 2, e, 0:RT, :], DN_T, preferred_element_type=f32)
        acc = part if acc is None else acc + part
        spill = spill | (seg - start + n > RT)
    acc_sc[...] = acc

    @pl.when(spill)
    def _():
        for e in range(N_EXPERTS):
            seg, start = window(i, e)
            target = rank_ref[e:e + 1, :] + (seg - start).astype(f32)
            p = jnp.where(last_row == target, 1.0, 0.0).astype(bf16)
            acc_sc[...] += lax.dot_general(p, ybuf[i % 2, e, RT:WIN, :], DN_T, preferred_element_type=f32)

    out = h_ref[...] + mod_ref[5:6, :] * acc_sc[...]
    if apply_final:
        out = out * lax.rsqrt(jnp.mean(out * out, axis=-1, keepdims=True) + EPS) * fg_ref[...]
    o_ref[...] = out


def _moe_combine(ys, rank, seg, nseg, wmax, h, mod, final_g, n_rt, apply_final):
    grid_spec = pltpu.PrefetchScalarGridSpec(
        num_scalar_prefetch=3,
        grid=(n_rt,),
        in_specs=[
            pl.BlockSpec((None, N_EXPERTS, RT), lambda i, *_: (i, 0, 0)),
            pl.BlockSpec((RT, D), lambda i, *_: (i, 0)),
            pl.BlockSpec((None, 8, D), lambda i, *_: (jnp.minimum(i // RT_PER_SAMPLE, CTX_MOD_ROW), 0, 0)),
            pl.BlockSpec((1, D), lambda i, *_: (0, 0)),
            pl.BlockSpec(memory_space=pl.ANY),
        ],
        out_specs=pl.BlockSpec((RT, D), lambda i, *_: (i, 0)),
        scratch_shapes=[pltpu.VMEM((2, N_EXPERTS, WIN, D), bf16), pltpu.VMEM((RT, D), f32),
                        pltpu.SemaphoreType.DMA((2, N_EXPERTS))],
    )
    return pl.pallas_call(
        functools.partial(_moe_combine_kernel, n_rt=n_rt, apply_final=apply_final),
        grid_spec=grid_spec,
        out_shape=jax.ShapeDtypeStruct((n_rt * RT, D), f32),
        compiler_params=_params(("arbitrary",)),
        name="moe_combine",
    )(seg, nseg, wmax, rank, h, mod, final_g, ys)


def _moe(h, mod, g, router, wg, wu, wd, nt, final_g, apply_final):
    n_rt = nt * (TM // RT)
    xs, gs, rank, seg, counts = _moe_route(h, mod, g, router, n_rt)
    tiles = (counts + FT - 1) // FT
    ends = jnp.cumsum(tiles)
    n_tiles = ends[-1]
    max_tiles = (2 * n_rt * RT) // FT + N_EXPERTS
    j = jnp.minimum(jnp.arange(max_tiles, dtype=jnp.int32), n_tiles - 1)
    tile_expert = jnp.sum((j[:, None] >= ends[None, :]).astype(jnp.int32), axis=1)
    tile_block = tile_expert * CAP_BLOCKS + j - (ends - tiles)[tile_expert]
    ys = _moe_experts(xs, gs, wg, wu, wd, tile_expert, tile_block, n_tiles.reshape(1))
    seg2 = seg.reshape(n_rt, N_EXPERTS)
    nseg = (jnp.concatenate([seg2[1:], counts[None, :]], axis=0) - seg2).reshape(-1)
    wmax = jnp.maximum(tiles * FT - WIN, 0)
    return _moe_combine(ys, rank, seg, nseg, wmax, h, mod, final_g, n_rt, apply_final)


HALO = 8
CONV_COLS = 256


def _ssd_in_kernel(h_ref, hp_ref, hn_ref, mod_ref, g_ref, wz_ref, wx_ref, wdt_ref, cw_ref, cb_ref,
                   z_ref, x_ref, dt_ref):
    t = pl.program_id(0)
    is_ctx = t >= NT_LAT
    seg_start = is_ctx | (t % TILES_PER_SAMPLE == 0)
    seg_end = is_ctx | (t % TILES_PER_SAMPLE == TILES_PER_SAMPLE - 1)
    shift, scale, g = mod_ref[0:1, :], mod_ref[1:2, :], g_ref[...]
    a = _norm_mod(h_ref[...], g, shift, scale).astype(bf16)
    halo = jnp.concatenate([hp_ref[...], hn_ref[...]], axis=0)
    a_halo = _norm_mod(halo, g, shift, scale).astype(bf16)

    for j in range(SSD_INNER // D):
        z_ref[:, j * D:(j + 1) * D] = jnp.dot(a, wz_ref[:, j * D:(j + 1) * D], preferred_element_type=f32).astype(bf16)
    dt_ref[...] = jnp.dot(a, wdt_ref[...], preferred_element_type=f32)

    cw = CONV_COLS
    row = lax.broadcasted_iota(jnp.int32, (TM, cw), 0)
    first = row == 0
    last = row == TM - 1
    no_prev = is_ctx & (row == CTX)
    no_next = is_ctx & (row == CTX - 1)
    for j in range(SSD_CONV_CH // cw):
        cols = slice(j * cw, (j + 1) * cw)
        x = jnp.dot(a, wx_ref[:, cols], preferred_element_type=f32)
        xh = jnp.dot(a_halo, wx_ref[:, cols], preferred_element_type=f32)
        prev_row = jnp.where(seg_start, 0.0, xh[HALO - 1:HALO, :])
        next_row = jnp.where(seg_end, 0.0, xh[HALO:HALO + 1, :])
        xm1 = jnp.where(first, prev_row, pltpu.roll(x, 1, 0))
        xp1 = jnp.where(last, next_row, pltpu.roll(x, TM - 1, 0))
        xm1 = jnp.where(no_prev, 0.0, xm1)
        xp1 = jnp.where(no_next, 0.0, xp1)
        y = cw_ref[0:1, cols] * xm1 + cw_ref[1:2, cols] * x + cw_ref[2:3, cols] * xp1 + cb_ref[:, cols]
        x_ref[:, cols] = _silu(y).astype(bf16)


def _ssd_in_proj(h, mod, g, wz, wx, wdt, conv_w, conv_b):
    per = TM // HALO
    last_blk = T // HALO - 1
    return pl.pallas_call(
        _ssd_in_kernel,
        grid=(NT,),
        in_specs=[_tile_spec(D),
                  pl.BlockSpec((HALO, D), lambda t: (jnp.maximum(t * per - 1, 0), 0)),
                  pl.BlockSpec((HALO, D), lambda t: (jnp.minimum((t + 1) * per, last_blk), 0)),
                  _mod_spec(), _const_spec((1, D)), _const_spec((D, SSD_INNER)),
                  _const_spec((D, SSD_CONV_CH)), _const_spec((D, 256)),
                  _const_spec((8, SSD_CONV_CH)), _const_spec((1, SSD_CONV_CH))],
        out_specs=[_tile_spec(SSD_INNER), _tile_spec(SSD_CONV_CH), _tile_spec(256)],
        out_shape=[jax.ShapeDtypeStruct((T, SSD_INNER), bf16), jax.ShapeDtypeStruct((T, SSD_CONV_CH), bf16),
                   jax.ShapeDtypeStruct((T, 256), f32)],
        compiler_params=_params(("parallel",)),
        name="ssd_in_proj",
    )(h, h, h, mod, g, wz, wx, wdt, conv_w, conv_b)


def _ssd_scan_kernel(xf_ref, bf_ref, cf_ref, dtf_ref, xb_ref, bb_ref, cb_ref, dtb_ref, bias_ref, alog_ref,
                     yf_ref, yb_ref, state_sc):
    @pl.when(pl.program_id(1) == 0)
    def _():
        state_sc[...] = jnp.zeros_like(state_sc)

    _ssd_chunk(0, xf_ref, bf_ref, cf_ref, dtf_ref, bias_ref, alog_ref, yf_ref, state_sc)
    _ssd_chunk(1, xb_ref, bb_ref, cb_ref, dtb_ref, bias_ref, alog_ref, yb_ref, state_sc)


def _ssd_chunk(d, x_ref, b_ref, c_ref, dt_ref, bias_ref, alog_ref, y_ref, state_sc):
    L = SSD_CHUNK
    li = lax.broadcasted_iota(jnp.int32, (L, L), 0)
    si = lax.broadcasted_iota(jnp.int32, (L, L), 1)
    causal = (si <= li) if d == 0 else (si >= li)
    tri = jnp.where(causal, 1.0, 0.0).astype(bf16)
    hi = lax.broadcasted_iota(jnp.int32, (128, SSD_INNER), 0)
    ci = lax.broadcasted_iota(jnp.int32, (128, SSD_INNER), 1)
    expand = jnp.where(ci // SSD_P == hi, 1.0, 0.0).astype(bf16)

    dt = jax.nn.softplus(dt_ref[...] + bias_ref[d])
    a_neg = -jnp.exp(alog_ref[d])
    da = dt * a_neg
    da_parts = _split3(da)
    cs = (jnp.dot(tri, da_parts[0], preferred_element_type=f32) + jnp.dot(tri, da_parts[1], preferred_element_type=f32)
          + jnp.dot(tri, da_parts[2], preferred_element_type=f32))
    cs_t = cs.T
    total = jnp.sum(da, axis=0, keepdims=True)

    e_out = jnp.exp(cs)
    e_in = jnp.exp(total - cs) * dt
    dt_x = jnp.dot(dt.astype(bf16), expand, preferred_element_type=f32).astype(bf16)
    e_in_x = jnp.dot(e_in.astype(bf16), expand, preferred_element_type=f32).astype(bf16)
    out_scale = jnp.dot(e_out.astype(bf16), expand, preferred_element_type=f32)
    chunk_decay = _dot3(_split_bf16(jnp.broadcast_to(jnp.exp(total), (8, 128))), expand)[0:1]

    x = x_ref[...]
    first_head = lax.broadcasted_iota(jnp.int32, (L, 2 * SSD_P), 1) < SSD_P
    xdt = x * dt_x
    xw = x * e_in_x

    for g in range(SSD_GROUPS):
        bg = b_ref[:, g * SSD_N:(g + 1) * SSD_N]
        cg = c_ref[:, g * SSD_N:(g + 1) * SSD_N]
        cb = lax.dot_general(cg, bg, (((1,), (1,)), ((), ())), preferred_element_type=f32)
        gsl = slice(g * SSD_HPG * SSD_P, (g + 1) * SSD_HPG * SSD_P)
        st = state_sc[d, g]
        y_off = jnp.dot(cg, st.astype(bf16), preferred_element_type=f32) * out_scale[:, gsl]
        def decay_matrix(hd):
            seg = cs[:, hd:hd + 1] - cs_t[hd:hd + 1, :]
            return (cb * jnp.exp(jnp.where(causal, seg, -jnp.inf))).astype(bf16)

        ys = []
        for r in range(0, SSD_HPG, 2):
            hd = g * SSD_HPG + r
            m2 = jnp.concatenate([decay_matrix(hd), decay_matrix(hd + 1)], axis=1)
            x2 = xdt[:, hd * SSD_P:(hd + 2) * SSD_P]
            rhs = jnp.concatenate([jnp.where(first_head, x2, 0), jnp.where(first_head, 0, x2)], axis=0)
            ys.append(jnp.dot(m2, rhs, preferred_element_type=f32))
        y_ref[:, gsl] = (jnp.concatenate(ys, axis=1) + y_off).astype(bf16)
        new = lax.dot_general(bg, xw[:, gsl], (((0,), (0,)), ((), ())), preferred_element_type=f32)
        state_sc[d, g] = st * chunk_decay[:, gsl] + new


SSD_NCHUNK = (SEQ + CTX) // SSD_CHUNK
SSD_CTX_CHUNKS = CTX // SSD_CHUNK
SSD_LAT_CHUNKS = SEQ // SSD_CHUNK


def _ssd_chunk_block(d, b, c):
    ctx_j = c if d == 0 else SSD_CTX_CHUNKS - 1 - c
    lat_j = c - SSD_CTX_CHUNKS if d == 0 else SSD_LAT_CHUNKS - 1 - (c - SSD_CTX_CHUNKS)
    return jnp.where(c < SSD_CTX_CHUNKS, N_LAT // SSD_CHUNK + b * SSD_CTX_CHUNKS + ctx_j, b * SSD_LAT_CHUNKS + lat_j)


def _ssd_scan(xbc, dt, dt_bias, a_log):
    x_blk = SSD_INNER // SSD_BC

    def chunk_specs(d):
        return [
            pl.BlockSpec((SSD_CHUNK, SSD_INNER), lambda b, c: (_ssd_chunk_block(d, b, c), 0)),
            pl.BlockSpec((SSD_CHUNK, SSD_BC), lambda b, c: (_ssd_chunk_block(d, b, c), x_blk)),
            pl.BlockSpec((SSD_CHUNK, SSD_BC), lambda b, c: (_ssd_chunk_block(d, b, c), x_blk + 1)),
            pl.BlockSpec((SSD_CHUNK, 128), lambda b, c: (_ssd_chunk_block(d, b, c), d)),
        ]

    def y_spec(d):
        return pl.BlockSpec((SSD_CHUNK, SSD_INNER), lambda b, c: (_ssd_chunk_block(d, b, c), 0))

    return pl.pallas_call(
        _ssd_scan_kernel,
        grid=(BATCH, SSD_NCHUNK),
        in_specs=chunk_specs(0) + chunk_specs(1) + [_const_spec((2, 1, 128)), _const_spec((2, 1, 128))],
        out_specs=[y_spec(0), y_spec(1)],
        out_shape=[jax.ShapeDtypeStruct((T, SSD_INNER), bf16)] * 2,
        scratch_shapes=[pltpu.VMEM((2, SSD_GROUPS, SSD_N, SSD_HPG * SSD_P), f32)],
        compiler_params=_params(("parallel", "arbitrary")),
        name="ssd_scan",
    )(xbc, xbc, xbc, dt, xbc, xbc, xbc, dt, dt_bias, a_log)


def _ssd_out_kernel(yf_ref, yb_ref, x_ref, z_ref, dskip_ref, ng_ref, w_ref, h_ref, mod_ref, o_ref):
    y = yf_ref[...].astype(f32) + yb_ref[...].astype(f32) + x_ref[...].astype(f32) * dskip_ref[...]
    gated = y * _silu(z_ref[...].astype(f32))
    gw = SSD_INNER // SSD_GROUPS
    parts = []
    for g in range(SSD_GROUPS):
        s = gated[:, g * gw:(g + 1) * gw]
        parts.append(s * lax.rsqrt(jnp.mean(s * s, axis=-1, keepdims=True) + EPS))
    normed = (jnp.concatenate(parts, axis=1) * ng_ref[...]).astype(bf16)
    out = jnp.dot(normed, w_ref[...], preferred_element_type=f32)
    o_ref[...] = h_ref[...] + mod_ref[2:3, :] * out


def _ssd_out(yf, yb, xbc, z, dskip, ng, w_out, h, mod, nt):
    return pl.pallas_call(
        _ssd_out_kernel,
        grid=(nt,),
        in_specs=[
            _tile_spec(SSD_INNER), _tile_spec(SSD_INNER), _tile_spec(SSD_INNER),
            _tile_spec(SSD_INNER), _const_spec((1, SSD_INNER)), _const_spec((1, SSD_INNER)),
            _const_spec((SSD_INNER, D)), _tile_spec(D), _mod_spec(),
        ],
        out_specs=_tile_spec(D),
        out_shape=jax.ShapeDtypeStruct((nt * TM, D), f32),
        compiler_params=_params(("parallel",)),
        name="ssd_out",
    )(yf, yb, xbc, z, dskip, ng, w_out, h, mod)


def _gelu(x):
    return 0.5 * x * (1.0 + lax.erf(x * math.sqrt(0.5)))


def _cmlp_kernel(h_ref, mod_ref, g_ref, wu_ref, wv_ref, bu_ref, bv_ref, vg_ref, ws_ref, bs_ref, wo_ref, o_ref, uv_sc):
    a = _norm_mod(h_ref[...], g_ref[...], mod_ref[0:1, :], mod_ref[1:2, :]).astype(bf16)
    v = _gelu(jnp.dot(a, wv_ref[...], preferred_element_type=f32) + bv_ref[...])
    v = (v * lax.rsqrt(jnp.mean(v * v, axis=-1, keepdims=True) + EPS) * vg_ref[...]).astype(bf16)
    u = _gelu(jnp.dot(a, wu_ref[...], preferred_element_type=f32) + bu_ref[...])
    for ck in range(TM // CMLP_CHUNK):
        rows = slice(ck * CMLP_CHUNK, (ck + 1) * CMLP_CHUNK)
        for g in range(CMLP_GROUPS):
            cols = slice(g * CMLP_GW, (g + 1) * CMLP_GW)
            mixed = jnp.dot(ws_ref[g], v[rows, cols], preferred_element_type=f32) + bs_ref[:, cols]
            uv_sc[rows, cols] = (u[rows, cols] * mixed).astype(bf16)
    out = jnp.dot(uv_sc[...], wo_ref[...], preferred_element_type=f32)
    o_ref[...] = h_ref[...] + mod_ref[2:3, :] * out


def _cmlp(h, mod, g, wu, wv, bu, bv, vg, ws, bs, wo, nt):
    return pl.pallas_call(
        _cmlp_kernel,
        grid=(nt,),
        in_specs=[
            _tile_spec(D), _mod_spec(), _const_spec((1, D)), _const_spec((D, CMLP_D)), _const_spec((D, CMLP_D)),
            _const_spec((1, CMLP_D)), _const_spec((1, CMLP_D)), _const_spec((1, CMLP_D)),
            _const_spec((CMLP_GROUPS, CMLP_CHUNK, CMLP_CHUNK)), _const_spec((CMLP_CHUNK, CMLP_D)),
            _const_spec((CMLP_D, D)),
        ],
        out_specs=_tile_spec(D),
        out_shape=jax.ShapeDtypeStruct((nt * TM, D), f32),
        scratch_shapes=[pltpu.VMEM((TM, CMLP_D), bf16)],
        compiler_params=_params(("parallel",)),
        name="cmlp",
    )(h, mod, g, wu, wv, bu, bv, vg, ws, bs, wo)


def _rope_tables():
    pos = np.arange(SEQ)
    inv_freq = (1.0 / (np.float32(ROPE_THETA) ** (np.arange(0, ROPE_AXIS_DIM, 2, dtype=np.float32) / ROPE_AXIS_DIM)))
    inv_freq = inv_freq.astype(np.float32)
    ang_r = (pos // GRID_W).astype(np.float32)[:, None] * inv_freq
    ang_c = (pos % GRID_W).astype(np.float32)[:, None] * inv_freq
    cos = np.concatenate([np.cos(ang_r)] * 2 + [np.cos(ang_c)] * 2, axis=1)
    sin = np.concatenate([-np.sin(ang_r), np.sin(ang_r), -np.sin(ang_c), np.sin(ang_c)], axis=1)
    cos = np.concatenate([cos, np.ones((TM, HEAD_DIM), np.float32)], axis=0).astype(np.float32)
    sin = np.concatenate([sin, np.zeros((TM, HEAD_DIM), np.float32)], axis=0).astype(np.float32)
    return jnp.asarray(cos), jnp.asarray(sin)


def _row(v):
    return v.reshape(1, -1)


def _pad_lanes(v, width=128):
    return jnp.pad(v, ((0, 0), (0, width - v.shape[1])))


def kernel(x, c, ctx, c_ctx, w_mod, b_mod, norm1_g, norm2_g, attn_w_qkv, attn_q_g, attn_k_g, attn_w_o, ssd_w_in, ssd_conv_w, ssd_conv_b, ssd_dt_bias_f, ssd_dt_bias_b, ssd_a_log_f, ssd_a_log_b, ssd_d_skip, ssd_norm_g, ssd_w_out, cmlp_w_in, cmlp_b_in, cmlp_v_g, cmlp_w_s, cmlp_b_s, cmlp_w_out, ffn_w_gate, ffn_w_up, ffn_w_down, moe_router, moe_w_gate, moe_w_up, moe_w_down, final_g):
    h = None
    h_pair = (x.reshape(N_LAT, D), ctx.reshape(N_CTX, D))
    cond = jnp.concatenate([c, c_ctx[None, :], jnp.zeros((8 - BATCH - 1, D), f32)], axis=0)
    mods = _modulation(cond, w_mod, b_mod)
    cos_t, sin_t = _rope_tables()
    ffn_w = None
    moe_w = None
    final_row = _row(final_g)

    for i in range(DEPTH):
        need_ctx = i < DEPTH - 1
        nt = NT if need_ctx else NT_LAT
        mod = mods[i]
        kind, j = i % 3, i // 3
        g1 = _row(norm1_g[i])
        if kind == 0:
            w_qkv = attn_w_qkv[j]
            w_partner = _rope_partner(w_qkv[:, :(N_HEADS + N_KV) * HEAD_DIM]).astype(bf16)
            q_g = jnp.stack([attn_q_g[j], _rope_partner(attn_q_g[j])])
            k_g = jnp.stack([attn_k_g[j], _rope_partner(attn_k_g[j])])
            q, k, v = _qkv_proj(h_pair, mod, g1, w_qkv.astype(bf16), w_partner, q_g, k_g, cos_t, sin_t)
            moe_layer = i if i % 2 == 1 else i + 1
            nxt = moe_layer // 2
            casts = [(w, nxt) for w in (moe_w_gate, moe_w_up, moe_w_down)]
            if ffn_w is None:
                casts += [(w, None) for w in (ffn_w_gate, ffn_w_up, ffn_w_down)]
            o, cast_w = _attention(q, k, v, casts)
            moe_w = cast_w[:3]
            if ffn_w is None:
                ffn_w = cast_w[3:]
            w_o = attn_w_o[j].astype(bf16)
            if need_ctx:
                h = _proj_res2((o, _attention_ctx(q, k, v)), w_o, h_pair, mod)
            else:
                h = _proj_res(o, w_o, h, mod, nt)
        elif kind == 1:
            w_in = ssd_w_in[j]
            wz = w_in[:, :SSD_INNER].astype(bf16)
            wx = w_in[:, SSD_INNER:SSD_INNER + SSD_CONV_CH].astype(bf16)
            w_dt = w_in[:, SSD_INNER + SSD_CONV_CH:]
            wdt = jnp.concatenate([_pad_lanes(w_dt[:, :SSD_HEADS]), _pad_lanes(w_dt[:, SSD_HEADS:])], axis=1).astype(bf16)
            conv_w = jnp.pad(ssd_conv_w[j], ((0, 8 - ssd_conv_w.shape[1]), (0, 0)))
            z, xbc, dt = _ssd_in_proj(h, mod, g1, wz, wx, wdt, conv_w, _row(ssd_conv_b[j]))
            dt_bias = jnp.stack([_pad_lanes(_row(ssd_dt_bias_f[j])), _pad_lanes(_row(ssd_dt_bias_b[j]))])
            a_log = jnp.stack([_pad_lanes(_row(ssd_a_log_f[j])), _pad_lanes(_row(ssd_a_log_b[j]))])
            yf, yb = _ssd_scan(xbc, dt, dt_bias, a_log)
            dskip = _row(jnp.repeat(ssd_d_skip[j], SSD_P))
            h = _ssd_out(yf, yb, xbc, z, dskip, _row(ssd_norm_g[j]), ssd_w_out[j].astype(bf16), h, mod, nt)
        else:
            w_in = cmlp_w_in[j]
            b_in = cmlp_b_in[j]
            bs = jnp.repeat(cmlp_b_s[j].T, CMLP_GW, axis=1)
            h = _cmlp(h, mod, g1, w_in[:, :CMLP_D].astype(bf16), w_in[:, CMLP_D:].astype(bf16),
                      _row(b_in[:CMLP_D]), _row(b_in[CMLP_D:]), _row(cmlp_v_g[j]), cmlp_w_s[j].astype(bf16), bs,
                      cmlp_w_out[j].astype(bf16), nt)
        kk = i // 2
        g2 = _row(norm2_g[i])
        if i % 2 == 0:
            h = _ffn(h, mod, g2, *ffn_w, kk, nt)
        else:
            h = _moe(h, mod, g2, _pad_lanes(moe_router[kk]), *moe_w, nt, final_row, i == DEPTH - 1)
        h_pair = (h, h)
    return h.reshape(BATCH, SEQ, D)
```
